```python
import jax, jax.numpy as jnp
from jax import lax
import numpy as np

D_MODEL = 2048
BATCH = 4
SEQ = 2048
DEPTH = 4
DEC_BATCH = 8
DEC_SEQ = 1
PAST_LEN = 16384
PAGE_SIZE = 128

HEAD_DIM = 64
GROUP_WIDTH = D_MODEL // 4
N_HEADS = GROUP_WIDTH // HEAD_DIM
MIX_WIDTH = 4 * GROUP_WIDTH
D_FF = 4 * D_MODEL
RMS_EPS = 1e-6
GN_EPS = 1e-5
NEG_INF = -1e30
ATTN_SCALE = HEAD_DIM ** -0.5

MOBA_BLOCK = 256
MOBA_TOPK = 3
MOBA_Q_BLOCK = 32

NSA_KV_HEADS = 2
NSA_GROUP = N_HEADS // NSA_KV_HEADS
NSA_KV_WIDTH = NSA_KV_HEADS * HEAD_DIM
CMP_LEN = 32
CMP_STRIDE = 16
CMP_HIDDEN = 2 * HEAD_DIM
SLC_BLOCK = 64
SLC_TOPK = 16
WINDOW = 512
NSA_Q_BLOCK = 64

RWKV_DECAY_RANK = 64
RWKV_AAA_RANK = 64
RWKV_GATE_RANK = 128
RWKV_LN_EPS = 64e-5

RET_CHUNK = 128
ROPE_BASE = 10000.0

MOBA_COLS = 3 * GROUP_WIDTH
NSA_COLS = GROUP_WIDTH + 6 * NSA_KV_WIDTH + 3 * N_HEADS
RWKV_COLS = 3 * GROUP_WIDTH + RWKV_DECAY_RANK + RWKV_AAA_RANK + RWKV_GATE_RANK
RET_COLS = 4 * GROUP_WIDTH
IN_COLS = MOBA_COLS + NSA_COLS + RWKV_COLS + RET_COLS

kernel_name = 'hybrid_moba_nsa_rwkv7_retnet_step'


def rmsnorm(x, g):
    xf = x.astype(jnp.float32)
    y = xf * lax.rsqrt(jnp.mean(xf * xf, axis=-1, keepdims=True) + RMS_EPS)
    return (y * g.astype(jnp.float32)).astype(x.dtype)


def head_norm(y, g, b, eps):
    yf = y.astype(jnp.float32)
    mu = jnp.mean(yf, axis=-1, keepdims=True)
    var = jnp.mean(jnp.square(yf - mu), axis=-1, keepdims=True)
    shp = y.shape[-2:]
    return (yf - mu) * lax.rsqrt(var + eps) * g.reshape(shp).astype(jnp.float32) + b.reshape(shp).astype(jnp.float32)


def masked_softmax(logits, mask):
    p = jax.nn.softmax(jnp.where(mask, logits.astype(jnp.float32), NEG_INF), axis=-1)
    return p * mask


def rotary(x, pos):
    half = HEAD_DIM // 2
    inv = ROPE_BASE ** (-jnp.arange(half, dtype=jnp.float32) / half)
    ang = pos.astype(jnp.float32)[:, None] * inv[None, :]
    cos = jnp.cos(ang)[None, :, None, :]
    sin = jnp.sin(ang)[None, :, None, :]
    x1 = x[..., :half].astype(jnp.float32)
    x2 = x[..., half:].astype(jnp.float32)
    return jnp.concatenate([x1 * cos - x2 * sin, x1 * sin + x2 * cos], axis=-1)


def sweep_queries(fn, q, q_pos, block):
    b, t = q.shape[:2]
    blk = block if t % block == 0 else t
    n = t // blk
    qs = jnp.moveaxis(q.reshape((b, n, blk) + q.shape[2:]), 1, 0)
    ps = q_pos.reshape(n, blk)
    out = lax.map(lambda a: fn(a[0], a[1]), (qs, ps))
    out = jnp.moveaxis(out, 0, 1)
    return out.reshape((b, t) + out.shape[3:])


def moba_attention(q, k_all, v_all, q_pos):
    b, L, h, dh = k_all.shape
    t = q.shape[1]
    nb = -(-L // MOBA_BLOCK)
    pad = ((0, 0), (0, nb * MOBA_BLOCK + t - L), (0, 0), (0, 0))
    kl = jnp.pad(k_all, pad)
    vl = jnp.pad(v_all, pad)
    k_mean = jnp.mean(kl[:, :nb * MOBA_BLOCK].astype(jnp.float32).reshape(b, nb, MOBA_BLOCK, h, dh), axis=2)
    n_top = min(MOBA_TOPK, nb)
    bi = jnp.arange(b)[:, None, None, None]
    hi = jnp.arange(h)[None, :, None, None]
    blk_ids = jnp.arange(nb)
    blk_off = jnp.arange(MOBA_BLOCK)

    def attend(qc, pc):
        tc = pc.shape[0]
        cur = pc // MOBA_BLOCK
        gate = jnp.einsum('bthd,bnhd->bhtn', qc, k_mean, preferred_element_type=jnp.float32)
        gate = jnp.where(blk_ids[None, None, None, :] < cur[None, None, :, None], gate, -jnp.inf)
        top_val, top_idx = lax.top_k(gate, n_top)
        rows = (top_idx[..., None] * MOBA_BLOCK + blk_off).reshape(b, h, tc, n_top * MOBA_BLOCK)
        k_sel = kl[bi, rows, hi]
        v_sel = vl[bi, rows, hi]
        l_sel = jnp.einsum('bthd,bhtmd->bhtm', qc, k_sel, preferred_element_type=jnp.float32)
        m_sel = jnp.repeat(jnp.isfinite(top_val), MOBA_BLOCK, axis=-1)
        start = cur[0] * MOBA_BLOCK
        slab = MOBA_BLOCK + tc - 1
        k_own = lax.dynamic_slice_in_dim(kl, start, slab, axis=1)
        v_own = lax.dynamic_slice_in_dim(vl, start, slab, axis=1)
        own_pos = start + jnp.arange(slab)
        m_own = (own_pos[None, :] <= pc[:, None]) & (own_pos[None, :] >= (cur * MOBA_BLOCK)[:, None])
        l_own = jnp.einsum('bthd,bshd->bhts', qc, k_own, preferred_element_type=jnp.float32)
        logits = jnp.concatenate([l_sel, l_own], axis=-1) * ATTN_SCALE
        mask = jnp.concatenate([m_sel, jnp.broadcast_to(m_own, (b, h, tc, slab))], axis=-1)
        p = masked_softmax(logits, mask)
        m = n_top * MOBA_BLOCK
        return (jnp.einsum('bhtm,bhtmd->bthd', p[..., :m], v_sel)
                + jnp.einsum('bhts,bshd->bthd', p[..., m:], v_own))

    return sweep_queries(attend, q, q_pos, MOBA_Q_BLOCK)


def nsa_compress(kv, pe, w1, b1, w2):
    b, L, g, dh = kv.shape
    n_chunk = L // CMP_STRIDE
    span = CMP_LEN // CMP_STRIDE
    n_cmp = n_chunk - span + 1
    ch = kv[:, :n_chunk * CMP_STRIDE].reshape(b, n_chunk, CMP_STRIDE, g, dh)
    seg = jnp.concatenate([ch[:, r:r + n_cmp] for r in range(span)], axis=2)
    seg = seg + pe[None, None, :, None, :]
    flat = jnp.moveaxis(seg, 3, 2).reshape(b, n_cmp, g, CMP_LEN * dh)
    return jax.nn.gelu(flat @ w1 + b1) @ w2


def nsa_attention(q, kc, vc, ks, vs, kw, vw, q_pos, win_base, cmp_pe, cmp_w1, cmp_b1, cmp_w2):
    b, L, g, dh = kc.shape
    ck = nsa_compress(kc, cmp_pe[0], cmp_w1[0], cmp_b1[0], cmp_w2[0])
    cv = nsa_compress(vc, cmp_pe[1], cmp_w1[1], cmp_b1[1], cmp_w2[1])
    n_cmp = ck.shape[1]
    cmp_end = jnp.arange(n_cmp) * CMP_STRIDE + CMP_LEN - 1
    n_slc = -(-L // SLC_BLOCK)
    spad = ((0, 0), (0, n_slc * SLC_BLOCK - L), (0, 0), (0, 0))
    ks_pad = jnp.pad(ks, spad)
    vs_pad = jnp.pad(vs, spad)
    n_sel = min(SLC_TOPK, n_slc)
    ratio = SLC_BLOCK // CMP_STRIDE
    lead = CMP_LEN // CMP_STRIDE - 1
    imp_pad = ((0, 0), (0, 0), (0, 0), (lead, ratio * n_slc + ratio - n_cmp))
    wpad = ((0, 0), (WINDOW, 0), (0, 0), (0, 0))
    kw_pad = jnp.pad(kw, wpad)
    vw_pad = jnp.pad(vw, wpad)
    bi = jnp.arange(b)[:, None, None, None]
    gi = jnp.arange(g)[None, :, None, None]
    slc_ids = jnp.arange(n_slc)
    slc_off = jnp.arange(SLC_BLOCK)

    def attend(qc, pc):
        tc = pc.shape[0]
        qg = qc.reshape(b, tc, g, NSA_GROUP, dh)
        l_cmp = jnp.einsum('btkgd,bnkd->bkgtn', qg, ck, preferred_element_type=jnp.float32) * ATTN_SCALE
        p_cmp = masked_softmax(l_cmp, cmp_end[None, :] <= pc[:, None])
        o_cmp = jnp.einsum('bkgtn,bnkd->btkgd', p_cmp, cv)
        imp = jnp.pad(p_cmp.sum(axis=2), imp_pad)
        imp = sum(imp[..., s:s + ratio * n_slc:ratio] for s in range(ratio + lead))
        cur = pc // SLC_BLOCK
        eligible = slc_ids[None, :] <= cur[:, None]
        forced = (slc_ids[None, :] == 0) | (slc_ids[None, :] == cur[:, None]) | (slc_ids[None, :] == cur[:, None] - 1)
        score = jnp.where(eligible, jnp.where(forced, jnp.inf, imp), -jnp.inf)
        top_val, top_idx = lax.top_k(score, n_sel)
        rows = top_idx[..., None] * SLC_BLOCK + slc_off
        m_sel = (top_val > -jnp.inf)[..., None] & (rows <= pc[None, None, :, None, None])
        rows = rows.reshape(b, g, tc, n_sel * SLC_BLOCK)
        k_sel = ks_pad[bi, rows, gi]
        v_sel = vs_pad[bi, rows, gi]
        l_sel = jnp.einsum('btkgd,bktmd->bkgtm', qg, k_sel, preferred_element_type=jnp.float32) * ATTN_SCALE
        p_sel = masked_softmax(l_sel, m_sel.reshape(b, g, tc, -1)[:, :, None])
        o_sel = jnp.einsum('bkgtm,bktmd->btkgd', p_sel, v_sel)
        start = pc[0] - win_base + 1
        span_w = WINDOW - 1 + tc
        k_win = lax.dynamic_slice_in_dim(kw_pad, start, span_w, axis=1)
        v_win = lax.dynamic_slice_in_dim(vw_pad, start, span_w, axis=1)
        w_pos = win_base - WINDOW + start + jnp.arange(span_w)
        dist = pc[:, None] - w_pos[None, :]
        m_win = (dist >= 0) & (dist < WINDOW) & (w_pos[None, :] >= win_base)
        l_win = jnp.einsum('btkgd,bskd->bkgts', qg, k_win, preferred_element_type=jnp.float32) * ATTN_SCALE
        p_win = masked_softmax(l_win, m_win)
        o_win = jnp.einsum('bkgts,bskd->btkgd', p_win, v_win)
        return jnp.stack([o_cmp, o_sel, o_win], axis=2).reshape(b, tc, 3, N_HEADS, dh)

    return sweep_queries(attend, q, q_pos, NSA_Q_BLOCK)


def rwkv7_time_mix(cols, shift_prev, s0, mu, w0, w2, a0, a2, g2, k_k, k_a, r_k, ln_g, ln_b):
    b, t, _ = cols.shape
    prev = jnp.concatenate([shift_prev[:, None].astype(cols.dtype), cols[:, :-1]], axis=1)
    mixed = (cols + (prev - cols) * mu).astype(jnp.float32)
    gw = GROUP_WIDTH
    o1 = 3 * gw
    o2 = o1 + RWKV_DECAY_RANK
    o3 = o2 + RWKV_AAA_RANK
    r, k, v = mixed[..., :gw], mixed[..., gw:2 * gw], mixed[..., 2 * gw:o1]
    xw, xa, xg = mixed[..., o1:o2], mixed[..., o2:o3], mixed[..., o3:]
    w_log = -jax.nn.softplus(-(w0 + jnp.tanh(xw) @ w2)) - 0.5
    decay = jnp.exp(-jnp.exp(w_log))
    a = jax.nn.sigmoid(a0 + xa @ a2)
    gate = jax.nn.sigmoid(xg) @ g2

    def heads(z):
        return z.reshape(b, t, N_HEADS, HEAD_DIM)

    r, k, v, decay, a = heads(r), heads(k), heads(v), heads(decay), heads(a)
    kk = k * k_k.reshape(N_HEADS, HEAD_DIM)
    kk = kk / jnp.maximum(jnp.linalg.norm(kk, axis=-1, keepdims=True), 1e-12)
    k = k * (1.0 + (a - 1.0) * k_a.reshape(N_HEADS, HEAD_DIM))

    def step(s, inp):
        r_t, w_t, k_t, v_t, kk_t, a_t = inp
        s = (s * w_t[:, :, None, :]
             - jnp.einsum('bhvk,bhk->bhv', s, kk_t)[..., None] * (kk_t * a_t)[:, :, None, :]
             + v_t[..., None] * k_t[:, :, None, :])
        return s, jnp.einsum('bhvk,bhk->bhv', s, r_t)

    xs = tuple(jnp.moveaxis(z, 1, 0) for z in (r, decay, k, v, kk, a))
    s_fin, ys = lax.scan(step, s0.astype(jnp.float32), xs)
    y = head_norm(jnp.moveaxis(ys, 0, 1), ln_g, ln_b, RWKV_LN_EPS)
    y = y + jnp.sum(r * k * r_k, axis=-1, keepdims=True) * v
    return y.reshape(b, t, gw) * gate, s_fin, cols[:, -1]


def retention(q, k, v, s0, log_gamma):
    b, t, h, _ = q.shape
    c = RET_CHUNK if t % RET_CHUNK == 0 else t
    n = t // c
    idx = jnp.arange(c, dtype=jnp.float32)
    diff = idx[:, None] - idx[None, :]
    dmask = jnp.where(diff >= 0, jnp.exp(jnp.maximum(diff, 0.0)[None] * log_gamma[:, None, None]), 0.0)
    xi = jnp.exp((idx + 1.0)[:, None] * log_gamma[None, :])
    zeta = jnp.exp((c - 1.0 - idx)[:, None] * log_gamma[None, :])
    chunk_decay = jnp.exp(c * log_gamma)

    def chunks(z):
        return jnp.moveaxis(z.astype(jnp.float32).reshape((b, n, c) + z.shape[2:]), 1, 0)

    def step(s, inp):
        qc, kc, vc = inp
        att = jnp.einsum('bihd,bjhd->bhij', qc, kc) * dmask
        out = (jnp.einsum('bhij,bjhe->bihe', att, vc)
               + jnp.einsum('bihd,bhde->bihe', qc, s) * xi[None, :, :, None])
        s = s * chunk_decay[None, :, None, None] + jnp.einsum('bjhd,bjhe->bhde', kc * zeta[None, :, :, None], vc)
        return s, out

    s_fin, outs = lax.scan(step, s0.astype(jnp.float32), (chunks(q), chunks(k), chunks(v)))
    return jnp.moveaxis(outs, 0, 1).reshape(b, t, h, -1), s_fin


def trunk_layer(x, pos0, moba_past, nsa_past, win_buf, rwkv_s, rwkv_shift, ret_s, p):
    b, t, _ = x.shape
    q_pos = pos0 + jnp.arange(t, dtype=jnp.int32)
    h = rmsnorm(x, p['norm_g'][0])
    cols = h @ p['w_in']
    c_moba, c_nsa, c_rwkv, c_ret = jnp.split(
        cols, [MOBA_COLS, MOBA_COLS + NSA_COLS, MOBA_COLS + NSA_COLS + RWKV_COLS], axis=-1)

    def heads(z):
        return z.reshape(b, t, N_HEADS, HEAD_DIM)

    gw = GROUP_WIDTH
    q_a, k_a, v_a = heads(c_moba[..., :gw]), heads(c_moba[..., gw:2 * gw]), heads(c_moba[..., 2 * gw:])
    moba_new = jnp.stack([k_a, v_a], axis=2)
    moba_all = jnp.concatenate([moba_past.astype(moba_new.dtype), moba_new], axis=1)
    o_moba = moba_attention(q_a, moba_all[:, :, 0], moba_all[:, :, 1], q_pos)
    q_b = heads(c_nsa[..., :gw])
    kv_b = c_nsa[..., gw:gw + 6 * NSA_KV_WIDTH].reshape(b, t, 6, NSA_KV_HEADS, HEAD_DIM)
    gate_b = jax.nn.sigmoid(c_nsa[..., gw + 6 * NSA_KV_WIDTH:].astype(jnp.float32)).reshape(b, t, 3, N_HEADS)
    nsa_new = kv_b[:, :, :4]
    win_new = kv_b[:, :, 4:]
    nsa_all = jnp.concatenate([nsa_past.astype(nsa_new.dtype), nsa_new], axis=1)
    win_all = jnp.concatenate([win_buf.astype(win_new.dtype), win_new], axis=1)
    win_base = pos0 - win_buf.shape[1]
    branches = nsa_attention(q_b, nsa_all[:, :, 0], nsa_all[:, :, 1], nsa_all[:, :, 2], nsa_all[:, :, 3],
                             win_all[:, :, 0], win_all[:, :, 1], q_pos, win_base,
                             p['nsa_cmp_pe'], p['nsa_cmp_w1'], p['nsa_cmp_b1'], p['nsa_cmp_w2'])
    o_nsa = jnp.sum(gate_b[..., None] * branches, axis=2)
    o_rwkv, rwkv_s_new, rwkv_shift_new = rwkv7_time_mix(
        c_rwkv, rwkv_shift, rwkv_s, p['rwkv_mu'], p['rwkv_w0'], p['rwkv_w2'], p['rwkv_a0'], p['rwkv_a2'],
        p['rwkv_g2'], p['rwkv_k_k'], p['rwkv_k_a'], p['rwkv_r_k'], p['rwkv_ln_g'], p['rwkv_ln_b'])
    q_d = rotary(heads(c_ret[..., :gw]), q_pos)
    k_d = rotary(heads(c_ret[..., gw:2 * gw]), q_pos) * ATTN_SCALE
    v_d = heads(c_ret[..., 2 * gw:3 * gw])
    o_ret, ret_s_new = retention(q_d, k_d, v_d, ret_s, p['log_gamma'])
    o_ret = jax.nn.silu(c_ret[..., 3 * gw:].astype(jnp.float32)) * head_norm(
        o_ret, p['ret_gn_g'], p['ret_gn_b'], GN_EPS).reshape(b, t, gw)
    mix = jnp.concatenate([o_moba.reshape(b, t, gw), o_nsa.reshape(b, t, gw), o_rwkv, o_ret],
                          axis=-1).astype(x.dtype)
    x = x + rmsnorm(mix @ p['w_out'], p['norm_g'][1])
    hf = rmsnorm(x, p['norm_g'][2])
    f = jnp.square(jax.nn.relu(hf @ p['w_up'])) @ p['w_down']
    x = x + rmsnorm(f, p['norm_g'][3])
    win_keep = min(WINDOW, win_all.shape[1])
    return x, (moba_new, nsa_new, win_all[:, win_all.shape[1] - win_keep:], rwkv_s_new, rwkv_shift_new, ret_s_new)


def setup_inputs(seed: int = 0) -> dict:
    key = jax.random.key(seed)
    ks = jax.random.split(key, 40)
    n_pages = PAST_LEN // PAGE_SIZE
    n_used = DEC_BATCH * n_pages
    n_phys = (5 * n_used) // 4
    win_buf = min(WINDOW, PAST_LEN)

    def nrm(i, shape, scale):
        return jax.random.normal(ks[i], shape, jnp.float32) * scale

    page_table = jax.random.permutation(ks[0], n_phys)[:n_used].reshape(DEC_BATCH, n_pages).astype(jnp.int32)
    return {
        'x_prompt': nrm(1, (BATCH, SEQ, D_MODEL), 1.0),
        'x_sample': nrm(2, (DEC_BATCH, DEC_SEQ, D_MODEL), 1.0),
        'cache_moba_kv': nrm(3, (DEPTH, n_phys, PAGE_SIZE, 2, N_HEADS, HEAD_DIM), 1.0),
        'cache_nsa_kv': nrm(4, (DEPTH, n_phys, PAGE_SIZE, 4, NSA_KV_HEADS, HEAD_DIM), 1.0),
        'state_nsa_win': nrm(5, (DEPTH, DEC_BATCH, win_buf, 2, NSA_KV_HEADS, HEAD_DIM), 1.0),
        'state_rwkv': nrm(6, (DEPTH, DEC_BATCH, N_HEADS, HEAD_DIM, HEAD_DIM), 0.3),
        'state_rwkv_shift': nrm(7, (DEPTH, DEC_BATCH, RWKV_COLS), 1.0),
        'state_ret': nrm(8, (DEPTH, DEC_BATCH, N_HEADS, HEAD_DIM, HEAD_DIM), 0.3),
        'page_table': page_table,
        'norm_g': 1.0 + nrm(9, (DEPTH, 4, D_MODEL), 0.05),
        'w_in': nrm(10, (DEPTH, D_MODEL, IN_COLS), D_MODEL ** -0.5),
        'w_out': nrm(11, (DEPTH, MIX_WIDTH, D_MODEL), MIX_WIDTH ** -0.5),
        'w_up': nrm(12, (DEPTH, D_MODEL, D_FF), D_MODEL ** -0.5),
        'w_down': nrm(13, (DEPTH, D_FF, D_MODEL), D_FF ** -0.5),
        'nsa_cmp_pe': nrm(14, (DEPTH, 2, CMP_LEN, HEAD_DIM), 0.1),
        'nsa_cmp_w1': nrm(15, (DEPTH, 2, CMP_LEN * HEAD_DIM, CMP_HIDDEN), (CMP_LEN * HEAD_DIM) ** -0.5),
        'nsa_cmp_b1': nrm(16, (DEPTH, 2, CMP_HIDDEN), 0.01),
        'nsa_cmp_w2': nrm(17, (DEPTH, 2, CMP_HIDDEN, HEAD_DIM), CMP_HIDDEN ** -0.5),
        'rwkv_mu': jax.random.uniform(ks[18], (DEPTH, RWKV_COLS), jnp.float32, 0.0, 1.0),
        'rwkv_w0': jax.random.uniform(ks[19], (DEPTH, GROUP_WIDTH), jnp.float32, -5.0, 1.0),
        'rwkv_w2': nrm(20, (DEPTH, RWKV_DECAY_RANK, GROUP_WIDTH), 0.1),
        'rwkv_a0': nrm(21, (DEPTH, GROUP_WIDTH), 0.1),
        'rwkv_a2': nrm(22, (DEPTH, RWKV_AAA_RANK, GROUP_WIDTH), 0.5 * RWKV_AAA_RANK ** -0.5),
        'rwkv_g2': nrm(23, (DEPTH, RWKV_GATE_RANK, GROUP_WIDTH), RWKV_GATE_RANK ** -0.5),
        'rwkv_k_k': 0.85 + nrm(24, (DEPTH, GROUP_WIDTH), 0.05),
        'rwkv_k_a': 1.0 + nrm(25, (DEPTH, GROUP_WIDTH), 0.05),
        'rwkv_r_k': nrm(26, (DEPTH, N_HEADS, HEAD_DIM), 0.1),
        'rwkv_ln_g': 1.0 + nrm(27, (DEPTH, GROUP_WIDTH), 0.05),
        'rwkv_ln_b': nrm(28, (DEPTH, GROUP_WIDTH), 0.01),
        'ret_gn_g': 1.0 + nrm(29, (DEPTH, GROUP_WIDTH), 0.05),
        'ret_gn_b': nrm(30, (DEPTH, GROUP_WIDTH), 0.01),
    }


def reference(x_prompt, x_sample, cache_moba_kv, cache_nsa_kv, state_nsa_win, state_rwkv, state_rwkv_shift,
              state_ret, page_table, norm_g, w_in, w_out, w_up, w_down, nsa_cmp_pe, nsa_cmp_w1, nsa_cmp_b1,
              nsa_cmp_w2, rwkv_mu, rwkv_w0, rwkv_w2, rwkv_a0, rwkv_a2, rwkv_g2, rwkv_k_k, rwkv_k_a, rwkv_r_k,
              rwkv_ln_g, rwkv_ln_b, ret_gn_g, ret_gn_b):
    bp = x_prompt.shape[0]
    bs = x_sample.shape[0]
    past_len = page_table.shape[1] * PAGE_SIZE
    dt = x_prompt.dtype
    log_gamma = jnp.log(1.0 - jnp.exp2(-5.0 - jnp.arange(N_HEADS, dtype=jnp.float32)))
    empty_moba = jnp.zeros((bp, 0, 2, N_HEADS, HEAD_DIM), dt)
    empty_nsa = jnp.zeros((bp, 0, 4, NSA_KV_HEADS, HEAD_DIM), dt)
    empty_win = jnp.zeros((bp, 0, 2, NSA_KV_HEADS, HEAD_DIM), dt)
    zero_mat = jnp.zeros((bp, N_HEADS, HEAD_DIM, HEAD_DIM), jnp.float32)
    zero_shift = jnp.zeros((bp, RWKV_COLS), dt)
    y_prompt, y_sample = x_prompt, x_sample
    st_p, st_s = [], []
    for l in range(DEPTH):
        p = {'norm_g': norm_g[l], 'w_in': w_in[l], 'w_out': w_out[l], 'w_up': w_up[l], 'w_down': w_down[l],
             'nsa_cmp_pe': nsa_cmp_pe[l], 'nsa_cmp_w1': nsa_cmp_w1[l], 'nsa_cmp_b1': nsa_cmp_b1[l],
             'nsa_cmp_w2': nsa_cmp_w2[l], 'rwkv_mu': rwkv_mu[l], 'rwkv_w0': rwkv_w0[l], 'rwkv_w2': rwkv_w2[l],
             'rwkv_a0': rwkv_a0[l], 'rwkv_a2': rwkv_a2[l], 'rwkv_g2': rwkv_g2[l], 'rwkv_k_k': rwkv_k_k[l],
             'rwkv_k_a': rwkv_k_a[l], 'rwkv_r_k': rwkv_r_k[l], 'rwkv_ln_g': rwkv_ln_g[l], 'rwkv_ln_b': rwkv_ln_b[l],
             'ret_gn_g': ret_gn_g[l], 'ret_gn_b': ret_gn_b[l], 'log_gamma': log_gamma}
        y_prompt, new_p = trunk_layer(y_prompt, 0, empty_moba, empty_nsa, empty_win, zero_mat, zero_shift,
                                      zero_mat, p)
        moba_past = cache_moba_kv[l, page_table].reshape(bs, past_len, 2, N_HEADS, HEAD_DIM)
        nsa_past = cache_nsa_kv[l, page_table].reshape(bs, past_len, 4, NSA_KV_HEADS, HEAD_DIM)
        y_sample, new_s = trunk_layer(y_sample, past_len, moba_past, nsa_past, state_nsa_win[l], state_rwkv[l],
                                      state_rwkv_shift[l], state_ret[l], p)
        st_p.append(new_p)
        st_s.append(new_s)

    def stk(sts, i):
        return jnp.stack([s[i] for s in sts], axis=0)

    return (y_prompt, y_sample,
            stk(st_p, 0), stk(st_s, 0),
            stk(st_p, 1), stk(st_s, 1),
            stk(st_p, 2), stk(st_s, 2),
            stk(st_p, 3), stk(st_s, 3),
            stk(st_p, 4), stk(st_s, 4),
            stk(st_p, 5), stk(st_s, 5))
```

```python
import functools

import jax
import jax.numpy as jnp
import numpy as np
from jax import lax
from jax.experimental import pallas as pl
from jax.experimental.pallas import tpu as pltpu

F32 = jnp.float32
BF16 = jnp.bfloat16
I32 = jnp.int32
HI = lax.Precision.HIGHEST

D_MODEL = 2048
HEAD_DIM = 64
GROUP_WIDTH = D_MODEL // 4
N_HEADS = GROUP_WIDTH // HEAD_DIM
D_FF = 4 * D_MODEL
RMS_EPS = 1e-6
GN_EPS = 1e-5
NEG_INF = -1e30
ATTN_SCALE = HEAD_DIM ** -0.5
PAGE_SIZE = 128
MOBA_BLOCK = 256
MOBA_TOPK = 3
NSA_KV_HEADS = 2
NSA_GROUP = N_HEADS // NSA_KV_HEADS
NSA_KV_WIDTH = NSA_KV_HEADS * HEAD_DIM
CMP_LEN = 32
CMP_STRIDE = 16
CMP_HIDDEN = 2 * HEAD_DIM
SLC_BLOCK = 64
SLC_TOPK = 16
WINDOW = 512
RWKV_DECAY_RANK = 64
RWKV_AAA_RANK = 64
RWKV_GATE_RANK = 128
RWKV_LN_EPS = 64e-5
RET_CHUNK = 128
ROPE_BASE = 10000.0

MOBA_COLS = 3 * GROUP_WIDTH
NSA_COLS = GROUP_WIDTH + 6 * NSA_KV_WIDTH + 3 * N_HEADS
RWKV_COLS = 3 * GROUP_WIDTH + RWKV_DECAY_RANK + RWKV_AAA_RANK + RWKV_GATE_RANK
RET_COLS = 4 * GROUP_WIDTH
IN_COLS = MOBA_COLS + NSA_COLS + RWKV_COLS + RET_COLS

OFF_RWKV = 0
OFF_MOBA = OFF_RWKV + RWKV_COLS
OFF_NSAQ = OFF_MOBA + MOBA_COLS
OFF_NSAKV = OFF_NSAQ + GROUP_WIDTH
OFF_RET = OFF_NSAKV + 6 * NSA_KV_WIDTH
OFF_GATE = OFF_RET + RET_COLS
N_PAD = OFF_GATE + 512

LANE = 128
VMEM_LIMIT = 56 * 1024 * 1024


def _cparams(sem):
    return pltpu.CompilerParams(dimension_semantics=sem, vmem_limit_bytes=VMEM_LIMIT)


def _dot(a, b, precision=None):
    return jnp.dot(a, b, preferred_element_type=F32, precision=precision)


def _dot_nt(a, b, precision=None):
    return lax.dot_general(a, b, (((1,), (1,)), ((), ())), preferred_element_type=F32, precision=precision)


def _dot_tn(a, b, precision=None):
    return lax.dot_general(a, b, (((0,), (0,)), ((), ())), preferred_element_type=F32, precision=precision)


def _head_blockdiag(n, scale=1.0):
    r = lax.broadcasted_iota(I32, (n, n), 0) // HEAD_DIM
    c = lax.broadcasted_iota(I32, (n, n), 1) // HEAD_DIM
    return jnp.where(r == c, scale, 0.0).astype(F32)


def _sigmoid(x):
    return 1.0 / (1.0 + jnp.exp(-x))


def _inproj_body(x_ref, g_ref, w_ref, o_ref, h_scr):
    @pl.when(pl.program_id(1) == 0)
    def _():
        x = x_ref[...]
        ms = jnp.mean(x * x, axis=-1, keepdims=True)
        h_scr[...] = (x * lax.rsqrt(ms + RMS_EPS) * g_ref[...]).astype(BF16)

    o_ref[...] = _dot(h_scr[...], w_ref[...])


def _inproj(x, g, w, tm, tn):
    m, d = x.shape
    n = w.shape[1]
    return pl.pallas_call(
        _inproj_body,
        grid=(m // tm, n // tn),
        in_specs=[pl.BlockSpec((tm, d), lambda i, j: (i, 0)),
                  pl.BlockSpec((1, d), lambda i, j: (0, 0)),
                  pl.BlockSpec((d, tn), lambda i, j: (0, j))],
        out_specs=pl.BlockSpec((tm, tn), lambda i, j: (i, j)),
        out_shape=jax.ShapeDtypeStruct((m, n), F32),
        scratch_shapes=[pltpu.VMEM((tm, d), BF16)],
        compiler_params=_cparams(("parallel", "arbitrary")),
        name="inproj",
    )(x, g, w)


def _outproj_body(a_ref, b_ref, c_ref, d_ref, x_ref, g_ref, w_ref, o_ref):
    gw = GROUP_WIDTH
    y = _dot(a_ref[...], w_ref[0:gw, :])
    y += _dot(b_ref[...], w_ref[gw:2 * gw, :])
    y += _dot(c_ref[...], w_ref[2 * gw:3 * gw, :])
    y += _dot(d_ref[...], w_ref[3 * gw:4 * gw, :])
    ms = jnp.mean(y * y, axis=-1, keepdims=True)
    o_ref[...] = x_ref[...] + y * lax.rsqrt(ms + RMS_EPS) * g_ref[...]


def _outproj(parts, x, g, w, tm):
    m, d = x.shape
    gw = GROUP_WIDTH
    part_spec = pl.BlockSpec((tm, gw), lambda i: (i, 0))
    return pl.pallas_call(
        _outproj_body,
        grid=(m // tm,),
        in_specs=[part_spec, part_spec, part_spec, part_spec,
                  pl.BlockSpec((tm, d), lambda i: (i, 0)),
                  pl.BlockSpec((1, d), lambda i: (0, 0)),
                  pl.BlockSpec((4 * gw, d), lambda i: (0, 0))],
        out_specs=pl.BlockSpec((tm, d), lambda i: (i, 0)),
        out_shape=jax.ShapeDtypeStruct((m, d), F32),
        compiler_params=_cparams(("parallel",)),
        name="outproj",
    )(*parts, x, g, w)


def _ffn_body(x_ref, g2_ref, g3_ref, wu_ref, wd_ref, o_ref, h_scr, acc_scr):
    f = pl.program_id(1)

    @pl.when(f == 0)
    def _():
        x = x_ref[...]
        ms = jnp.mean(x * x, axis=-1, keepdims=True)
        h_scr[...] = (x * lax.rsqrt(ms + RMS_EPS) * g2_ref[...]).astype(BF16)
        acc_scr[...] = jnp.zeros_like(acc_scr)

    u = jnp.maximum(_dot(h_scr[...], wu_ref[...]), 0.0)
    acc_scr[...] += _dot((u * u).astype(BF16), wd_ref[...])

    @pl.when(f == pl.num_programs(1) - 1)
    def _():
        y = acc_scr[...]
        ms = jnp.mean(y * y, axis=-1, keepdims=True)
        o_ref[...] = x_ref[...] + y * lax.rsqrt(ms + RMS_EPS) * g3_ref[...]


def _ffn(x, g2, g3, wu, wd, tm, tf):
    m, d = x.shape
    f = wu.shape[1]
    return pl.pallas_call(
        _ffn_body,
        grid=(m // tm, f // tf),
        in_specs=[pl.BlockSpec((tm, d), lambda i, j: (i, 0)),
                  pl.BlockSpec((1, d), lambda i, j: (0, 0)),
                  pl.BlockSpec((1, d), lambda i, j: (0, 0)),
                  pl.BlockSpec((d, tf), lambda i, j: (0, j)),
                  pl.BlockSpec((tf, d), lambda i, j: (j, 0))],
        out_specs=pl.BlockSpec((tm, d), lambda i, j: (i, 0)),
        out_shape=jax.ShapeDtypeStruct((m, d), F32),
        scratch_shapes=[pltpu.VMEM((tm, d), BF16), pltpu.VMEM((tm, d), F32)],
        compiler_params=_cparams(("parallel", "arbitrary")),
        name="ffn",
    )(x, g2, g3, wu, wd)


def _moba_prompt_body(q_ref, k_ref, v_ref, o_ref, *, nb):
    blk = MOBA_BLOCK
    qt = pl.program_id(2)
    row = lax.broadcasted_iota(I32, (blk, blk), 0)
    col = lax.broadcasted_iota(I32, (blk, blk), 1)
    tril = (col <= row).astype(F32)
    blk_id = lax.broadcasted_iota(I32, (blk, nb), 1)
    eligible = blk_id < qt
    for j in range(2):
        sl = slice(HEAD_DIM * j, HEAD_DIM * (j + 1))
        q = q_ref[:, sl]
        kmean = jnp.concatenate(
            [jnp.sum(k_ref[n * blk:(n + 1) * blk, sl], axis=0, keepdims=True) for n in range(nb)],
            axis=0) * (1.0 / blk)
        gate = jnp.where(eligible, _dot_nt(q, kmean, HI), -jnp.inf)
        rank = jnp.zeros((blk, nb), I32)
        for m in range(nb):
            gm = gate[:, m:m + 1]
            rank += ((gm > gate) | ((gm == gate) & (m < blk_id))).astype(I32)
        sel = (eligible & (rank < MOBA_TOPK)).astype(F32)
        qb = q.astype(BF16)

        def body(n, carry):
            m_i, l_i, acc = carry
            start = pl.multiple_of(n * blk, blk)
            kb = k_ref[pl.ds(start, blk), sl].astype(BF16)
            vb = v_ref[pl.ds(start, blk), sl].astype(BF16)
            s = _dot_nt(qb, kb) * ATTN_SCALE
            sel_n = jnp.sum(jnp.where(blk_id == n, sel, 0.0), axis=-1, keepdims=True)
            mask = jnp.where(n == qt, tril, jnp.broadcast_to(sel_n, (blk, blk))) > 0.5
            s = jnp.where(mask, s, NEG_INF)
            m_new = jnp.maximum(m_i, jnp.max(s, axis=-1, keepdims=True))
            p = jnp.where(mask, jnp.exp(s - m_new), 0.0)
            alpha = jnp.exp(m_i - m_new)
            l_new = alpha * l_i + jnp.sum(p, axis=-1, keepdims=True)
            acc = alpha * acc + _dot(p.astype(BF16), vb)
            return m_new, l_new, acc

        init = (jnp.full((blk, 1), NEG_INF, F32), jnp.zeros((blk, 1), F32), jnp.zeros((blk, HEAD_DIM), F32))
        _, l_f, acc_f = lax.fori_loop(0, qt + 1, body, init)
        o_ref[:, sl] = (acc_f / l_f).astype(o_ref.dtype)


def _moba_prompt(cols, b, t):
    blk = MOBA_BLOCK
    nb = t // blk
    qoff = OFF_MOBA // LANE
    koff = (OFF_MOBA + GROUP_WIDTH) // LANE
    voff = (OFF_MOBA + 2 * GROUP_WIDTH) // LANE
    return pl.pallas_call(
        functools.partial(_moba_prompt_body, nb=nb),
        grid=(b, N_HEADS // 2, nb),
        in_specs=[pl.BlockSpec((blk, LANE), lambda i, h, q: (i * nb + q, qoff + h)),
                  pl.BlockSpec((t, LANE), lambda i, h, q: (i, koff + h)),
                  pl.BlockSpec((t, LANE), lambda i, h, q: (i, voff + h))],
        out_specs=pl.BlockSpec((blk, LANE), lambda i, h, q: (i * nb + q, h)),
        out_shape=jax.ShapeDtypeStruct((b * t, GROUP_WIDTH), BF16),
        compiler_params=_cparams(("parallel", "parallel", "arbitrary")),
        name="moba_prompt",
    )(cols, cols, cols)


def _gelu_tanh(x):
    return x * (0.5 * (1.0 + jnp.tanh(np.sqrt(2.0 / np.pi).astype(np.float32) * (x + 0.044715 * (x * x * x)))))


def _cmp_bias(pe_ref, w1_ref, b1_ref, kv):
    pe8 = jnp.broadcast_to(pe_ref[kv], (8, CMP_LEN * HEAD_DIM))
    return _dot(pe8, w1_ref[kv], HI)[0:1, :] + b1_ref[kv]


def _compress(load_rows, n, w1r_ref, kv, bias, w2):
    acc = [jnp.zeros((n, 2 * CMP_HIDDEN), F32) for _ in range(NSA_KV_HEADS)]
    for r in range(CMP_STRIDE):
        x = load_rows(r).astype(BF16)
        for g in range(NSA_KV_HEADS):
            acc[g] += _dot(x[:, g * HEAD_DIM:(g + 1) * HEAD_DIM], w1r_ref[kv, r])
    out = []
    for g in range(NSA_KV_HEADS):
        hid = acc[g][:, :CMP_HIDDEN] + pltpu.roll(acc[g][:, CMP_HIDDEN:], n - 1, 0) + bias
        out.append(_dot(_gelu_tanh(hid).astype(BF16), w2.astype(BF16)))
    return out


def _softmax_rows(l, mask):
    m = jnp.max(jnp.where(mask, l, NEG_INF), axis=-1, keepdims=True)
    e = jnp.where(mask, jnp.exp(l - m), 0.0)
    s = jnp.sum(e, axis=-1, keepdims=True)
    return jnp.where(s > 0.0, e / jnp.where(s > 0.0, s, 1.0), 0.0)


def _slc_matrix(n_cmp_pad, n_cmp, n_slc_pad):
    n = lax.broadcasted_iota(I32, (n_cmp_pad, n_slc_pad), 0)
    j = lax.broadcasted_iota(I32, (n_cmp_pad, n_slc_pad), 1)
    ratio = SLC_BLOCK // CMP_STRIDE
    return ((n >= ratio * j - 1) & (n <= ratio * j + ratio - 1) & (n < n_cmp)).astype(F32)


def _topk_rows(score, ids, n_cand, k):
    rank = jnp.zeros(score.shape, I32)
    for m in range(n_cand):
        sm = score[:, m:m + 1]
        rank += ((sm > score) | ((sm == score) & (m < ids))).astype(I32)
    return rank < k


def _nsa_prompt_body(q0_ref, q1_ref, kc_ref, vc_ref, ksvs_ref, kwvw_ref, gate_ref, w1r_ref, w1_ref, b1_ref, w2_ref,
                     pe_ref, o_ref, ck_scr, cv_scr, *, t):
    tq = 256
    n_chunk = t // CMP_STRIDE
    n_cmp = n_chunk - CMP_LEN // CMP_STRIDE + 1
    qt = pl.program_id(1)
    hd = HEAD_DIM

    @pl.when(qt == 0)
    def _():
        for kv, (src, dst) in enumerate(((kc_ref, ck_scr), (vc_ref, cv_scr))):
            bias = _cmp_bias(pe_ref, w1_ref, b1_ref, kv)
            out = _compress(lambda r: src[pl.ds(r, n_chunk, stride=CMP_STRIDE), :],
                            n_chunk, w1r_ref, kv, bias, w2_ref[kv])
            for g in range(NSA_KV_HEADS):
                dst[g] = out[g]

    tpos = qt * tq + lax.broadcasted_iota(I32, (tq, 1), 0)
    tpos4 = jnp.concatenate([tpos] * NSA_GROUP, axis=0)
    gates = _sigmoid(gate_ref[...])
    slc_ids = lax.broadcasted_iota(I32, (tq, LANE), 1)
    n_slc = t // SLC_BLOCK
    kcol = lax.broadcasted_iota(I32, (tq, tq), 1)
    jrow = lax.broadcasted_iota(I32, (LANE, tq), 0)
    kcol_e = lax.broadcasted_iota(I32, (LANE, tq), 1)
    for g in range(NSA_KV_HEADS):
        q_ref = q0_ref if g == 0 else q1_ref
        q4 = jnp.concatenate([q_ref[:, j * hd:(j + 1) * hd] for j in range(NSA_GROUP)], axis=0).astype(BF16)
        cmp_end = lax.broadcasted_iota(I32, (NSA_GROUP * tq, n_chunk), 1) * CMP_STRIDE + (CMP_LEN - 1)
        l_cmp = _dot_nt(q4, ck_scr[g].astype(BF16)) * ATTN_SCALE
        p_cmp = _softmax_rows(l_cmp, cmp_end <= tpos4)
        o_cmp = _dot(p_cmp.astype(BF16), cv_scr[g].astype(BF16))
        imp = p_cmp[0:tq]
        for j in range(1, NSA_GROUP):
            imp = imp + p_cmp[j * tq:(j + 1) * tq]
        imp_slc = _dot(imp, _slc_matrix(n_chunk, n_cmp, LANE), HI)
        cur = tpos // SLC_BLOCK
        eligible = (slc_ids <= cur) & (slc_ids < n_slc)
        forced = (slc_ids == 0) | (slc_ids == cur) | (slc_ids == cur - 1)
        score = jnp.where(eligible, jnp.where(forced, jnp.inf, imp_slc), -jnp.inf)
        sel = (eligible & _topk_rows(score, slc_ids, n_slc, SLC_TOPK)).astype(BF16)

        def sel_body(n, carry):
            m_i, l_i, acc = carry
            start = pl.multiple_of(n * tq, tq)
            kb = ksvs_ref[pl.ds(start, tq), g * hd:(g + 1) * hd].astype(BF16)
            vb = ksvs_ref[pl.ds(start, tq), NSA_KV_WIDTH + g * hd:NSA_KV_WIDTH + (g + 1) * hd].astype(BF16)
            expand = (jrow == (tq // SLC_BLOCK) * n + kcol_e // SLC_BLOCK).astype(BF16)
            mask = (_dot(sel, expand) > 0.5) & (n * tq + kcol <= tpos)
            mask4 = jnp.concatenate([mask.astype(F32)] * NSA_GROUP, axis=0) > 0.5
            s = jnp.where(mask4, _dot_nt(q4, kb) * ATTN_SCALE, NEG_INF)
            m_new = jnp.maximum(m_i, jnp.max(s, axis=-1, keepdims=True))
            p = jnp.where(mask4, jnp.exp(s - m_new), 0.0)
            alpha = jnp.exp(m_i - m_new)
            return (m_new, alpha * l_i + jnp.sum(p, axis=-1, keepdims=True),
                    alpha * acc + _dot(p.astype(BF16), vb))

        def win_body(n, carry):
            m_i, l_i, acc = carry
            start = pl.multiple_of(n * tq, tq)
            kb = kwvw_ref[pl.ds(start, tq), g * hd:(g + 1) * hd].astype(BF16)
            vb = kwvw_ref[pl.ds(start, tq), NSA_KV_WIDTH + g * hd:NSA_KV_WIDTH + (g + 1) * hd].astype(BF16)
            dist = tpos - (n * tq + kcol)
            mask = (dist >= 0) & (dist < WINDOW)
            mask4 = jnp.concatenate([mask.astype(F32)] * NSA_GROUP, axis=0) > 0.5
            s = jnp.where(mask4, _dot_nt(q4, kb) * ATTN_SCALE, NEG_INF)
            m_new = jnp.maximum(m_i, jnp.max(s, axis=-1, keepdims=True))
            p = jnp.where(mask4, jnp.exp(s - m_new), 0.0)
            alpha = jnp.exp(m_i - m_new)
            return (m_new, alpha * l_i + jnp.sum(p, axis=-1, keepdims=True),
                    alpha * acc + _dot(p.astype(BF16), vb))

        rows = NSA_GROUP * tq
        init = (jnp.full((rows, 1), NEG_INF, F32), jnp.zeros((rows, 1), F32), jnp.zeros((rows, hd), F32))
        _, l_s, acc_s = lax.fori_loop(0, qt + 1, sel_body, init)
        _, l_w, acc_w = lax.fori_loop(jnp.maximum(qt - (WINDOW // tq), 0), qt + 1, win_body, init)
        o_sel = acc_s / l_s
        o_win = acc_w / l_w
        for j in range(NSA_GROUP):
            h = g * NSA_GROUP + j
            rs = slice(j * tq, (j + 1) * tq)
            o = (gates[:, h:h + 1] * o_cmp[rs] + gates[:, N_HEADS + h:N_HEADS + h + 1] * o_sel[rs]
                 + gates[:, 2 * N_HEADS + h:2 * N_HEADS + h + 1] * o_win[rs])
            o_ref[:, h * hd:(h + 1) * hd] = o.astype(o_ref.dtype)


def _nsa_prompt(cols, cmp_w, b, t):
    w1r, w1, b1, w2, pe = cmp_w
    tq = 256
    nq = t // tq
    n_chunk = t // CMP_STRIDE
    qoff = OFF_NSAQ // 256
    kvoff = OFF_NSAKV // 256
    full = lambda a: pl.BlockSpec(a.shape, lambda i, q: (0,) * a.ndim)
    return pl.pallas_call(
        functools.partial(_nsa_prompt_body, t=t),
        grid=(b, nq),
        in_specs=[pl.BlockSpec((tq, 256), lambda i, q: (i * nq + q, qoff)),
                  pl.BlockSpec((tq, 256), lambda i, q: (i * nq + q, qoff + 1)),
                  pl.BlockSpec((t, LANE), lambda i, q: (i, 2 * kvoff)),
                  pl.BlockSpec((t, LANE), lambda i, q: (i, 2 * kvoff + 1)),
                  pl.BlockSpec((t, 256), lambda i, q: (i, kvoff + 1)),
                  pl.BlockSpec((t, 256), lambda i, q: (i, kvoff + 2)),
                  pl.BlockSpec((tq, LANE), lambda i, q: (i * nq + q, OFF_GATE // LANE)),
                  full(w1r), full(w1), full(b1), full(w2), full(pe)],
        out_specs=pl.BlockSpec((tq, GROUP_WIDTH), lambda i, q: (i * nq + q, 0)),
        out_shape=jax.ShapeDtypeStruct((b * t, GROUP_WIDTH), BF16),
        scratch_shapes=[pltpu.VMEM((NSA_KV_HEADS, n_chunk, HEAD_DIM), F32),
                        pltpu.VMEM((NSA_KV_HEADS, n_chunk, HEAD_DIM), F32)],
        compiler_params=_cparams(("parallel", "arbitrary")),
        name="nsa_prompt",
    )(cols, cols, cols, cols, cols, cols, cols, w1r, w1, b1, w2, pe)


def _rope(x, cos, sin):
    half = HEAD_DIM // 2
    lane = lax.broadcasted_iota(I32, x.shape, 1) % HEAD_DIM
    nxt = pltpu.roll(x, x.shape[1] - half, 1)
    prv = pltpu.roll(x, half, 1)
    return x * cos + jnp.where(lane < half, -nxt, prv) * sin


def _head_norm(y, g, b, eps):
    avg = _head_blockdiag(y.shape[1], 1.0 / HEAD_DIM)
    mu = _dot(y, avg, HI)
    d = y - mu
    var = _dot(d * d, avg, HI)
    return d * lax.rsqrt(var + eps) * g + b


def _ret_prompt_body(q_ref, k_ref, v_ref, gate_ref, cos_ref, sin_ref, dmask_ref, xi_ref, zeta_ref, cd_ref,
                     gn_g_ref, gn_b_ref, s0_ref, o_ref, s_ref, o_scr):
    @pl.when(pl.program_id(1) == 0)
    def _():
        s_ref[...] = s0_ref[...]

    cos = cos_ref[...]
    sin = sin_ref[...]
    q = _rope(q_ref[...], cos, sin)
    k = _rope(k_ref[...], cos, sin) * ATTN_SCALE
    kz = (k * zeta_ref[...]).astype(BF16)
    qb = q.astype(BF16)
    kb = k.astype(BF16)
    for h in range(N_HEADS):
        sl = slice(h * HEAD_DIM, (h + 1) * HEAD_DIM)
        vb = v_ref[:, sl].astype(BF16)
        s = s_ref[0, h]
        att = _dot_nt(qb[:, sl], kb[:, sl]) * dmask_ref[h]
        o_scr[:, sl] = _dot(att.astype(BF16), vb) + _dot(qb[:, sl], s.astype(BF16)) * xi_ref[:, sl]
        s_ref[0, h] = s * cd_ref[:, sl] + _dot_tn(kz[:, sl], vb)
    gate = gate_ref[...]
    y = _head_norm(o_scr[...], gn_g_ref[...], gn_b_ref[...], GN_EPS)
    o_ref[...] = (gate * _sigmoid(gate) * y).astype(o_ref.dtype)


def _ret_tables(log_gamma, c):
    idx = jnp.arange(c, dtype=F32)
    diff = idx[:, None] - idx[None, :]
    dmask = jnp.where(diff >= 0, jnp.exp(jnp.maximum(diff, 0.0)[None] * log_gamma[:, None, None]), 0.0)
    rep = lambda z: jnp.repeat(z, HEAD_DIM, axis=-1)
    xi = rep(jnp.exp((idx + 1.0)[:, None] * log_gamma[None, :]))
    zeta = rep(jnp.exp((c - 1.0 - idx)[:, None] * log_gamma[None, :]))
    cd = rep(jnp.exp(c * log_gamma)[None, :])
    return dmask, xi, zeta, cd


def _rope_tables(pos):
    half = HEAD_DIM // 2
    inv = ROPE_BASE ** (-jnp.arange(half, dtype=F32) / half)
    ang = pos.astype(F32)[:, None] * inv[None, :]
    tile = lambda z: jnp.tile(z, (1, 2 * N_HEADS))
    return tile(jnp.cos(ang)), tile(jnp.sin(ang))


def _ret_prompt(cols, s0, tables, rope, gn_g, gn_b, b, t):
    c = RET_CHUNK
    nc = t // c
    gw = GROUP_WIDTH
    off = OFF_RET // gw
    dmask, xi, zeta, cd = tables
    cos, sin = rope
    col_spec = lambda j: pl.BlockSpec((c, gw), lambda i, n: (i * nc + n, off + j))
    full = lambda a: pl.BlockSpec(a.shape, lambda i, n: (0,) * a.ndim)
    st_spec = pl.BlockSpec((1, N_HEADS, HEAD_DIM, HEAD_DIM), lambda i, n: (i, 0, 0, 0))
    return pl.pallas_call(
        _ret_prompt_body,
        grid=(b, nc),
        in_specs=[col_spec(0), col_spec(1), col_spec(2), col_spec(3),
                  pl.BlockSpec((c, gw), lambda i, n: (n, 0)), pl.BlockSpec((c, gw), lambda i, n: (n, 0)),
                  full(dmask), full(xi), full(zeta), full(cd), full(gn_g), full(gn_b), st_spec],
        out_specs=[pl.BlockSpec((c, gw), lambda i, n: (i * nc + n, 0)), st_spec],
        out_shape=[jax.ShapeDtypeStruct((b * t, gw), BF16),
                   jax.ShapeDtypeStruct((b, N_HEADS, HEAD_DIM, HEAD_DIM), F32)],
        scratch_shapes=[pltpu.VMEM((c, gw), F32)],
        compiler_params=_cparams(("parallel", "arbitrary")),
        name="ret_prompt",
    )(cols, cols, cols, cols, cos, sin, dmask, xi, zeta, cd, gn_g, gn_b, s0)


def _softplus(x):
    return jnp.maximum(x, 0.0) + jnp.log(1.0 + jnp.exp(-jnp.abs(x)))


def _rwkv_prep(c, prev, p):
    mu, w0, w2, a0, a2, g2, k_k, k_a = p
    gw = GROUP_WIDTH
    mixed = c + (prev - c) * mu
    r, k, v = mixed[:, :gw], mixed[:, gw:2 * gw], mixed[:, 2 * gw:3 * gw]
    o1 = 3 * gw
    o2 = o1 + RWKV_DECAY_RANK
    o3 = o2 + RWKV_AAA_RANK
    xw, xa, xg = mixed[:, o1:o2], mixed[:, o2:o3], mixed[:, o3:]
    w_log = -_softplus(-(w0 + _dot(jnp.tanh(xw), w2, HI))) - 0.5
    decay = jnp.exp(-jnp.exp(w_log))
    a = _sigmoid(a0 + _dot(xa, a2, HI))
    gate = _dot(_sigmoid(xg).astype(BF16), g2.astype(BF16))
    kk = k * k_k
    norm = jnp.sqrt(_dot(kk * kk, _head_blockdiag(gw), HI))
    kk = kk / jnp.maximum(norm, 1e-12)
    k = k * (1.0 + (a - 1.0) * k_a)
    return r, decay, k, v, kk, kk * a, gate


def _rwkv_steps(vecs, get_state, put_state, n_steps):
    r8, w8, k8, v8, kk8, ka8 = vecs
    lane = lax.broadcasted_iota(I32, (HEAD_DIM, LANE), 1)
    pad = jnp.zeros((LANE - 8, LANE), F32)
    ys = []
    for hp in range(N_HEADS // 2):
        vt = jnp.concatenate([v8[:, hp * LANE:(hp + 1) * LANE], pad], axis=0).T
        yts = []
        for h2 in range(2):
            h = 2 * hp + h2
            sl = slice(h * HEAD_DIM, (h + 1) * HEAD_DIM)
            yt = jnp.zeros((HEAD_DIM, LANE), F32)
            s = get_state(h, 0)
            for j in range(8):
                if n_steps == 1:
                    s = get_state(h, j)
                vcol = vt[h2 * HEAD_DIM:(h2 + 1) * HEAD_DIM, j:j + 1]
                sa = jnp.sum(s * kk8[j:j + 1, sl], axis=-1, keepdims=True)
                s = s * w8[j:j + 1, sl] - sa * ka8[j:j + 1, sl] + vcol * k8[j:j + 1, sl]
                ycol = jnp.sum(s * r8[j:j + 1, sl], axis=-1, keepdims=True)
                yt = jnp.where(lane == j, ycol, yt)
                if n_steps == 1:
                    put_state(h, j, s)
            if n_steps != 1:
                put_state(h, 0, s)
            yts.append(yt)
        ys.append(jnp.concatenate(yts, axis=0).T[0:8, :])
    return jnp.concatenate(ys, axis=1)


def _rwkv_post(ys, r, k, v, gate, r_k, ln_g, ln_b):
    y = _head_norm(ys, ln_g, ln_b, RWKV_LN_EPS)
    y = y + _dot(r * k * r_k, _head_blockdiag(GROUP_WIDTH), HI) * v
    return y * gate


def _rwkv_prompt_body(c_ref, shift_ref, mu_ref, w0_ref, w2_ref, a0_ref, a2_ref, g2_ref, kk_ref, ka_ref, rk_ref,
                      lng_ref, lnb_ref, s0_ref, o_ref, s_ref, carry, r_s, w_s, k_s, v_s, kk_s, kka_s, y_s):
    tc = c_ref.shape[0]

    @pl.when(pl.program_id(1) == 0)
    def _():
        s_ref[...] = s0_ref[...]
        carry[...] = shift_ref[...]

    c = c_ref[...]
    row = lax.broadcasted_iota(I32, c.shape, 0)
    prev = jnp.where(row == 0, carry[...], pltpu.roll(c, 1, 0))
    carry[...] = c[tc - 1:tc, :]
    p = (mu_ref[...], w0_ref[...], w2_ref[...], a0_ref[...], a2_ref[...], g2_ref[...], kk_ref[...], ka_ref[...])
    r, w, k, v, kk, kka, gate = _rwkv_prep(c, prev, p)
    r_s[...] = r
    w_s[...] = w
    k_s[...] = k
    v_s[...] = v
    kk_s[...] = kk
    kka_s[...] = kka

    def get_state(h, j):
        return s_ref[0, h]

    def put_state(h, j, s):
        s_ref[0, h] = s

    def group(i, _):
        rows = pl.ds(pl.multiple_of(i * 8, 8), 8)
        vecs = tuple(z[rows, :] for z in (r_s, w_s, k_s, v_s, kk_s, kka_s))
        y_s[rows, :] = _rwkv_steps(vecs, get_state, put_state, 8)
        return 0

    lax.fori_loop(0, tc // 8, group, 0)
    o_ref[...] = _rwkv_post(y_s[...], r, k, v, gate, rk_ref[...], lng_ref[...], lnb_ref[...]).astype(o_ref.dtype)


def _rwkv_prompt(cols, shift_prev, s0, params, b, t, tc=256):
    nt = t // tc
    gw = GROUP_WIDTH
    full = lambda a: pl.BlockSpec(a.shape, lambda i, n: (0,) * a.ndim)
    st_spec = pl.BlockSpec((1, N_HEADS, HEAD_DIM, HEAD_DIM), lambda i, n: (i, 0, 0, 0))
    vec = pltpu.VMEM((tc, gw), F32)
    return pl.pallas_call(
        _rwkv_prompt_body,
        grid=(b, nt),
        in_specs=[pl.BlockSpec((tc, RWKV_COLS), lambda i, n: (i * nt + n, OFF_RWKV // RWKV_COLS)),
                  pl.BlockSpec((None, 1, RWKV_COLS), lambda i, n: (i, 0, 0))]
                 + [full(a) for a in params] + [st_spec],
        out_specs=[pl.BlockSpec((tc, gw), lambda i, n: (i * nt + n, 0)), st_spec],
        out_shape=[jax.ShapeDtypeStruct((b * t, gw), BF16),
                   jax.ShapeDtypeStruct((b, N_HEADS, HEAD_DIM, HEAD_DIM), F32)],
        scratch_shapes=[pltpu.VMEM((1, RWKV_COLS), F32), vec, vec, vec, vec, vec, vec, vec],
        compiler_params=_cparams(("parallel", "arbitrary")),
        name="rwkv_prompt",
    )(cols, shift_prev.reshape(b, 1, RWKV_COLS), *params, s0)


def _past_stats_body(pt_ref, kt_ref, kct_ref, vct_ref, q_ref, w1r_ref, w1_ref, b1_ref, w2_ref, pe_ref,
                     top_ref, cmp_ref, qb, gsum, kc_rows, vc_rows, *, n_pages):
    p = pl.program_id(1)
    pages_per_blk = MOBA_BLOCK // PAGE_SIZE
    nb = n_pages // pages_per_blk
    n_chunk = n_pages * PAGE_SIZE // CMP_STRIDE

    @pl.when(p == 0)
    def _():
        gsum[...] = jnp.zeros_like(gsum)
        qb[...] = jnp.broadcast_to(q_ref[...], qb.shape)

    blk = p // pages_per_blk
    gsum[blk] += jnp.sum(kt_ref[...] * qb[...], axis=1)
    row0 = pl.multiple_of(p * PAGE_SIZE, PAGE_SIZE)
    kc_rows[pl.ds(row0, PAGE_SIZE), :] = kct_ref[...].reshape(LANE, PAGE_SIZE).T
    vc_rows[pl.ds(row0, PAGE_SIZE), :] = vct_ref[...].reshape(LANE, PAGE_SIZE).T

    @pl.when(p == n_pages - 1)
    def _():
        gate = jnp.sum(gsum[...], axis=-1) * (1.0 / MOBA_BLOCK)
        ids = lax.broadcasted_iota(I32, (nb, N_HEADS), 0)
        rows8 = lax.broadcasted_iota(I32, (8, N_HEADS), 0)
        top = jnp.zeros((8, N_HEADS), I32)
        for j in range(MOBA_TOPK):
            best = jnp.max(gate, axis=0, keepdims=True)
            arg = jnp.min(jnp.where(gate == best, ids, nb), axis=0, keepdims=True)
            top = jnp.where(rows8 == j, arg, top)
            gate = jnp.where(ids == arg, -jnp.inf, gate)
        top_ref[...] = top
        for kv, src in enumerate((kc_rows, vc_rows)):
            bias = _cmp_bias(pe_ref, w1_ref, b1_ref, kv)
            out = _compress(lambda r: src[pl.ds(r, n_chunk, stride=CMP_STRIDE), :], n_chunk, w1r_ref, kv, bias,
                            w2_ref[kv])
            for g in range(NSA_KV_HEADS):
                cmp_ref[kv * NSA_KV_HEADS + g] = out[g]


def _past_stats(page_table, moba_t, nsa_t, q_col, cmp_w, layer):
    w1r, w1, b1, w2, pe = cmp_w
    bs, n_pages = page_table.shape
    nb = n_pages * PAGE_SIZE // MOBA_BLOCK
    rows = n_pages * PAGE_SIZE
    n_chunk = rows // CMP_STRIDE
    full = lambda a: pl.BlockSpec(a.shape, lambda i, p, pt: (0,) * a.ndim)
    nsa_page = lambda t: pl.BlockSpec((None, None, None, NSA_KV_HEADS, HEAD_DIM, PAGE_SIZE),
                                      lambda i, p, pt: (layer, pt[i, p], t, 0, 0, 0))
    return pl.pallas_call(
        functools.partial(_past_stats_body, n_pages=n_pages),
        grid_spec=pltpu.PrefetchScalarGridSpec(
            num_scalar_prefetch=1,
            grid=(bs, n_pages),
            in_specs=[pl.BlockSpec((None, None, None, N_HEADS, HEAD_DIM, PAGE_SIZE),
                                   lambda i, p, pt: (layer, pt[i, p], 0, 0, 0, 0)),
                      nsa_page(0), nsa_page(1),
                      pl.BlockSpec((None, N_HEADS, HEAD_DIM, 1), lambda i, p, pt: (i, 0, 0, 0)),
                      full(w1r), full(w1), full(b1), full(w2), full(pe)],
            out_specs=[pl.BlockSpec((None, 8, N_HEADS), lambda i, p, pt: (i, 0, 0)),
                       pl.BlockSpec((None, 2 * NSA_KV_HEADS, n_chunk, HEAD_DIM), lambda i, p, pt: (i, 0, 0, 0))],
            scratch_shapes=[pltpu.VMEM((N_HEADS, HEAD_DIM, PAGE_SIZE), F32), pltpu.VMEM((nb, N_HEADS, PAGE_SIZE), F32),
                            pltpu.VMEM((rows, LANE), F32), pltpu.VMEM((rows, LANE), F32)]),
        out_shape=[jax.ShapeDtypeStruct((bs, 8, N_HEADS), I32),
                   jax.ShapeDtypeStruct((bs, 2 * NSA_KV_HEADS, n_chunk, HEAD_DIM), F32)],
        compiler_params=_cparams(("parallel", "arbitrary")),
        name="past_stats",
    )(page_table, moba_t, nsa_t, nsa_t, q_col, w1r, w1, b1, w2, pe)


def _moba_sample_body(pt_ref, top_ref, kt_ref, vt_ref, q_ref, kn_ref, vn_ref, o_ref, m_s, l_s, acc_s):
    j, r = pl.program_id(2), pl.program_id(3)
    q = q_ref[...]
    q8 = jnp.broadcast_to(q, (8, HEAD_DIM)).astype(BF16)

    @pl.when((j == 0) & (r == 0))
    def _():
        s_self = jnp.sum(q * kn_ref[...], axis=-1, keepdims=True) * ATTN_SCALE
        m_s[...] = jnp.broadcast_to(s_self, m_s.shape)
        l_s[...] = jnp.ones_like(l_s)
        acc_s[...] = jnp.broadcast_to(vn_ref[...], acc_s.shape)

    s = _dot(q8, kt_ref[...].astype(BF16)) * ATTN_SCALE
    m_i = m_s[:, 0:1]
    m_new = jnp.maximum(m_i, jnp.max(s, axis=-1, keepdims=True))
    pr = jnp.exp(s - m_new)
    alpha = jnp.exp(m_i - m_new)
    l_s[...] = alpha * l_s[...] + jnp.sum(pr, axis=-1, keepdims=True)
    acc_s[...] = alpha * acc_s[...] + _dot_nt(pr.astype(BF16), vt_ref[...].astype(BF16))
    m_s[...] = jnp.broadcast_to(m_new, m_s.shape)

    @pl.when((j == pl.num_programs(2) - 1) & (r == pl.num_programs(3) - 1))
    def _():
        o_ref[...] = (acc_s[...] / l_s[...])[0:1, :]


def _moba_sample(page_table, top, moba_t, q, k_new, v_new, layer):
    bs = page_table.shape[0]
    ppb = MOBA_BLOCK // PAGE_SIZE

    def page_spec(kv):
        return pl.BlockSpec((None, None, None, None, HEAD_DIM, PAGE_SIZE),
                            lambda i, h, j, r, pt, tp: (layer, pt[i, tp[i, j, h] * ppb + r], kv, h, 0, 0))

    head = pl.BlockSpec((None, None, 1, HEAD_DIM), lambda i, h, j, r, pt, tp: (i, h, 0, 0))
    st = pltpu.VMEM((8, HEAD_DIM), F32)
    return pl.pallas_call(
        _moba_sample_body,
        grid_spec=pltpu.PrefetchScalarGridSpec(
            num_scalar_prefetch=2,
            grid=(bs, N_HEADS, MOBA_TOPK, ppb),
            in_specs=[page_spec(0), page_spec(1), head, head, head],
            out_specs=head,
            scratch_shapes=[st, st, st]),
        out_shape=jax.ShapeDtypeStruct((bs, N_HEADS, 1, HEAD_DIM), F32),
        compiler_params=_cparams(("parallel", "parallel", "arbitrary", "arbitrary")),
        name="moba_sample",
    )(page_table, top, moba_t, moba_t, q, k_new, v_new)


def _heads_to_rows(row, g):
    parts = [row[:, (g * NSA_GROUP + j) * HEAD_DIM:(g * NSA_GROUP + j + 1) * HEAD_DIM] for j in range(NSA_GROUP)]
    return jnp.concatenate(parts + [jnp.zeros((8 - NSA_GROUP, HEAD_DIM), F32)], axis=0)


def _nsa_sample_sel_body(q_ref, cmp_ref, win_ref, kvn_ref, gate_ref, idx_ref, part_ref, *, past):
    hd = HEAD_DIM
    n_chunk = past // CMP_STRIDE
    n_slc = past // SLC_BLOCK + 1
    n_slc_pad = -(-n_slc // LANE) * LANE
    cur = past // SLC_BLOCK
    qrow = q_ref[...]
    gates = _sigmoid(gate_ref[...])
    kvn = kvn_ref[...]
    rows8 = lax.broadcasted_iota(I32, (8, 1), 0)
    ids = lax.broadcasted_iota(I32, (1, n_slc_pad), 1)
    m_iota = lax.broadcasted_iota(I32, (n_slc_pad, n_slc_pad), 0)
    j_iota = lax.broadcasted_iota(I32, (n_slc_pad, n_slc_pad), 1)
    lane = lax.broadcasted_iota(I32, (1, LANE), 1)
    idx_out = jnp.zeros((8, LANE), I32)
    for g in range(NSA_KV_HEADS):
        q8 = _heads_to_rows(qrow, g)
        q8b = q8.astype(BF16)
        n_ids = lax.broadcasted_iota(I32, (8, n_chunk), 1)
        l_cmp = _dot_nt(q8b, cmp_ref[g].astype(BF16)) * ATTN_SCALE
        p_cmp = _softmax_rows(l_cmp, n_ids * CMP_STRIDE + (CMP_LEN - 1) <= past)
        o_cmp = _dot(p_cmp.astype(BF16), cmp_ref[NSA_KV_HEADS + g].astype(BF16))
        imp = jnp.sum(jnp.where(rows8 < NSA_GROUP, p_cmp, 0.0), axis=0, keepdims=True)
        imp_slc = _dot(jnp.broadcast_to(imp, (8, n_chunk)), _slc_matrix(n_chunk, n_chunk - 1, n_slc_pad), HI)
        eligible = (ids <= cur) & (ids < n_slc)
        forced = (ids == 0) | (ids == cur) | (ids == cur - 1)
        score = jnp.where(eligible, jnp.where(forced, jnp.inf, imp_slc[0:1, :]), -jnp.inf)
        s_col = jnp.broadcast_to(score, (LANE, n_slc_pad)).T[:, 0:1]
        beats = (s_col > score) | ((s_col == score) & (m_iota < j_iota))
        rank = jnp.sum(beats.astype(F32), axis=0, keepdims=True)
        sel = jnp.where(eligible & (rank < SLC_TOPK), 1.0, 0.0)
        sel_col = jnp.broadcast_to(sel, (LANE, n_slc_pad)).T[:, 0:1]
        before = jnp.sum(jnp.where(m_iota < j_iota, sel_col, 0.0), axis=0, keepdims=True)
        idx_row = jnp.zeros((1, LANE), I32)
        for i in range(SLC_TOPK):
            hit = (sel > 0.5) & (before == i)
            idx_i = jnp.sum(jnp.where(hit, ids, 0), axis=-1, keepdims=True)
            idx_row = jnp.where(lane == i, idx_i, idx_row)
        idx_out = jnp.where(lax.broadcasted_iota(I32, (8, LANE), 0) == g, idx_row, idx_out)
        kw = win_ref[0, g].astype(BF16)
        vw = win_ref[1, g].astype(BF16)
        n_buf = win_ref.shape[-1]
        w_ids = lax.broadcasted_iota(I32, (8, n_buf), 1)
        l_win = _dot(q8b, kw) * ATTN_SCALE
        w_mask = w_ids >= n_buf - (WINDOW - 1)
        kw_new = kvn[:, 4 * NSA_KV_WIDTH + g * hd:4 * NSA_KV_WIDTH + (g + 1) * hd]
        vw_new = kvn[:, 5 * NSA_KV_WIDTH + g * hd:5 * NSA_KV_WIDTH + (g + 1) * hd]
        s_self = jnp.sum(q8 * kw_new, axis=-1, keepdims=True) * ATTN_SCALE
        m = jnp.maximum(jnp.max(jnp.where(w_mask, l_win, NEG_INF), axis=-1, keepdims=True), s_self)
        e = jnp.where(w_mask, jnp.exp(l_win - m), 0.0)
        e_self = jnp.exp(s_self - m)
        o_win = (_dot_nt(e.astype(BF16), vw) + e_self * vw_new) / (jnp.sum(e, axis=-1, keepdims=True) + e_self)
        for j in range(NSA_GROUP):
            h = g * NSA_GROUP + j
            part_ref[:, h * hd:(h + 1) * hd] = (gates[:, h:h + 1] * o_cmp[j:j + 1]
                                                + gates[:, 2 * N_HEADS + h:2 * N_HEADS + h + 1] * o_win[j:j + 1])
    idx_ref[...] = idx_out


def _nsa_sample_sel(q, cmp_tok, win_state, kv_new, gates, layer, past):
    bs = q.shape[0]
    n_buf = win_state.shape[-1]
    n_chunk = cmp_tok.shape[2]
    return pl.pallas_call(
        functools.partial(_nsa_sample_sel_body, past=past),
        grid=(bs,),
        in_specs=[pl.BlockSpec((None, 1, GROUP_WIDTH), lambda i: (i, 0, 0)),
                  pl.BlockSpec((None, 2 * NSA_KV_HEADS, n_chunk, HEAD_DIM), lambda i: (i, 0, 0, 0)),
                  pl.BlockSpec((None, None, 2, NSA_KV_HEADS, HEAD_DIM, n_buf), lambda i: (layer, i, 0, 0, 0, 0)),
                  pl.BlockSpec((None, 1, 6 * NSA_KV_WIDTH), lambda i: (i, 0, 0)),
                  pl.BlockSpec((None, 1, LANE), lambda i: (i, 0, 0))],
        out_specs=[pl.BlockSpec((None, 8, LANE), lambda i: (i, 0, 0)),
                   pl.BlockSpec((None, 1, GROUP_WIDTH), lambda i: (i, 0, 0))],
        out_shape=[jax.ShapeDtypeStruct((bs, 8, LANE), I32), jax.ShapeDtypeStruct((bs, 1, GROUP_WIDTH), F32)],
        compiler_params=_cparams(("parallel",)),
        name="nsa_sample_sel",
    )(q, cmp_tok, win_state, kv_new, gates)


def _nsa_sample_attn_body(pt_ref, idx_ref, ks0_ref, vs0_ref, ks1_ref, vs1_ref, q_ref, kvn_ref, gate_ref, part_ref,
                          o_ref, m_s, l_s, acc_s, *, n_past_blk):
    b, i = pl.program_id(0), pl.program_id(1)
    hd = HEAD_DIM
    qrow = q_ref[...]
    kvn = kvn_ref[...]
    blocks = ((ks0_ref, vs0_ref), (ks1_ref, vs1_ref))
    for g in range(NSA_KV_HEADS):
        q8 = _heads_to_rows(qrow, g)
        ks_new = kvn[:, 2 * NSA_KV_WIDTH + g * hd:2 * NSA_KV_WIDTH + (g + 1) * hd]
        vs_new = kvn[:, 3 * NSA_KV_WIDTH + g * hd:3 * NSA_KV_WIDTH + (g + 1) * hd]

        @pl.when(i == 0)
        def _():
            m_s[g] = jnp.broadcast_to(jnp.sum(q8 * ks_new, axis=-1, keepdims=True) * ATTN_SCALE, (8, hd))
            l_s[g] = jnp.ones((8, hd), F32)
            acc_s[g] = jnp.broadcast_to(vs_new, (8, hd))

        blk = idx_ref[b, g, i]
        half = lax.broadcasted_iota(I32, (8, PAGE_SIZE), 1) // SLC_BLOCK
        mask = half == jnp.where(blk < n_past_blk, blk % (PAGE_SIZE // SLC_BLOCK), -1)
        kt = blocks[g][0][...].astype(BF16)
        vt = blocks[g][1][...].astype(BF16)
        s = jnp.where(mask, _dot(q8.astype(BF16), kt) * ATTN_SCALE, NEG_INF)
        m_i = m_s[g][:, 0:1]
        m_new = jnp.maximum(m_i, jnp.max(s, axis=-1, keepdims=True))
        pr = jnp.where(mask, jnp.exp(s - m_new), 0.0)
        alpha = jnp.exp(m_i - m_new)
        l_s[g] = alpha * l_s[g] + jnp.sum(pr, axis=-1, keepdims=True)
        acc_s[g] = alpha * acc_s[g] + _dot_nt(pr.astype(BF16), vt)
        m_s[g] = jnp.broadcast_to(m_new, (8, hd))

    @pl.when(i == pl.num_programs(1) - 1)
    def _():
        gates = _sigmoid(gate_ref[...])
        for g in range(NSA_KV_HEADS):
            o_sel = acc_s[g] / l_s[g]
            for j in range(NSA_GROUP):
                h = g * NSA_GROUP + j
                sl = slice(h * hd, (h + 1) * hd)
                o_ref[:, sl] = (part_ref[:, sl] + gates[:, N_HEADS + h:N_HEADS + h + 1] * o_sel[j:j + 1]
                                ).astype(o_ref.dtype)


def _nsa_sample_attn(page_table, sel_idx, nsa_cache, q, kv_new, gates, part, layer, past):
    bs = page_table.shape[0]
    n_past_blk = past // SLC_BLOCK
    per_page = PAGE_SIZE // SLC_BLOCK

    def blk_spec(g, t):
        def imap(b, i, pt, ix):
            blk = jnp.minimum(ix[b, g, i], n_past_blk - 1)
            return (layer, pt[b, blk // per_page], t, g, 0, 0)
        return pl.BlockSpec((None, None, None, None, HEAD_DIM, PAGE_SIZE), imap)

    row = lambda w: pl.BlockSpec((None, 1, w), lambda b, i, pt, ix: (b, 0, 0))
    return pl.pallas_call(
        functools.partial(_nsa_sample_attn_body, n_past_blk=n_past_blk),
        grid_spec=pltpu.PrefetchScalarGridSpec(
            num_scalar_prefetch=2,
            grid=(bs, SLC_TOPK),
            in_specs=[blk_spec(0, 2), blk_spec(0, 3), blk_spec(1, 2), blk_spec(1, 3),
                      row(GROUP_WIDTH), row(6 * NSA_KV_WIDTH), row(LANE), row(GROUP_WIDTH)],
            out_specs=row(GROUP_WIDTH),
            scratch_shapes=[pltpu.VMEM((NSA_KV_HEADS, 8, HEAD_DIM), F32)] * 3),
        out_shape=jax.ShapeDtypeStruct((bs, 1, GROUP_WIDTH), F32),
        compiler_params=_cparams(("parallel", "arbitrary")),
        name="nsa_sample_attn",
    )(page_table, sel_idx, nsa_cache, nsa_cache, nsa_cache, nsa_cache, q, kv_new, gates, part)


def _recur_sample_body(c_ref, shift_ref, mu_ref, w0_ref, w2_ref, a0_ref, a2_ref, g2_ref, kk_ref, ka_ref, rk_ref,
                       lng_ref, lnb_ref, s_rw_ref, ret_ref, cos_ref, sin_ref, gam_ref, gn_g_ref, gn_b_ref, s_rt_ref,
                       o_rw_ref, s_rw_out, o_rt_ref, s_rt_out, o_scr):
    gw = GROUP_WIDTH
    hd = HEAD_DIM
    p = (mu_ref[...], w0_ref[...], w2_ref[...], a0_ref[...], a2_ref[...], g2_ref[...], kk_ref[...], ka_ref[...])
    r, w, k, v, kk, kka, gate = _rwkv_prep(c_ref[...], shift_ref[...], p)

    def put_state(h, j, s):
        s_rw_out[j, h] = s

    ys = _rwkv_steps((r, w, k, v, kk, kka), lambda h, j: s_rw_ref[j, h], put_state, 1)
    o_rw_ref[...] = _rwkv_post(ys, r, k, v, gate, rk_ref[...], lng_ref[...], lnb_ref[...]).astype(o_rw_ref.dtype)

    cos = cos_ref[...]
    sin = sin_ref[...]
    q = _rope(ret_ref[:, 0:gw], cos, sin)
    kr = _rope(ret_ref[:, gw:2 * gw], cos, sin) * ATTN_SCALE
    vr = ret_ref[:, 2 * gw:3 * gw]
    gam = gam_ref[...]
    qk = _dot(q * kr, _head_blockdiag(gw), HI)
    pad = jnp.zeros((LANE - 8, LANE), F32)
    for hp in range(N_HEADS // 2):
        ps = slice(hp * LANE, (hp + 1) * LANE)
        qt = jnp.concatenate([q[:, ps], pad], axis=0).T
        kt = jnp.concatenate([kr[:, ps], pad], axis=0).T
        for h2 in range(2):
            h = 2 * hp + h2
            sl = slice(h * hd, (h + 1) * hd)
            for j in range(8):
                s = s_rt_ref[j, h]
                qcol = qt[h2 * hd:(h2 + 1) * hd, j:j + 1]
                kcol = kt[h2 * hd:(h2 + 1) * hd, j:j + 1]
                g_h = gam[:, sl]
                o_scr[j:j + 1, sl] = (qk[j:j + 1, sl] * vr[j:j + 1, sl]
                                      + g_h * jnp.sum(qcol * s, axis=0, keepdims=True))
                s_rt_out[j, h] = s * g_h + kcol * vr[j:j + 1, sl]
    gt = ret_ref[:, 3 * gw:4 * gw]
    y = _head_norm(o_scr[...], gn_g_ref[...], gn_b_ref[...], GN_EPS)
    o_rt_ref[...] = (gt * _sigmoid(gt) * y).astype(o_rt_ref.dtype)


def _recur_sample(c_rwkv, shift, rwkv_params, s_rwkv, c_ret, rope, gamma, gn_g, gn_b, s_ret):
    bs = c_rwkv.shape[0]
    cos, sin = rope
    st = jax.ShapeDtypeStruct((bs, N_HEADS, HEAD_DIM, HEAD_DIM), F32)
    ob = jax.ShapeDtypeStruct((bs, GROUP_WIDTH), BF16)
    return pl.pallas_call(
        _recur_sample_body,
        out_shape=[ob, st, ob, st],
        scratch_shapes=[pltpu.VMEM((bs, GROUP_WIDTH), F32)],
        compiler_params=pltpu.CompilerParams(vmem_limit_bytes=VMEM_LIMIT),
        name="recur_sample",
    )(c_rwkv, shift, *rwkv_params, s_rwkv, c_ret, cos, sin, gamma, gn_g, gn_b, s_ret)


def _prep_rwkv_params(mu, w0, w2, a0, a2, g2, k_k, k_a, r_k, ln_g, ln_b):
    row = lambda z: z.reshape(1, -1)
    return (row(mu), row(w0), w2, row(a0), a2, g2, row(k_k), row(k_a), row(r_k), row(ln_g), row(ln_b))


def _prep_cmp_weights(w1, b1, w2, pe):
    span = CMP_LEN // CMP_STRIDE
    w1r = w1.reshape(2, span, CMP_STRIDE, HEAD_DIM, CMP_HIDDEN)
    w1r = jnp.transpose(w1r, (0, 2, 3, 1, 4)).reshape(2, CMP_STRIDE, HEAD_DIM, span * CMP_HIDDEN).astype(BF16)
    return (w1r, w1, b1.reshape(2, 1, CMP_HIDDEN), w2, pe.reshape(2, 1, CMP_LEN * HEAD_DIM))


def _pad_w_in(w):
    o_nsa = MOBA_COLS
    o_gate = o_nsa + GROUP_WIDTH + 6 * NSA_KV_WIDTH
    o_rwkv = o_nsa + NSA_COLS
    o_ret = o_rwkv + RWKV_COLS
    zeros = jnp.zeros((w.shape[0], N_PAD - OFF_GATE - 3 * N_HEADS), w.dtype)
    return jnp.concatenate([w[:, o_rwkv:o_ret], w[:, :o_gate], w[:, o_ret:], w[:, o_gate:o_rwkv], zeros], axis=1)


def kernel(x_prompt, x_sample, cache_moba_kv, cache_nsa_kv, state_nsa_win, state_rwkv, state_rwkv_shift, state_ret,
           page_table, norm_g, w_in, w_out, w_up, w_down, nsa_cmp_pe, nsa_cmp_w1, nsa_cmp_b1, nsa_cmp_w2, rwkv_mu,
           rwkv_w0, rwkv_w2, rwkv_a0, rwkv_a2, rwkv_g2, rwkv_k_k, rwkv_k_a, rwkv_r_k, rwkv_ln_g, rwkv_ln_b, ret_gn_g,
           ret_gn_b):
    bp, t, d = x_prompt.shape
    bs = x_sample.shape[0]
    assert x_sample.shape[1] == 1 and d == D_MODEL
    depth = w_in.shape[0]
    past = page_table.shape[1] * PAGE_SIZE
    gw = GROUP_WIDTH
    hd = HEAD_DIM

    log_gamma = jnp.log(1.0 - jnp.exp2(-5.0 - jnp.arange(N_HEADS, dtype=F32)))
    ret_tables = _ret_tables(log_gamma, RET_CHUNK)
    rope_p = _rope_tables(jnp.arange(t, dtype=I32))
    rope_s = _rope_tables(jnp.full((1,), past, I32))
    gamma_row = jnp.repeat(jnp.exp(log_gamma), hd)[None, :]
    moba_t = jnp.transpose(cache_moba_kv, (0, 1, 3, 4, 5, 2))
    nsa_t = jnp.transpose(cache_nsa_kv, (0, 1, 3, 4, 5, 2))
    win_t = jnp.transpose(state_nsa_win, (0, 1, 3, 4, 5, 2))

    xp = x_prompt.reshape(bp * t, d)
    xs = x_sample.reshape(bs, d)
    zero_state = jnp.zeros((bp, N_HEADS, hd, hd), F32)
    zero_shift = jnp.zeros((bp, RWKV_COLS), F32)
    st_p, st_s = [], []
    for l in range(depth):
        g = norm_g[l].reshape(4, 1, d)
        w_in_l = _pad_w_in(w_in[l]).astype(BF16)
        w_out_l = w_out[l].astype(BF16)
        w_up_l = w_up[l].astype(BF16)
        w_down_l = w_down[l].astype(BF16)
        cmp_w = _prep_cmp_weights(nsa_cmp_w1[l], nsa_cmp_b1[l], nsa_cmp_w2[l], nsa_cmp_pe[l])
        rwkv_p = _prep_rwkv_params(rwkv_mu[l], rwkv_w0[l], rwkv_w2[l], rwkv_a0[l], rwkv_a2[l], rwkv_g2[l],
                                   rwkv_k_k[l], rwkv_k_a[l], rwkv_r_k[l], rwkv_ln_g[l], rwkv_ln_b[l])
        gn_g = ret_gn_g[l].reshape(1, gw)
        gn_b = ret_gn_b[l].reshape(1, gw)

        cols = _inproj(xp, g[0], w_in_l, min(1024, bp * t), 512)
        o_moba = _moba_prompt(cols, bp, t)
        o_nsa = _nsa_prompt(cols, cmp_w, bp, t)
        o_rwkv, rwkv_s = _rwkv_prompt(cols, zero_shift, zero_state, rwkv_p, bp, t)
        o_ret, ret_s = _ret_prompt(cols, zero_state, ret_tables, rope_p, gn_g, gn_b, bp, t)
        xp = _outproj((o_moba, o_nsa, o_rwkv, o_ret), xp, g[1], w_out_l, 256)
        xp = _ffn(xp, g[2], g[3], w_up_l, w_down_l, min(512, bp * t), 512)
        c3 = cols.reshape(bp, t, N_PAD)
        win_keep = min(WINDOW, t)
        st_p.append((c3[:, :, OFF_MOBA + gw:OFF_MOBA + 3 * gw].reshape(bp, t, 2, N_HEADS, hd),
                     c3[:, :, OFF_NSAKV:OFF_NSAKV + 4 * NSA_KV_WIDTH].reshape(bp, t, 4, NSA_KV_HEADS, hd),
                     c3[:, t - win_keep:, OFF_NSAKV + 4 * NSA_KV_WIDTH:OFF_NSAKV + 6 * NSA_KV_WIDTH
                        ].reshape(bp, win_keep, 2, NSA_KV_HEADS, hd),
                     rwkv_s, c3[:, t - 1, OFF_RWKV:OFF_RWKV + RWKV_COLS], ret_s))

        cs = _inproj(xs, g[0], w_in_l, bs, 512)
        heads = lambda z: z.reshape(bs, N_HEADS, 1, hd)
        q_m = cs[:, OFF_MOBA:OFF_MOBA + gw]
        k_m = cs[:, OFF_MOBA + gw:OFF_MOBA + 2 * gw]
        v_m = cs[:, OFF_MOBA + 2 * gw:OFF_MOBA + 3 * gw]
        top, cmp_tok = _past_stats(page_table, moba_t, nsa_t, q_m.reshape(bs, N_HEADS, hd, 1), cmp_w, l)
        o_moba_s = _moba_sample(page_table, top[:, :MOBA_TOPK, :], moba_t, heads(q_m), heads(k_m), heads(v_m), l)
        q_n = cs[:, OFF_NSAQ:OFF_NSAQ + gw].reshape(bs, 1, gw)
        kv_new = cs[:, OFF_NSAKV:OFF_NSAKV + 6 * NSA_KV_WIDTH].reshape(bs, 1, 6 * NSA_KV_WIDTH)
        gates = cs[:, OFF_GATE:OFF_GATE + LANE].reshape(bs, 1, LANE)
        sel_idx, part = _nsa_sample_sel(q_n, cmp_tok, win_t, kv_new, gates, l, past)
        o_nsa_s = _nsa_sample_attn(page_table, sel_idx[:, :NSA_KV_HEADS, :SLC_TOPK], nsa_t, q_n, kv_new, gates, part,
                                   l, past)
        o_rwkv_s, rwkv_s_s, o_ret_s, ret_s_s = _recur_sample(
            cs[:, OFF_RWKV:OFF_RWKV + RWKV_COLS], state_rwkv_shift[l], rwkv_p, state_rwkv[l],
            cs[:, OFF_RET:OFF_RET + RET_COLS], rope_s, gamma_row, gn_g, gn_b, state_ret[l])
        parts_s = (o_moba_s.reshape(bs, gw).astype(BF16), o_nsa_s.reshape(bs, gw).astype(BF16), o_rwkv_s, o_ret_s)
        xs = _outproj(parts_s, xs, g[1], w_out_l, bs)
        xs = _ffn(xs, g[2], g[3], w_up_l, w_down_l, bs, 512)
        win_new = cs[:, OFF_NSAKV + 4 * NSA_KV_WIDTH:OFF_NSAKV + 6 * NSA_KV_WIDTH].reshape(bs, 1, 2, NSA_KV_HEADS, hd)
        win_all = jnp.concatenate([state_nsa_win[l], win_new], axis=1)
        keep_s = min(WINDOW, win_all.shape[1])
        st_s.append((cs[:, OFF_MOBA + gw:OFF_MOBA + 3 * gw].reshape(bs, 1, 2, N_HEADS, hd),
                     cs[:, OFF_NSAKV:OFF_NSAKV + 4 * NSA_KV_WIDTH].reshape(bs, 1, 4, NSA_KV_HEADS, hd),
                     win_all[:, win_all.shape[1] - keep_s:],
                     rwkv_s_s, cs[:, OFF_RWKV:OFF_RWKV + RWKV_COLS], ret_s_s))

    stk = lambda sts, i: jnp.stack([s[i] for s in sts], axis=0)
    outs = [xp.reshape(bp, t, d), xs.reshape(bs, 1, d)]
    for i in range(6):
        outs += [stk(st_p, i), stk(st_s, i)]
    return tuple(outs)
```

```python
import functools

import jax
import jax.numpy as jnp
import numpy as np
from jax import lax
from jax.experimental import pallas as pl
from jax.experimental.pallas import tpu as pltpu

F32 = jnp.float32
BF16 = jnp.bfloat16
I32 = jnp.int32
HI = lax.Precision.HIGHEST

D_MODEL = 2048
HEAD_DIM = 64
GROUP_WIDTH = D_MODEL // 4
N_HEADS = GROUP_WIDTH // HEAD_DIM
D_FF = 4 * D_MODEL
RMS_EPS = 1e-6
GN_EPS = 1e-5
NEG_INF = -1e30
ATTN_SCALE = HEAD_DIM ** -0.5
PAGE_SIZE = 128
MOBA_BLOCK = 256
MOBA_TOPK = 3
NSA_KV_HEADS = 2
NSA_GROUP = N_HEADS // NSA_KV_HEADS
NSA_KV_WIDTH = NSA_KV_HEADS * HEAD_DIM
CMP_LEN = 32
CMP_STRIDE = 16
CMP_HIDDEN = 2 * HEAD_DIM
SLC_BLOCK = 64
SLC_TOPK = 16
WINDOW = 512
RWKV_DECAY_RANK = 64
RWKV_AAA_RANK = 64
RWKV_GATE_RANK = 128
RWKV_LN_EPS = 64e-5
RET_CHUNK = 128
ROPE_BASE = 10000.0

MOBA_COLS = 3 * GROUP_WIDTH
NSA_COLS = GROUP_WIDTH + 6 * NSA_KV_WIDTH + 3 * N_HEADS
RWKV_COLS = 3 * GROUP_WIDTH + RWKV_DECAY_RANK + RWKV_AAA_RANK + RWKV_GATE_RANK
RET_COLS = 4 * GROUP_WIDTH
IN_COLS = MOBA_COLS + NSA_COLS + RWKV_COLS + RET_COLS

OFF_RWKV = 0
OFF_MOBA = OFF_RWKV + RWKV_COLS
OFF_NSAQ = OFF_MOBA + MOBA_COLS
OFF_NSAKV = OFF_NSAQ + GROUP_WIDTH
OFF_RET = OFF_NSAKV + 6 * NSA_KV_WIDTH
OFF_GATE = OFF_RET + RET_COLS
N_PAD = OFF_GATE + 512

LANE = 128
VMEM_LIMIT = 56 * 1024 * 1024


def _cparams(sem):
    return pltpu.CompilerParams(dimension_semantics=sem, vmem_limit_bytes=VMEM_LIMIT)


def _dot(a, b, precision=None):
    return jnp.dot(a, b, preferred_element_type=F32, precision=precision)


def _dot_nt(a, b, precision=None):
    return lax.dot_general(a, b, (((1,), (1,)), ((), ())), preferred_element_type=F32, precision=precision)


def _dot_tn(a, b, precision=None):
    return lax.dot_general(a, b, (((0,), (0,)), ((), ())), preferred_element_type=F32, precision=precision)


def _head_blockdiag(n, scale=1.0):
    r = lax.broadcasted_iota(I32, (n, n), 0) // HEAD_DIM
    c = lax.broadcasted_iota(I32, (n, n), 1) // HEAD_DIM
    return jnp.where(r == c, scale, 0.0).astype(F32)


def _sigmoid(x):
    return 1.0 / (1.0 + jnp.exp(-x))


def _inproj_body(x_ref, g_ref, w_ref, o_ref, h_scr):
    @pl.when(pl.program_id(1) == 0)
    def _():
        x = x_ref[...]
        ms = jnp.mean(x * x, axis=-1, keepdims=True)
        h_scr[...] = (x * lax.rsqrt(ms + RMS_EPS) * g_ref[...]).astype(BF16)

    o_ref[...] = _dot(h_scr[...], w_ref[...])


def _inproj(x, g, w, tm, tn):
    m, d = x.shape
    n = w.shape[1]
    return pl.pallas_call(
        _inproj_body,
        grid=(m // tm, n // tn),
        in_specs=[pl.BlockSpec((tm, d), lambda i, j: (i, 0)),
                  pl.BlockSpec((1, d), lambda i, j: (0, 0)),
                  pl.BlockSpec((d, tn), lambda i, j: (0, j))],
        out_specs=pl.BlockSpec((tm, tn), lambda i, j: (i, j)),
        out_shape=jax.ShapeDtypeStruct((m, n), F32),
        scratch_shapes=[pltpu.VMEM((tm, d), BF16)],
        compiler_params=_cparams(("parallel", "arbitrary")),
        name="inproj",
    )(x, g, w)


def _outproj_body(a_ref, b_ref, c_ref, d_ref, x_ref, g_ref, w_ref, o_ref):
    gw = GROUP_WIDTH
    y = _dot(a_ref[...], w_ref[0:gw, :])
    y += _dot(b_ref[...], w_ref[gw:2 * gw, :])
    y += _dot(c_ref[...], w_ref[2 * gw:3 * gw, :])
    y += _dot(d_ref[...], w_ref[3 * gw:4 * gw, :])
    ms = jnp.mean(y * y, axis=-1, keepdims=True)
    o_ref[...] = x_ref[...] + y * lax.rsqrt(ms + RMS_EPS) * g_ref[...]


def _outproj(parts, x, g, w, tm):
    m, d = x.shape
    gw = GROUP_WIDTH
    part_spec = pl.BlockSpec((tm, gw), lambda i: (i, 0))
    return pl.pallas_call(
        _outproj_body,
        grid=(m // tm,),
        in_specs=[part_spec, part_spec, part_spec, part_spec,
                  pl.BlockSpec((tm, d), lambda i: (i, 0)),
                  pl.BlockSpec((1, d), lambda i: (0, 0)),
                  pl.BlockSpec((4 * gw, d), lambda i: (0, 0))],
        out_specs=pl.BlockSpec((tm, d), lambda i: (i, 0)),
        out_shape=jax.ShapeDtypeStruct((m, d), F32),
        compiler_params=_cparams(("parallel",)),
        name="outproj",
    )(*parts, x, g, w)


def _ffn_body(x_ref, g2_ref, g3_ref, wu_ref, wd_ref, o_ref, h_scr, acc_scr):
    f = pl.program_id(1)

    @pl.when(f == 0)
    def _():
        x = x_ref[...]
        ms = jnp.mean(x * x, axis=-1, keepdims=True)
        h_scr[...] = (x * lax.rsqrt(ms + RMS_EPS) * g2_ref[...]).astype(BF16)
        acc_scr[...] = jnp.zeros_like(acc_scr)

    u = jnp.maximum(_dot(h_scr[...], wu_ref[...]), 0.0)
    acc_scr[...] += _dot((u * u).astype(BF16), wd_ref[...])

    @pl.when(f == pl.num_programs(1) - 1)
    def _():
        y = acc_scr[...]
        ms = jnp.mean(y * y, axis=-1, keepdims=True)
        o_ref[...] = x_ref[...] + y * lax.rsqrt(ms + RMS_EPS) * g3_ref[...]


def _ffn(x, g2, g3, wu, wd, tm, tf):
    m, d = x.shape
    f = wu.shape[1]
    return pl.pallas_call(
        _ffn_body,
        grid=(m // tm, f // tf),
        in_specs=[pl.BlockSpec((tm, d), lambda i, j: (i, 0)),
                  pl.BlockSpec((1, d), lambda i, j: (0, 0)),
                  pl.BlockSpec((1, d), lambda i, j: (0, 0)),
                  pl.BlockSpec((d, tf), lambda i, j: (0, j)),
                  pl.BlockSpec((tf, d), lambda i, j: (j, 0))],
        out_specs=pl.BlockSpec((tm, d), lambda i, j: (i, 0)),
        out_shape=jax.ShapeDtypeStruct((m, d), F32),
        scratch_shapes=[pltpu.VMEM((tm, d), BF16), pltpu.VMEM((tm, d), F32)],
        compiler_params=_cparams(("parallel", "arbitrary")),
        name="ffn",
    )(x, g2, g3, wu, wd)


MOBA_HEAD_GROUP = 4


def _moba_prompt_body(q_ref, k_ref, v_ref, o_ref, kmean_scr, *, nb):
    blk = MOBA_BLOCK
    hg = MOBA_HEAD_GROUP
    qt = pl.program_id(2)

    @pl.when(qt == 0)
    def _():
        for n in range(nb):
            kmean_scr[n:n + 1, :] = jnp.sum(k_ref[n * blk:(n + 1) * blk, :], axis=0, keepdims=True) * (1.0 / blk)

    row = lax.broadcasted_iota(I32, (blk, blk), 0)
    col = lax.broadcasted_iota(I32, (blk, blk), 1)
    bias_own = jnp.where(col <= row, 0.0, NEG_INF)
    blk_id = lax.broadcasted_iota(I32, (blk, nb), 1)
    eligible = blk_id < qt
    sls = [slice(HEAD_DIM * j, HEAD_DIM * (j + 1)) for j in range(hg)]
    sel, qb = [], []
    for j in range(hg):
        q = q_ref[:, sls[j]]
        gate = jnp.where(eligible, _dot_nt(q, kmean_scr[:, sls[j]], HI), -jnp.inf)
        rank = jnp.zeros((blk, nb), I32)
        for m in range(nb):
            gm = gate[:, m:m + 1]
            rank += ((gm > gate) | ((gm == gate) & (m < blk_id))).astype(I32)
        sel.append((eligible & (rank < MOBA_TOPK)).astype(F32))
        qb.append((q * ATTN_SCALE).astype(BF16))

    def body(n, carry):
        start = pl.multiple_of(n * blk, blk)
        out = []
        for j in range(hg):
            m_i, l_i, acc = carry[j]
            kb = k_ref[pl.ds(start, blk), sls[j]].astype(BF16)
            vb = v_ref[pl.ds(start, blk), sls[j]].astype(BF16)
            sel_n = jnp.sum(jnp.where(blk_id == n, sel[j], 0.0), axis=-1, keepdims=True)
            bias = jnp.where(n == qt, bias_own, jnp.broadcast_to((sel_n - 1.0) * -NEG_INF, (blk, blk)))
            s = _dot_nt(qb[j], kb) + bias
            m_new = jnp.maximum(m_i, jnp.max(s, axis=-1, keepdims=True))
            p = jnp.exp(s - m_new)
            alpha = jnp.exp(m_i - m_new)
            out.append((m_new, alpha * l_i + jnp.sum(p, axis=-1, keepdims=True),
                        alpha * acc + _dot(p.astype(BF16), vb)))
        return tuple(out)

    init = tuple((jnp.full((blk, 1), NEG_INF, F32), jnp.zeros((blk, 1), F32), jnp.zeros((blk, HEAD_DIM), F32))
                 for _ in range(hg))
    fin = lax.fori_loop(0, qt + 1, body, init)
    for j in range(hg):
        o_ref[:, sls[j]] = (fin[j][2] / fin[j][1]).astype(o_ref.dtype)


def _moba_prompt(cols, b, t):
    blk = MOBA_BLOCK
    nb = t // blk
    w = MOBA_HEAD_GROUP * HEAD_DIM
    qoff = OFF_MOBA // w
    koff = (OFF_MOBA + GROUP_WIDTH) // w
    voff = (OFF_MOBA + 2 * GROUP_WIDTH) // w
    return pl.pallas_call(
        functools.partial(_moba_prompt_body, nb=nb),
        grid=(b, GROUP_WIDTH // w, nb),
        in_specs=[pl.BlockSpec((blk, w), lambda i, h, q: (i * nb + q, qoff + h)),
                  pl.BlockSpec((t, w), lambda i, h, q: (i, koff + h)),
                  pl.BlockSpec((t, w), lambda i, h, q: (i, voff + h))],
        out_specs=pl.BlockSpec((blk, w), lambda i, h, q: (i * nb + q, h)),
        out_shape=jax.ShapeDtypeStruct((b * t, GROUP_WIDTH), BF16),
        scratch_shapes=[pltpu.VMEM((nb, w), F32)],
        compiler_params=_cparams(("parallel", "parallel", "arbitrary")),
        name="moba_prompt",
    )(cols, cols, cols)


def _gelu_tanh(x):
    return x * (0.5 * (1.0 + jnp.tanh(np.sqrt(2.0 / np.pi).astype(np.float32) * (x + 0.044715 * (x * x * x)))))


def _cmp_bias(pe_ref, w1_ref, b1_ref, kv):
    pe8 = jnp.broadcast_to(pe_ref[kv], (8, CMP_LEN * HEAD_DIM))
    return _dot(pe8, w1_ref[kv], HI)[0:1, :] + b1_ref[kv]


def _compress(load_rows, n, w1r_ref, kv, bias, w2):
    acc = [jnp.zeros((n, 2 * CMP_HIDDEN), F32) for _ in range(NSA_KV_HEADS)]
    for r in range(CMP_STRIDE):
        x = load_rows(r).astype(BF16)
        for g in range(NSA_KV_HEADS):
            acc[g] += _dot(x[:, g * HEAD_DIM:(g + 1) * HEAD_DIM], w1r_ref[kv, r])
    out = []
    for g in range(NSA_KV_HEADS):
        hid = acc[g][:, :CMP_HIDDEN] + pltpu.roll(acc[g][:, CMP_HIDDEN:], n - 1, 0) + bias
        out.append(_dot(_gelu_tanh(hid).astype(BF16), w2.astype(BF16)))
    return out


def _softmax_rows(l, mask):
    m = jnp.max(jnp.where(mask, l, NEG_INF), axis=-1, keepdims=True)
    e = jnp.where(mask, jnp.exp(l - m), 0.0)
    s = jnp.sum(e, axis=-1, keepdims=True)
    return jnp.where(s > 0.0, e / jnp.where(s > 0.0, s, 1.0), 0.0)


def _slc_matrix(n_cmp_pad, n_cmp, n_slc_pad):
    n = lax.broadcasted_iota(I32, (n_cmp_pad, n_slc_pad), 0)
    j = lax.broadcasted_iota(I32, (n_cmp_pad, n_slc_pad), 1)
    ratio = SLC_BLOCK // CMP_STRIDE
    return ((n >= ratio * j - 1) & (n <= ratio * j + ratio - 1) & (n < n_cmp)).astype(F32)


def _topk_rows(score, ids, n_cand, k):
    rank = jnp.zeros(score.shape, I32)
    for m in range(n_cand):
        sm = score[:, m:m + 1]
        rank += ((sm > score) | ((sm == score) & (m < ids))).astype(I32)
    return rank < k


def _nsa_prompt_body(q0_ref, q1_ref, kc_ref, vc_ref, ksvs_ref, kwvw_ref, gate_ref, w1r_ref, w1_ref, b1_ref, w2_ref,
                     pe_ref, o_ref, ck_scr, cv_scr, *, t):
    tq = 256
    n_chunk = t // CMP_STRIDE
    n_cmp = n_chunk - CMP_LEN // CMP_STRIDE + 1
    qt = pl.program_id(1)
    hd = HEAD_DIM

    @pl.when(qt == 0)
    def _():
        for kv, (src, dst) in enumerate(((kc_ref, ck_scr), (vc_ref, cv_scr))):
            bias = _cmp_bias(pe_ref, w1_ref, b1_ref, kv)
            out = _compress(lambda r: src[pl.ds(r, n_chunk, stride=CMP_STRIDE), :],
                            n_chunk, w1r_ref, kv, bias, w2_ref[kv])
            for g in range(NSA_KV_HEADS):
                dst[g] = out[g]

    tpos = qt * tq + lax.broadcasted_iota(I32, (tq, 1), 0)
    tpos4 = jnp.concatenate([tpos] * NSA_GROUP, axis=0)
    gates = _sigmoid(gate_ref[...])
    slc_ids = lax.broadcasted_iota(I32, (tq, LANE), 1)
    n_slc = t // SLC_BLOCK
    kcol = lax.broadcasted_iota(I32, (tq, tq), 1)
    jrow = lax.broadcasted_iota(I32, (LANE, tq), 0)
    kcol_e = lax.broadcasted_iota(I32, (LANE, tq), 1)
    for g in range(NSA_KV_HEADS):
        q_ref = q0_ref if g == 0 else q1_ref
        q4f = jnp.concatenate([q_ref[:, j * hd:(j + 1) * hd] for j in range(NSA_GROUP)], axis=0)
        q4 = q4f.astype(BF16)
        q4s = (q4f * ATTN_SCALE).astype(BF16)
        cmp_end = lax.broadcasted_iota(I32, (NSA_GROUP * tq, n_chunk), 1) * CMP_STRIDE + (CMP_LEN - 1)
        l_cmp = _dot_nt(q4, ck_scr[g].astype(BF16)) * ATTN_SCALE
        p_cmp = _softmax_rows(l_cmp, cmp_end <= tpos4)
        o_cmp = _dot(p_cmp.astype(BF16), cv_scr[g].astype(BF16))
        imp = p_cmp[0:tq]
        for j in range(1, NSA_GROUP):
            imp = imp + p_cmp[j * tq:(j + 1) * tq]
        imp_slc = _dot(imp, _slc_matrix(n_chunk, n_cmp, LANE), HI)
        cur = tpos // SLC_BLOCK
        eligible = (slc_ids <= cur) & (slc_ids < n_slc)
        forced = (slc_ids == 0) | (slc_ids == cur) | (slc_ids == cur - 1)
        score = jnp.where(eligible, jnp.where(forced, jnp.inf, imp_slc), -jnp.inf)
        sel = (eligible & _topk_rows(score, slc_ids, n_slc, SLC_TOPK)).astype(BF16)

        def sel_body(n, carry):
            m_i, l_i, acc = carry
            start = pl.multiple_of(n * tq, tq)
            kb = ksvs_ref[pl.ds(start, tq), g * hd:(g + 1) * hd].astype(BF16)
            vb = ksvs_ref[pl.ds(start, tq), NSA_KV_WIDTH + g * hd:NSA_KV_WIDTH + (g + 1) * hd].astype(BF16)
            expand = (jrow == (tq // SLC_BLOCK) * n + kcol_e // SLC_BLOCK).astype(BF16)
            mask = (_dot(sel, expand) > 0.5) & (n * tq + kcol <= tpos)
            bias = jnp.where(mask, 0.0, NEG_INF)
            s = (_dot_nt(q4s, kb).reshape(NSA_GROUP, tq, tq) + bias[None]).reshape(NSA_GROUP * tq, tq)
            m_new = jnp.maximum(m_i, jnp.max(s, axis=-1, keepdims=True))
            p = jnp.exp(s - m_new)
            alpha = jnp.exp(m_i - m_new)
            return (m_new, alpha * l_i + jnp.sum(p, axis=-1, keepdims=True),
                    alpha * acc + _dot(p.astype(BF16), vb))

        def win_body(n, carry):
            m_i, l_i, acc = carry
            start = pl.multiple_of(n * tq, tq)
            kb = kwvw_ref[pl.ds(start, tq), g * hd:(g + 1) * hd].astype(BF16)
            vb = kwvw_ref[pl.ds(start, tq), NSA_KV_WIDTH + g * hd:NSA_KV_WIDTH + (g + 1) * hd].astype(BF16)
            dist = tpos - (n * tq + kcol)
            bias = jnp.where((dist >= 0) & (dist < WINDOW), 0.0, NEG_INF)
            s = (_dot_nt(q4s, kb).reshape(NSA_GROUP, tq, tq) + bias[None]).reshape(NSA_GROUP * tq, tq)
            m_new = jnp.maximum(m_i, jnp.max(s, axis=-1, keepdims=True))
            p = jnp.exp(s - m_new)
            alpha = jnp.exp(m_i - m_new)
            return (m_new, alpha * l_i + jnp.sum(p, axis=-1, keepdims=True),
                    alpha * acc + _dot(p.astype(BF16), vb))

        rows = NSA_GROUP * tq
        init = (jnp.full((rows, 1), NEG_INF, F32), jnp.zeros((rows, 1), F32), jnp.zeros((rows, hd), F32))
        _, l_s, acc_s = lax.fori_loop(0, qt + 1, sel_body, init)
        _, l_w, acc_w = lax.fori_loop(jnp.maximum(qt - (WINDOW // tq), 0), qt + 1, win_body, init)
        o_sel = acc_s / l_s
        o_win = acc_w / l_w
        for j in range(NSA_GROUP):
            h = g * NSA_GROUP + j
            rs = slice(j * tq, (j + 1) * tq)
            o = (gates[:, h:h + 1] * o_cmp[rs] + gates[:, N_HEADS + h:N_HEADS + h + 1] * o_sel[rs]
                 + gates[:, 2 * N_HEADS + h:2 * N_HEADS + h + 1] * o_win[rs])
            o_ref[:, h * hd:(h + 1) * hd] = o.astype(o_ref.dtype)


def _nsa_prompt(cols, cmp_w, b, t):
    w1r, w1, b1, w2, pe = cmp_w
    tq = 256
    nq = t // tq
    n_chunk = t // CMP_STRIDE
    qoff = OFF_NSAQ // 256
    kvoff = OFF_NSAKV // 256
    full = lambda a: pl.BlockSpec(a.shape, lambda i, q: (0,) * a.ndim)
    return pl.pallas_call(
        functools.partial(_nsa_prompt_body, t=t),
        grid=(b, nq),
        in_specs=[pl.BlockSpec((tq, 256), lambda i, q: (i * nq + q, qoff)),
                  pl.BlockSpec((tq, 256), lambda i, q: (i * nq + q, qoff + 1)),
                  pl.BlockSpec((t, LANE), lambda i, q: (i, 2 * kvoff)),
                  pl.BlockSpec((t, LANE), lambda i, q: (i, 2 * kvoff + 1)),
                  pl.BlockSpec((t, 256), lambda i, q: (i, kvoff + 1)),
                  pl.BlockSpec((t, 256), lambda i, q: (i, kvoff + 2)),
                  pl.BlockSpec((tq, LANE), lambda i, q: (i * nq + q, OFF_GATE // LANE)),
                  full(w1r), full(w1), full(b1), full(w2), full(pe)],
        out_specs=pl.BlockSpec((tq, GROUP_WIDTH), lambda i, q: (i * nq + q, 0)),
        out_shape=jax.ShapeDtypeStruct((b * t, GROUP_WIDTH), BF16),
        scratch_shapes=[pltpu.VMEM((NSA_KV_HEADS, n_chunk, HEAD_DIM), F32),
                        pltpu.VMEM((NSA_KV_HEADS, n_chunk, HEAD_DIM), F32)],
        compiler_params=_cparams(("parallel", "arbitrary")),
        name="nsa_prompt",
    )(cols, cols, cols, cols, cols, cols, cols, w1r, w1, b1, w2, pe)


def _rope(x, cos, sin):
    half = HEAD_DIM // 2
    lane = lax.broadcasted_iota(I32, x.shape, 1) % HEAD_DIM
    nxt = pltpu.roll(x, x.shape[1] - half, 1)
    prv = pltpu.roll(x, half, 1)
    return x * cos + jnp.where(lane < half, -nxt, prv) * sin


def _head_norm(y, g, b, eps):
    avg = _head_blockdiag(y.shape[1], 1.0 / HEAD_DIM)
    mu = _dot(y, avg, HI)
    d = y - mu
    var = _dot(d * d, avg, HI)
    return d * lax.rsqrt(var + eps) * g + b


def _ret_prompt_body(q_ref, k_ref, v_ref, gate_ref, cos_ref, sin_ref, dmask_ref, xi_ref, zeta_ref, cd_ref,
                     gn_g_ref, gn_b_ref, s0_ref, o_ref, s_ref, o_scr):
    @pl.when(pl.program_id(1) == 0)
    def _():
        s_ref[...] = s0_ref[...]

    cos = cos_ref[...]
    sin = sin_ref[...]
    q = _rope(q_ref[...], cos, sin)
    k = _rope(k_ref[...], cos, sin) * ATTN_SCALE
    kz = (k * zeta_ref[...]).astype(BF16)
    qb = q.astype(BF16)
    kb = k.astype(BF16)
    for h in range(N_HEADS):
        sl = slice(h * HEAD_DIM, (h + 1) * HEAD_DIM)
        vb = v_ref[:, sl].astype(BF16)
        s = s_ref[0, h]
        att = _dot_nt(qb[:, sl], kb[:, sl]) * dmask_ref[h]
        o_scr[:, sl] = _dot(att.astype(BF16), vb) + _dot(qb[:, sl], s.astype(BF16)) * xi_ref[:, sl]
        s_ref[0, h] = s * cd_ref[:, sl] + _dot_tn(kz[:, sl], vb)
    gate = gate_ref[...]
    y = _head_norm(o_scr[...], gn_g_ref[...], gn_b_ref[...], GN_EPS)
    o_ref[...] = (gate * _sigmoid(gate) * y).astype(o_ref.dtype)


def _ret_tables(log_gamma, c):
    idx = jnp.arange(c, dtype=F32)
    diff = idx[:, None] - idx[None, :]
    dmask = jnp.where(diff >= 0, jnp.exp(jnp.maximum(diff, 0.0)[None] * log_gamma[:, None, None]), 0.0)
    rep = lambda z: jnp.repeat(z, HEAD_DIM, axis=-1)
    xi = rep(jnp.exp((idx + 1.0)[:, None] * log_gamma[None, :]))
    zeta = rep(jnp.exp((c - 1.0 - idx)[:, None] * log_gamma[None, :]))
    cd = rep(jnp.exp(c * log_gamma)[None, :])
    return dmask, xi, zeta, cd


def _rope_tables(pos):
    half = HEAD_DIM // 2
    inv = ROPE_BASE ** (-jnp.arange(half, dtype=F32) / half)
    ang = pos.astype(F32)[:, None] * inv[None, :]
    tile = lambda z: jnp.tile(z, (1, 2 * N_HEADS))
    return tile(jnp.cos(ang)), tile(jnp.sin(ang))


def _ret_prompt(cols, s0, tables, rope, gn_g, gn_b, b, t):
    c = RET_CHUNK
    nc = t // c
    gw = GROUP_WIDTH
    off = OFF_RET // gw
    dmask, xi, zeta, cd = tables
    cos, sin = rope
    col_spec = lambda j: pl.BlockSpec((c, gw), lambda i, n: (i * nc + n, off + j))
    full = lambda a: pl.BlockSpec(a.shape, lambda i, n: (0,) * a.ndim)
    st_spec = pl.BlockSpec((1, N_HEADS, HEAD_DIM, HEAD_DIM), lambda i, n: (i, 0, 0, 0))
    return pl.pallas_call(
        _ret_prompt_body,
        grid=(b, nc),
        in_specs=[col_spec(0), col_spec(1), col_spec(2), col_spec(3),
                  pl.BlockSpec((c, gw), lambda i, n: (n, 0)), pl.BlockSpec((c, gw), lambda i, n: (n, 0)),
                  full(dmask), full(xi), full(zeta), full(cd), full(gn_g), full(gn_b), st_spec],
        out_specs=[pl.BlockSpec((c, gw), lambda i, n: (i * nc + n, 0)), st_spec],
        out_shape=[jax.ShapeDtypeStruct((b * t, gw), BF16),
                   jax.ShapeDtypeStruct((b, N_HEADS, HEAD_DIM, HEAD_DIM), F32)],
        scratch_shapes=[pltpu.VMEM((c, gw), F32)],
        compiler_params=_cparams(("parallel", "arbitrary")),
        name="ret_prompt",
    )(cols, cols, cols, cols, cos, sin, dmask, xi, zeta, cd, gn_g, gn_b, s0)


def _softplus(x):
    return jnp.maximum(x, 0.0) + jnp.log(1.0 + jnp.exp(-jnp.abs(x)))


def _rwkv_prep(c, prev, p):
    mu, w0, w2, a0, a2, g2, k_k, k_a = p
    gw = GROUP_WIDTH
    mixed = c + (prev - c) * mu
    r, k, v = mixed[:, :gw], mixed[:, gw:2 * gw], mixed[:, 2 * gw:3 * gw]
    o1 = 3 * gw
    o2 = o1 + RWKV_DECAY_RANK
    o3 = o2 + RWKV_AAA_RANK
    xw, xa, xg = mixed[:, o1:o2], mixed[:, o2:o3], mixed[:, o3:]
    w_log = -_softplus(-(w0 + _dot(jnp.tanh(xw), w2, HI))) - 0.5
    decay = jnp.exp(-jnp.exp(w_log))
    a = _sigmoid(a0 + _dot(xa, a2, HI))
    gate = _dot(_sigmoid(xg).astype(BF16), g2.astype(BF16))
    kk = k * k_k
    norm = jnp.sqrt(_dot(kk * kk, _head_blockdiag(gw), HI))
    kk = kk / jnp.maximum(norm, 1e-12)
    k = k * (1.0 + (a - 1.0) * k_a)
    return r, decay, k, v, kk, kk * a, gate


def _rwkv_steps(vecs, get_state, put_state, n_steps):
    r8, w8, k8, v8, kk8, ka8 = vecs
    lane = lax.broadcasted_iota(I32, (HEAD_DIM, LANE), 1)
    pad = jnp.zeros((LANE - 8, LANE), F32)
    ys = []
    for hp in range(N_HEADS // 2):
        vt = jnp.concatenate([v8[:, hp * LANE:(hp + 1) * LANE], pad], axis=0).T
        yts = []
        for h2 in range(2):
            h = 2 * hp + h2
            sl = slice(h * HEAD_DIM, (h + 1) * HEAD_DIM)
            yt = jnp.zeros((HEAD_DIM, LANE), F32)
            s = get_state(h, 0)
            for j in range(8):
                if n_steps == 1:
                    s = get_state(h, j)
                vcol = vt[h2 * HEAD_DIM:(h2 + 1) * HEAD_DIM, j:j + 1]
                sa = jnp.sum(s * kk8[j:j + 1, sl], axis=-1, keepdims=True)
                s = s * w8[j:j + 1, sl] - sa * ka8[j:j + 1, sl] + vcol * k8[j:j + 1, sl]
                ycol = jnp.sum(s * r8[j:j + 1, sl], axis=-1, keepdims=True)
                yt = jnp.where(lane == j, ycol, yt)
                if n_steps == 1:
                    put_state(h, j, s)
            if n_steps != 1:
                put_state(h, 0, s)
            yts.append(yt)
        ys.append(jnp.concatenate(yts, axis=0).T[0:8, :])
    return jnp.concatenate(ys, axis=1)


def _rwkv_post(ys, r, k, v, gate, r_k, ln_g, ln_b):
    y = _head_norm(ys, ln_g, ln_b, RWKV_LN_EPS)
    y = y + _dot(r * k * r_k, _head_blockdiag(GROUP_WIDTH), HI) * v
    return y * gate


def _rwkv_prep_body(c_ref, shift_ref, mu_ref, w0_ref, w2_ref, a0_ref, a2_ref, g2_ref, kk_ref, ka_ref, rk_ref,
                    w_o, kk_o, kka_o, k_o, r_o, v_o, bonus_o, gate_o, carry):
    tc = c_ref.shape[0]

    @pl.when(pl.program_id(1) == 0)
    def _():
        carry[...] = shift_ref[...]

    c = c_ref[...]
    row = lax.broadcasted_iota(I32, c.shape, 0)
    prev = jnp.where(row == 0, carry[...], pltpu.roll(c, 1, 0))
    carry[...] = c[tc - 1:tc, :]
    p = (mu_ref[...], w0_ref[...], w2_ref[...], a0_ref[...], a2_ref[...], g2_ref[...], kk_ref[...], ka_ref[...])
    r, w, k, v, kk, kka, gate = _rwkv_prep(c, prev, p)
    w_o[...] = w
    kk_o[...] = kk
    kka_o[...] = kka
    k_o[...] = k
    r_o[...] = r
    v_o[...] = v
    bonus_o[...] = _dot(r * k * rk_ref[...], _head_blockdiag(GROUP_WIDTH), HI) * v
    gate_o[...] = gate


def _rwkv_scan_body(xt_ref, v_ref, s0_ref, y_ref, s_ref):
    @pl.when(pl.program_id(0) == 0)
    def _():
        s_ref[...] = s0_ref[...]

    n_vq = s_ref.shape[0]

    def step(t, _):
        w, kk, kka, k, r = (xt_ref[i, t] for i in range(5))
        vrows = v_ref[t]
        for vq in range(n_vq):
            s = s_ref[vq]
            sa = jnp.sum(s * kk, axis=0, keepdims=True)
            s = s * w - sa * kka + vrows[vq:vq + 1, :] * k
            s_ref[vq] = s
            y_ref[t, vq:vq + 1, :] = jnp.sum(s * r, axis=0, keepdims=True)
        return 0

    lax.fori_loop(0, v_ref.shape[0], step, 0)


def _rwkv_post_body(y_ref, bonus_ref, gate_ref, lng_ref, lnb_ref, o_ref):
    y = _head_norm(y_ref[...], lng_ref[...], lnb_ref[...], RWKV_LN_EPS)
    o_ref[...] = ((y + bonus_ref[...]) * gate_ref[...]).astype(o_ref.dtype)


def _rwkv_prompt(cols, shift_prev, s0, params, b, t, tc=256, tscan=64):
    mu, w0, w2, a0, a2, g2, k_k, k_a, r_k, ln_g, ln_b = params
    nt = t // tc
    gw = GROUP_WIDTH
    hd = HEAD_DIM
    chains = b * N_HEADS
    rep = LANE // chains
    assert rep * chains == LANE and hd % rep == 0
    n_vq = hd // rep
    full = lambda a: pl.BlockSpec(a.shape, lambda i, n: (0,) * a.ndim)
    tok = pl.BlockSpec((tc, gw), lambda i, n: (i * nt + n, 0))
    vec = jax.ShapeDtypeStruct((b * t, gw), F32)
    prep_in = (mu, w0, w2, a0, a2, g2, k_k, k_a, r_k)
    w, kk, kka, k, r, v, bonus, gate = pl.pallas_call(
        _rwkv_prep_body,
        grid=(b, nt),
        in_specs=[pl.BlockSpec((tc, RWKV_COLS), lambda i, n: (i * nt + n, OFF_RWKV // RWKV_COLS)),
                  pl.BlockSpec((None, 1, RWKV_COLS), lambda i, n: (i, 0, 0))] + [full(a) for a in prep_in],
        out_specs=[tok] * 8,
        out_shape=[vec] * 8,
        scratch_shapes=[pltpu.VMEM((1, RWKV_COLS), F32)],
        compiler_params=_cparams(("parallel", "arbitrary")),
        name="rwkv_prep",
    )(cols, shift_prev.reshape(b, 1, RWKV_COLS), *prep_in)

    def key_tiles(x):
        x = jnp.transpose(x.reshape(b, t, N_HEADS, hd), (1, 3, 0, 2)).reshape(t, hd, chains)
        return jnp.tile(x, (1, 1, rep))
    xt = jnp.stack([key_tiles(z) for z in (w, kk, kka, k, r)], axis=0)
    vt = jnp.transpose(v.reshape(b, t, N_HEADS, rep, n_vq), (1, 4, 3, 0, 2)).reshape(t, n_vq, LANE)
    s0t = jnp.transpose(s0.reshape(b, N_HEADS, rep, n_vq, hd), (3, 4, 2, 0, 1)).reshape(n_vq, hd, LANE)

    ns = t // tscan
    st_spec = pl.BlockSpec((n_vq, hd, LANE), lambda n: (0, 0, 0))
    yt, st = pl.pallas_call(
        _rwkv_scan_body,
        grid=(ns,),
        in_specs=[pl.BlockSpec((5, tscan, hd, LANE), lambda n: (0, n, 0, 0)),
                  pl.BlockSpec((tscan, n_vq, LANE), lambda n: (n, 0, 0)), st_spec],
        out_specs=[pl.BlockSpec((tscan, n_vq, LANE), lambda n: (n, 0, 0)), st_spec],
        out_shape=[jax.ShapeDtypeStruct((t, n_vq, LANE), F32), jax.ShapeDtypeStruct((n_vq, hd, LANE), F32)],
        compiler_params=_cparams(("arbitrary",)),
        name="rwkv_scan",
    )(xt, vt, s0t)
    ys = jnp.transpose(yt.reshape(t, n_vq, rep, b, N_HEADS), (3, 0, 4, 2, 1)).reshape(b * t, gw)
    s_fin = jnp.transpose(st.reshape(n_vq, hd, rep, b, N_HEADS), (3, 4, 2, 0, 1)).reshape(b, N_HEADS, hd, hd)

    tm = min(512, b * t)
    tokm = pl.BlockSpec((tm, gw), lambda i: (i, 0))
    o = pl.pallas_call(
        _rwkv_post_body,
        grid=(b * t // tm,),
        in_specs=[tokm, tokm, tokm, pl.BlockSpec((1, gw), lambda i: (0, 0)), pl.BlockSpec((1, gw), lambda i: (0, 0))],
        out_specs=tokm,
        out_shape=jax.ShapeDtypeStruct((b * t, gw), BF16),
        compiler_params=_cparams(("parallel",)),
        name="rwkv_post",
    )(ys, bonus, gate, ln_g, ln_b)
    return o, s_fin


PAGES_PER_STEP = 4


def _past_stats_body(pt_ref, *refs, n_pages):
    pps = PAGES_PER_STEP
    kt_refs, kct_refs, vct_refs = refs[:pps], refs[pps:2 * pps], refs[2 * pps:3 * pps]
    (q_ref, w1r_ref, w1_ref, b1_ref, w2_ref, pe_ref, top_ref, cmp_ref, qb, gsum, kc_rows, vc_rows) = refs[3 * pps:]
    step = pl.program_id(1)
    pages_per_blk = MOBA_BLOCK // PAGE_SIZE
    nb = n_pages // pages_per_blk
    n_chunk = n_pages * PAGE_SIZE // CMP_STRIDE

    @pl.when(step == 0)
    def _():
        gsum[...] = jnp.zeros_like(gsum)
        qb[...] = jnp.broadcast_to(q_ref[...], qb.shape)

    for i in range(pps):
        p = step * pps + i
        gsum[p // pages_per_blk] += jnp.sum(kt_refs[i][...] * qb[...], axis=1)
        row0 = pl.multiple_of(p * PAGE_SIZE, PAGE_SIZE)
        kc_rows[pl.ds(row0, PAGE_SIZE), :] = kct_refs[i][...].reshape(LANE, PAGE_SIZE).T
        vc_rows[pl.ds(row0, PAGE_SIZE), :] = vct_refs[i][...].reshape(LANE, PAGE_SIZE).T

    @pl.when(step == n_pages // pps - 1)
    def _():
        gate = jnp.sum(gsum[...], axis=-1) * (1.0 / MOBA_BLOCK)
        ids = lax.broadcasted_iota(I32, (nb, N_HEADS), 0)
        rows8 = lax.broadcasted_iota(I32, (8, N_HEADS), 0)
        top = jnp.zeros((8, N_HEADS), I32)
        for j in range(MOBA_TOPK):
            best = jnp.max(gate, axis=0, keepdims=True)
            arg = jnp.min(jnp.where(gate == best, ids, nb), axis=0, keepdims=True)
            top = jnp.where(rows8 == j, arg, top)
            gate = jnp.where(ids == arg, -jnp.inf, gate)
        top_ref[...] = top
        for kv, src in enumerate((kc_rows, vc_rows)):
            bias = _cmp_bias(pe_ref, w1_ref, b1_ref, kv)
            out = _compress(lambda r: src[pl.ds(r, n_chunk, stride=CMP_STRIDE), :], n_chunk, w1r_ref, kv, bias,
                            w2_ref[kv])
            for g in range(NSA_KV_HEADS):
                cmp_ref[kv * NSA_KV_HEADS + g] = out[g]


def _past_stats(page_table, moba_t, nsa_t, q_col, cmp_w, layer):
    w1r, w1, b1, w2, pe = cmp_w
    bs, n_pages = page_table.shape
    nb = n_pages * PAGE_SIZE // MOBA_BLOCK
    rows = n_pages * PAGE_SIZE
    n_chunk = rows // CMP_STRIDE
    pps = PAGES_PER_STEP
    assert n_pages % pps == 0
    full = lambda a: pl.BlockSpec(a.shape, lambda i, p, pt: (0,) * a.ndim)
    k_page = lambda j: pl.BlockSpec((None, None, None, N_HEADS, HEAD_DIM, PAGE_SIZE),
                                    lambda i, p, pt: (layer, pt[i, p * pps + j], 0, 0, 0, 0))
    nsa_page = lambda t, j: pl.BlockSpec((None, None, None, NSA_KV_HEADS, HEAD_DIM, PAGE_SIZE),
                                         lambda i, p, pt: (layer, pt[i, p * pps + j], t, 0, 0, 0))
    return pl.pallas_call(
        functools.partial(_past_stats_body, n_pages=n_pages),
        grid_spec=pltpu.PrefetchScalarGridSpec(
            num_scalar_prefetch=1,
            grid=(bs, n_pages // pps),
            in_specs=[k_page(j) for j in range(pps)] + [nsa_page(0, j) for j in range(pps)]
                     + [nsa_page(1, j) for j in range(pps)]
                     + [pl.BlockSpec((None, N_HEADS, HEAD_DIM, 1), lambda i, p, pt: (i, 0, 0, 0)),
                        full(w1r), full(w1), full(b1), full(w2), full(pe)],
            out_specs=[pl.BlockSpec((None, 8, N_HEADS), lambda i, p, pt: (i, 0, 0)),
                       pl.BlockSpec((None, 2 * NSA_KV_HEADS, n_chunk, HEAD_DIM), lambda i, p, pt: (i, 0, 0, 0))],
            scratch_shapes=[pltpu.VMEM((N_HEADS, HEAD_DIM, PAGE_SIZE), F32), pltpu.VMEM((nb, N_HEADS, PAGE_SIZE), F32),
                            pltpu.VMEM((rows, LANE), F32), pltpu.VMEM((rows, LANE), F32)]),
        out_shape=[jax.ShapeDtypeStruct((bs, 8, N_HEADS), I32),
                   jax.ShapeDtypeStruct((bs, 2 * NSA_KV_HEADS, n_chunk, HEAD_DIM), F32)],
        compiler_params=_cparams(("parallel", "arbitrary")),
        name="past_stats",
    )(page_table, *([moba_t] * pps), *([nsa_t] * (2 * pps)), q_col, w1r, w1, b1, w2, pe)


def _moba_sample_body(pt_ref, top_ref, *refs):
    n_pg = (len(refs) - 4) // 2
    kt_refs, vt_refs = refs[:n_pg], refs[n_pg:2 * n_pg]
    q_ref, kn_ref, vn_ref, o_ref = refs[2 * n_pg:]
    q = q_ref[...] * ATTN_SCALE
    q8 = jnp.broadcast_to(q, (8, HEAD_DIM)).astype(BF16)
    s_all = [_dot(q8, kt[...].astype(BF16)) for kt in kt_refs]
    s_self = jnp.sum(q * kn_ref[...], axis=-1, keepdims=True)
    m = s_self
    for s in s_all:
        m = jnp.maximum(m, jnp.max(s, axis=-1, keepdims=True))
    e_self = jnp.exp(s_self - m)
    den = e_self
    acc = e_self * vn_ref[...]
    for s, vt in zip(s_all, vt_refs):
        pr = jnp.exp(s - m)
        den = den + jnp.sum(pr, axis=-1, keepdims=True)
        acc = acc + _dot_nt(pr.astype(BF16), vt[...].astype(BF16))
    o_ref[...] = (acc / den)[0:1, :]


def _moba_sample(page_table, top, moba_t, q, k_new, v_new, layer):
    bs = page_table.shape[0]
    ppb = MOBA_BLOCK // PAGE_SIZE

    def page_spec(kv, j, r):
        return pl.BlockSpec((None, None, None, None, HEAD_DIM, PAGE_SIZE),
                            lambda i, h, pt, tp: (layer, pt[i, tp[i, j, h] * ppb + r], kv, h, 0, 0))

    pages = [(j, r) for j in range(MOBA_TOPK) for r in range(ppb)]
    head = pl.BlockSpec((None, None, 1, HEAD_DIM), lambda i, h, pt, tp: (i, h, 0, 0))
    return pl.pallas_call(
        _moba_sample_body,
        grid_spec=pltpu.PrefetchScalarGridSpec(
            num_scalar_prefetch=2,
            grid=(bs, N_HEADS),
            in_specs=[page_spec(0, j, r) for j, r in pages] + [page_spec(1, j, r) for j, r in pages]
                     + [head, head, head],
            out_specs=head),
        out_shape=jax.ShapeDtypeStruct((bs, N_HEADS, 1, HEAD_DIM), F32),
        compiler_params=_cparams(("parallel", "parallel")),
        name="moba_sample",
    )(page_table, top, *([moba_t] * (2 * len(pages))), q, k_new, v_new)


def _heads_to_rows(row, g):
    parts = [row[:, (g * NSA_GROUP + j) * HEAD_DIM:(g * NSA_GROUP + j + 1) * HEAD_DIM] for j in range(NSA_GROUP)]
    return jnp.concatenate(parts + [jnp.zeros((8 - NSA_GROUP, HEAD_DIM), F32)], axis=0)


def _nsa_sample_sel_body(q_ref, cmp_ref, win_ref, kvn_ref, gate_ref, idx_ref, part_ref, *, past):
    hd = HEAD_DIM
    n_chunk = past // CMP_STRIDE
    n_slc = past // SLC_BLOCK + 1
    n_slc_pad = -(-n_slc // LANE) * LANE
    cur = past // SLC_BLOCK
    qrow = q_ref[...]
    gates = _sigmoid(gate_ref[...])
    kvn = kvn_ref[...]
    rows8 = lax.broadcasted_iota(I32, (8, 1), 0)
    ids = lax.broadcasted_iota(I32, (1, n_slc_pad), 1)
    m_iota = lax.broadcasted_iota(I32, (n_slc_pad, n_slc_pad), 0)
    j_iota = lax.broadcasted_iota(I32, (n_slc_pad, n_slc_pad), 1)
    lane = lax.broadcasted_iota(I32, (1, LANE), 1)
    idx_out = jnp.zeros((8, LANE), I32)
    for g in range(NSA_KV_HEADS):
        q8 = _heads_to_rows(qrow, g)
        q8b = q8.astype(BF16)
        n_ids = lax.broadcasted_iota(I32, (8, n_chunk), 1)
        l_cmp = _dot_nt(q8b, cmp_ref[g].astype(BF16)) * ATTN_SCALE
        p_cmp = _softmax_rows(l_cmp, n_ids * CMP_STRIDE + (CMP_LEN - 1) <= past)
        o_cmp = _dot(p_cmp.astype(BF16), cmp_ref[NSA_KV_HEADS + g].astype(BF16))
        imp = jnp.sum(jnp.where(rows8 < NSA_GROUP, p_cmp, 0.0), axis=0, keepdims=True)
        imp_slc = _dot(jnp.broadcast_to(imp, (8, n_chunk)), _slc_matrix(n_chunk, n_chunk - 1, n_slc_pad), HI)
        eligible = (ids <= cur) & (ids < n_slc)
        forced = (ids == 0) | (ids == cur) | (ids == cur - 1)
        score = jnp.where(eligible, jnp.where(forced, jnp.inf, imp_slc[0:1, :]), -jnp.inf)
        s_col = jnp.broadcast_to(score, (LANE, n_slc_pad)).T[:, 0:1]
        beats = (s_col > score) | ((s_col == score) & (m_iota < j_iota))
        rank = jnp.sum(beats.astype(F32), axis=0, keepdims=True)
        sel = jnp.where(eligible & (rank < SLC_TOPK), 1.0, 0.0)
        sel_col = jnp.broadcast_to(sel, (LANE, n_slc_pad)).T[:, 0:1]
        before = jnp.sum(jnp.where(m_iota < j_iota, sel_col, 0.0), axis=0, keepdims=True)
        idx_row = jnp.zeros((1, LANE), I32)
        for i in range(SLC_TOPK):
            hit = (sel > 0.5) & (before == i)
            idx_i = jnp.sum(jnp.where(hit, ids, 0), axis=-1, keepdims=True)
            idx_row = jnp.where(lane == i, idx_i, idx_row)
        idx_out = jnp.where(lax.broadcasted_iota(I32, (8, LANE), 0) == g, idx_row, idx_out)
        kw = win_ref[0, g].astype(BF16)
        vw = win_ref[1, g].astype(BF16)
        n_buf = win_ref.shape[-1]
        w_ids = lax.broadcasted_iota(I32, (8, n_buf), 1)
        l_win = _dot(q8b, kw) * ATTN_SCALE
        w_mask = w_ids >= n_buf - (WINDOW - 1)
        kw_new = kvn[:, 4 * NSA_KV_WIDTH + g * hd:4 * NSA_KV_WIDTH + (g + 1) * hd]
        vw_new = kvn[:, 5 * NSA_KV_WIDTH + g * hd:5 * NSA_KV_WIDTH + (g + 1) * hd]
        s_self = jnp.sum(q8 * kw_new, axis=-1, keepdims=True) * ATTN_SCALE
        m = jnp.maximum(jnp.max(jnp.where(w_mask, l_win, NEG_INF), axis=-1, keepdims=True), s_self)
        e = jnp.where(w_mask, jnp.exp(l_win - m), 0.0)
        e_self = jnp.exp(s_self - m)
        o_win = (_dot_nt(e.astype(BF16), vw) + e_self * vw_new) / (jnp.sum(e, axis=-1, keepdims=True) + e_self)
        for j in range(NSA_GROUP):
            h = g * NSA_GROUP + j
            part_ref[:, h * hd:(h + 1) * hd] = (gates[:, h:h + 1] * o_cmp[j:j + 1]
                                                + gates[:, 2 * N_HEADS + h:2 * N_HEADS + h + 1] * o_win[j:j + 1])
    idx_ref[...] = idx_out


def _nsa_sample_sel(q, cmp_tok, win_state, kv_new, gates, layer, past):
    bs = q.shape[0]
    n_buf = win_state.shape[-1]
    n_chunk = cmp_tok.shape[2]
    return pl.pallas_call(
        functools.partial(_nsa_sample_sel_body, past=past),
        grid=(bs,),
        in_specs=[pl.BlockSpec((None, 1, GROUP_WIDTH), lambda i: (i, 0, 0)),
                  pl.BlockSpec((None, 2 * NSA_KV_HEADS, n_chunk, HEAD_DIM), lambda i: (i, 0, 0, 0)),
                  pl.BlockSpec((None, None, 2, NSA_KV_HEADS, HEAD_DIM, n_buf), lambda i: (layer, i, 0, 0, 0, 0)),
                  pl.BlockSpec((None, 1, 6 * NSA_KV_WIDTH), lambda i: (i, 0, 0)),
                  pl.BlockSpec((None, 1, LANE), lambda i: (i, 0, 0))],
        out_specs=[pl.BlockSpec((None, 8, LANE), lambda i: (i, 0, 0)),
                   pl.BlockSpec((None, 1, GROUP_WIDTH), lambda i: (i, 0, 0))],
        out_shape=[jax.ShapeDtypeStruct((bs, 8, LANE), I32), jax.ShapeDtypeStruct((bs, 1, GROUP_WIDTH), F32)],
        compiler_params=_cparams(("parallel",)),
        name="nsa_sample_sel",
    )(q, cmp_tok, win_state, kv_new, gates)


def _nsa_sample_attn_body(pt_ref, idx_ref, ks0_ref, vs0_ref, ks1_ref, vs1_ref, q_ref, kvn_ref, gate_ref, part_ref,
                          o_ref, m_s, l_s, acc_s, *, n_past_blk):
    b, i = pl.program_id(0), pl.program_id(1)
    hd = HEAD_DIM
    qrow = q_ref[...]
    kvn = kvn_ref[...]
    blocks = ((ks0_ref, vs0_ref), (ks1_ref, vs1_ref))
    for g in range(NSA_KV_HEADS):
        q8 = _heads_to_rows(qrow, g)
        ks_new = kvn[:, 2 * NSA_KV_WIDTH + g * hd:2 * NSA_KV_WIDTH + (g + 1) * hd]
        vs_new = kvn[:, 3 * NSA_KV_WIDTH + g * hd:3 * NSA_KV_WIDTH + (g + 1) * hd]

        @pl.when(i == 0)
        def _():
            m_s[g] = jnp.broadcast_to(jnp.sum(q8 * ks_new, axis=-1, keepdims=True) * ATTN_SCALE, (8, hd))
            l_s[g] = jnp.ones((8, hd), F32)
            acc_s[g] = jnp.broadcast_to(vs_new, (8, hd))

        blk = idx_ref[b, g, i]
        half = lax.broadcasted_iota(I32, (8, PAGE_SIZE), 1) // SLC_BLOCK
        mask = half == jnp.where(blk < n_past_blk, blk % (PAGE_SIZE // SLC_BLOCK), -1)
        kt = blocks[g][0][...].astype(BF16)
        vt = blocks[g][1][...].astype(BF16)
        s = jnp.where(mask, _dot(q8.astype(BF16), kt) * ATTN_SCALE, NEG_INF)
        m_i = m_s[g][:, 0:1]
        m_new = jnp.maximum(m_i, jnp.max(s, axis=-1, keepdims=True))
        pr = jnp.where(mask, jnp.exp(s - m_new), 0.0)
        alpha = jnp.exp(m_i - m_new)
        l_s[g] = alpha * l_s[g] + jnp.sum(pr, axis=-1, keepdims=True)
        acc_s[g] = alpha * acc_s[g] + _dot_nt(pr.astype(BF16), vt)
        m_s[g] = jnp.broadcast_to(m_new, (8, hd))

    @pl.when(i == pl.num_programs(1) - 1)
    def _():
        gates = _sigmoid(gate_ref[...])
        for g in range(NSA_KV_HEADS):
            o_sel = acc_s[g] / l_s[g]
            for j in range(NSA_GROUP):
                h = g * NSA_GROUP + j
                sl = slice(h * hd, (h + 1) * hd)
                o_ref[:, sl] = (part_ref[:, sl] + gates[:, N_HEADS + h:N_HEADS + h + 1] * o_sel[j:j + 1]
                                ).astype(o_ref.dtype)


def _nsa_sample_attn(page_table, sel_idx, nsa_cache, q, kv_new, gates, part, layer, past):
    bs = page_table.shape[0]
    n_past_blk = past // SLC_BLOCK
    per_page = PAGE_SIZE // SLC_BLOCK

    def blk_spec(g, t):
        def imap(b, i, pt, ix):
            blk = jnp.minimum(ix[b, g, i], n_past_blk - 1)
            return (layer, pt[b, blk // per_page], t, g, 0, 0)
        return pl.BlockSpec((None, None, None, None, HEAD_DIM, PAGE_SIZE), imap)

    row = lambda w: pl.BlockSpec((None, 1, w), lambda b, i, pt, ix: (b, 0, 0))
    return pl.pallas_call(
        functools.partial(_nsa_sample_attn_body, n_past_blk=n_past_blk),
        grid_spec=pltpu.PrefetchScalarGridSpec(
            num_scalar_prefetch=2,
            grid=(bs, SLC_TOPK),
            in_specs=[blk_spec(0, 2), blk_spec(0, 3), blk_spec(1, 2), blk_spec(1, 3),
                      row(GROUP_WIDTH), row(6 * NSA_KV_WIDTH), row(LANE), row(GROUP_WIDTH)],
            out_specs=row(GROUP_WIDTH),
            scratch_shapes=[pltpu.VMEM((NSA_KV_HEADS, 8, HEAD_DIM), F32)] * 3),
        out_shape=jax.ShapeDtypeStruct((bs, 1, GROUP_WIDTH), F32),
        compiler_params=_cparams(("parallel", "arbitrary")),
        name="nsa_sample_attn",
    )(page_table, sel_idx, nsa_cache, nsa_cache, nsa_cache, nsa_cache, q, kv_new, gates, part)


def _recur_sample_body(c_ref, shift_ref, mu_ref, w0_ref, w2_ref, a0_ref, a2_ref, g2_ref, kk_ref, ka_ref, rk_ref,
                       lng_ref, lnb_ref, s_rw_ref, ret_ref, cos_ref, sin_ref, gam_ref, gn_g_ref, gn_b_ref, s_rt_ref,
                       o_rw_ref, s_rw_out, o_rt_ref, s_rt_out, o_scr):
    gw = GROUP_WIDTH
    hd = HEAD_DIM
    p = (mu_ref[...], w0_ref[...], w2_ref[...], a0_ref[...], a2_ref[...], g2_ref[...], kk_ref[...], ka_ref[...])
    r, w, k, v, kk, kka, gate = _rwkv_prep(c_ref[...], shift_ref[...], p)

    def put_state(h, j, s):
        s_rw_out[j, h] = s

    ys = _rwkv_steps((r, w, k, v, kk, kka), lambda h, j: s_rw_ref[j, h], put_state, 1)
    o_rw_ref[...] = _rwkv_post(ys, r, k, v, gate, rk_ref[...], lng_ref[...], lnb_ref[...]).astype(o_rw_ref.dtype)

    cos = cos_ref[...]
    sin = sin_ref[...]
    q = _rope(ret_ref[:, 0:gw], cos, sin)
    kr = _rope(ret_ref[:, gw:2 * gw], cos, sin) * ATTN_SCALE
    vr = ret_ref[:, 2 * gw:3 * gw]
    gam = gam_ref[...]
    qk = _dot(q * kr, _head_blockdiag(gw), HI)
    pad = jnp.zeros((LANE - 8, LANE), F32)
    for hp in range(N_HEADS // 2):
        ps = slice(hp * LANE, (hp + 1) * LANE)
        qt = jnp.concatenate([q[:, ps], pad], axis=0).T
        kt = jnp.concatenate([kr[:, ps], pad], axis=0).T
        for h2 in range(2):
            h = 2 * hp + h2
            sl = slice(h * hd, (h + 1) * hd)
            for j in range(8):
                s = s_rt_ref[j, h]
                qcol = qt[h2 * hd:(h2 + 1) * hd, j:j + 1]
                kcol = kt[h2 * hd:(h2 + 1) * hd, j:j + 1]
                g_h = gam[:, sl]
                o_scr[j:j + 1, sl] = (qk[j:j + 1, sl] * vr[j:j + 1, sl]
                                      + g_h * jnp.sum(qcol * s, axis=0, keepdims=True))
                s_rt_out[j, h] = s * g_h + kcol * vr[j:j + 1, sl]
    gt = ret_ref[:, 3 * gw:4 * gw]
    y = _head_norm(o_scr[...], gn_g_ref[...], gn_b_ref[...], GN_EPS)
    o_rt_ref[...] = (gt * _sigmoid(gt) * y).astype(o_rt_ref.dtype)


def _recur_sample(c_rwkv, shift, rwkv_params, s_rwkv, c_ret, rope, gamma, gn_g, gn_b, s_ret):
    bs = c_rwkv.shape[0]
    cos, sin = rope
    st = jax.ShapeDtypeStruct((bs, N_HEADS, HEAD_DIM, HEAD_DIM), F32)
    ob = jax.ShapeDtypeStruct((bs, GROUP_WIDTH), BF16)
    return pl.pallas_call(
        _recur_sample_body,
        out_shape=[ob, st, ob, st],
        scratch_shapes=[pltpu.VMEM((bs, GROUP_WIDTH), F32)],
        compiler_params=pltpu.CompilerParams(vmem_limit_bytes=VMEM_LIMIT),
        name="recur_sample",
    )(c_rwkv, shift, *rwkv_params, s_rwkv, c_ret, cos, sin, gamma, gn_g, gn_b, s_ret)


def _prep_rwkv_params(mu, w0, w2, a0, a2, g2, k_k, k_a, r_k, ln_g, ln_b):
    row = lambda z: z.reshape(1, -1)
    return (row(mu), row(w0), w2, row(a0), a2, g2, row(k_k), row(k_a), row(r_k), row(ln_g), row(ln_b))


def _prep_cmp_weights(w1, b1, w2, pe):
    span = CMP_LEN // CMP_STRIDE
    w1r = w1.reshape(2, span, CMP_STRIDE, HEAD_DIM, CMP_HIDDEN)
    w1r = jnp.transpose(w1r, (0, 2, 3, 1, 4)).reshape(2, CMP_STRIDE, HEAD_DIM, span * CMP_HIDDEN).astype(BF16)
    return (w1r, w1, b1.reshape(2, 1, CMP_HIDDEN), w2, pe.reshape(2, 1, CMP_LEN * HEAD_DIM))


def _pad_w_in(w):
    o_nsa = MOBA_COLS
    o_gate = o_nsa + GROUP_WIDTH + 6 * NSA_KV_WIDTH
    o_rwkv = o_nsa + NSA_COLS
    o_ret = o_rwkv + RWKV_COLS
    zeros = jnp.zeros((w.shape[0], N_PAD - OFF_GATE - 3 * N_HEADS), w.dtype)
    return jnp.concatenate([w[:, o_rwkv:o_ret], w[:, :o_gate], w[:, o_ret:], w[:, o_gate:o_rwkv], zeros], axis=1)


def kernel(x_prompt, x_sample, cache_moba_kv, cache_nsa_kv, state_nsa_win, state_rwkv, state_rwkv_shift, state_ret,
           page_table, norm_g, w_in, w_out, w_up, w_down, nsa_cmp_pe, nsa_cmp_w1, nsa_cmp_b1, nsa_cmp_w2, rwkv_mu,
           rwkv_w0, rwkv_w2, rwkv_a0, rwkv_a2, rwkv_g2, rwkv_k_k, rwkv_k_a, rwkv_r_k, rwkv_ln_g, rwkv_ln_b, ret_gn_g,
           ret_gn_b):
    bp, t, d = x_prompt.shape
    bs = x_sample.shape[0]
    assert x_sample.shape[1] == 1 and d == D_MODEL
    depth = w_in.shape[0]
    past = page_table.shape[1] * PAGE_SIZE
    gw = GROUP_WIDTH
    hd = HEAD_DIM

    log_gamma = jnp.log(1.0 - jnp.exp2(-5.0 - jnp.arange(N_HEADS, dtype=F32)))
    ret_tables = _ret_tables(log_gamma, RET_CHUNK)
    rope_p = _rope_tables(jnp.arange(t, dtype=I32))
    rope_s = _rope_tables(jnp.full((1,), past, I32))
    gamma_row = jnp.repeat(jnp.exp(log_gamma), hd)[None, :]
    moba_t = jnp.transpose(cache_moba_kv, (0, 1, 3, 4, 5, 2))
    nsa_t = jnp.transpose(cache_nsa_kv, (0, 1, 3, 4, 5, 2))
    win_t = jnp.transpose(state_nsa_win, (0, 1, 3, 4, 5, 2))

    xp = x_prompt.reshape(bp * t, d)
    xs = x_sample.reshape(bs, d)
    zero_state = jnp.zeros((bp, N_HEADS, hd, hd), F32)
    zero_shift = jnp.zeros((bp, RWKV_COLS), F32)
    st_p, st_s = [], []
    for l in range(depth):
        g = norm_g[l].reshape(4, 1, d)
        w_in_l = _pad_w_in(w_in[l]).astype(BF16)
        w_out_l = w_out[l].astype(BF16)
        w_up_l = w_up[l].astype(BF16)
        w_down_l = w_down[l].astype(BF16)
        cmp_w = _prep_cmp_weights(nsa_cmp_w1[l], nsa_cmp_b1[l], nsa_cmp_w2[l], nsa_cmp_pe[l])
        rwkv_p = _prep_rwkv_params(rwkv_mu[l], rwkv_w0[l], rwkv_w2[l], rwkv_a0[l], rwkv_a2[l], rwkv_g2[l],
                                   rwkv_k_k[l], rwkv_k_a[l], rwkv_r_k[l], rwkv_ln_g[l], rwkv_ln_b[l])
        gn_g = ret_gn_g[l].reshape(1, gw)
        gn_b = ret_gn_b[l].reshape(1, gw)

        cols = _inproj(xp, g[0], w_in_l, min(1024, bp * t), 512)
        o_moba = _moba_prompt(cols, bp, t)
        o_nsa = _nsa_prompt(cols, cmp_w, bp, t)
        o_rwkv, rwkv_s = _rwkv_prompt(cols, zero_shift, zero_state, rwkv_p, bp, t)
        o_ret, ret_s = _ret_prompt(cols, zero_state, ret_tables, rope_p, gn_g, gn_b, bp, t)
        xp = _outproj((o_moba, o_nsa, o_rwkv, o_ret), xp, g[1], w_out_l, 256)
        xp = _ffn(xp, g[2], g[3], w_up_l, w_down_l, min(512, bp * t), 512)
        c3 = cols.reshape(bp, t, N_PAD)
        win_keep = min(WINDOW, t)
        st_p.append((c3[:, :, OFF_MOBA + gw:OFF_MOBA + 3 * gw].reshape(bp, t, 2, N_HEADS, hd),
                     c3[:, :, OFF_NSAKV:OFF_NSAKV + 4 * NSA_KV_WIDTH].reshape(bp, t, 4, NSA_KV_HEADS, hd),
                     c3[:, t - win_keep:, OFF_NSAKV + 4 * NSA_KV_WIDTH:OFF_NSAKV + 6 * NSA_KV_WIDTH
                        ].reshape(bp, win_keep, 2, NSA_KV_HEADS, hd),
                     rwkv_s, c3[:, t - 1, OFF_RWKV:OFF_RWKV + RWKV_COLS], ret_s))

        cs = _inproj(xs, g[0], w_in_l, bs, 512)
        heads = lambda z: z.reshape(bs, N_HEADS, 1, hd)
        q_m = cs[:, OFF_MOBA:OFF_MOBA + gw]
        k_m = cs[:, OFF_MOBA + gw:OFF_MOBA + 2 * gw]
        v_m = cs[:, OFF_MOBA + 2 * gw:OFF_MOBA + 3 * gw]
        top, cmp_tok = _past_stats(page_table, moba_t, nsa_t, q_m.reshape(bs, N_HEADS, hd, 1), cmp_w, l)
        o_moba_s = _moba_sample(page_table, top[:, :MOBA_TOPK, :], moba_t, heads(q_m), heads(k_m), heads(v_m), l)
        q_n = cs[:, OFF_NSAQ:OFF_NSAQ + gw].reshape(bs, 1, gw)
        kv_new = cs[:, OFF_NSAKV:OFF_NSAKV + 6 * NSA_KV_WIDTH].reshape(bs, 1, 6 * NSA_KV_WIDTH)
        gates = cs[:, OFF_GATE:OFF_GATE + LANE].reshape(bs, 1, LANE)
        sel_idx, part = _nsa_sample_sel(q_n, cmp_tok, win_t, kv_new, gates, l, past)
        o_nsa_s = _nsa_sample_attn(page_table, sel_idx[:, :NSA_KV_HEADS, :SLC_TOPK], nsa_t, q_n, kv_new, gates, part,
                                   l, past)
        o_rwkv_s, rwkv_s_s, o_ret_s, ret_s_s = _recur_sample(
            cs[:, OFF_RWKV:OFF_RWKV + RWKV_COLS], state_rwkv_shift[l], rwkv_p, state_rwkv[l],
            cs[:, OFF_RET:OFF_RET + RET_COLS], rope_s, gamma_row, gn_g, gn_b, state_ret[l])
        parts_s = (o_moba_s.reshape(bs, gw).astype(BF16), o_nsa_s.reshape(bs, gw).astype(BF16), o_rwkv_s, o_ret_s)
        xs = _outproj(parts_s, xs, g[1], w_out_l, bs)
        xs = _ffn(xs, g[2], g[3], w_up_l, w_down_l, bs, 512)
        win_new = cs[:, OFF_NSAKV + 4 * NSA_KV_WIDTH:OFF_NSAKV + 6 * NSA_KV_WIDTH].reshape(bs, 1, 2, NSA_KV_HEADS, hd)
        win_all = jnp.concatenate([state_nsa_win[l], win_new], axis=1)
        keep_s = min(WINDOW, win_all.shape[1])
        st_s.append((cs[:, OFF_MOBA + gw:OFF_MOBA + 3 * gw].reshape(bs, 1, 2, N_HEADS, hd),
                     cs[:, OFF_NSAKV:OFF_NSAKV + 4 * NSA_KV_WIDTH].reshape(bs, 1, 4, NSA_KV_HEADS, hd),
                     win_all[:, win_all.shape[1] - keep_s:],
                     rwkv_s_s, cs[:, OFF_RWKV:OFF_RWKV + RWKV_COLS], ret_s_s))

    stk = lambda sts, i: jnp.stack([s[i] for s in sts], axis=0)
    outs = [xp.reshape(bp, t, d), xs.reshape(bs, 1, d)]
    for i in range(6):
        outs += [stk(st_p, i), stk(st_s, i)]
    return tuple(outs)
```

```python
import functools

import jax
import jax.numpy as jnp
import numpy as np
from jax import lax
from jax.experimental import pallas as pl
from jax.experimental.pallas import tpu as pltpu

F32 = jnp.float32
BF16 = jnp.bfloat16
I32 = jnp.int32
HI = lax.Precision.HIGHEST

D_MODEL = 2048
HEAD_DIM = 64
GROUP_WIDTH = D_MODEL // 4
N_HEADS = GROUP_WIDTH // HEAD_DIM
D_FF = 4 * D_MODEL
RMS_EPS = 1e-6
GN_EPS = 1e-5
NEG_INF = -1e30
ATTN_SCALE = HEAD_DIM ** -0.5
PAGE_SIZE = 128
MOBA_BLOCK = 256
MOBA_TOPK = 3
NSA_KV_HEADS = 2
NSA_GROUP = N_HEADS // NSA_KV_HEADS
NSA_KV_WIDTH = NSA_KV_HEADS * HEAD_DIM
CMP_LEN = 32
CMP_STRIDE = 16
CMP_HIDDEN = 2 * HEAD_DIM
SLC_BLOCK = 64
SLC_TOPK = 16
WINDOW = 512
RWKV_DECAY_RANK = 64
RWKV_AAA_RANK = 64
RWKV_GATE_RANK = 128
RWKV_LN_EPS = 64e-5
RET_CHUNK = 128
ROPE_BASE = 10000.0

MOBA_COLS = 3 * GROUP_WIDTH
NSA_COLS = GROUP_WIDTH + 6 * NSA_KV_WIDTH + 3 * N_HEADS
RWKV_COLS = 3 * GROUP_WIDTH + RWKV_DECAY_RANK + RWKV_AAA_RANK + RWKV_GATE_RANK
RET_COLS = 4 * GROUP_WIDTH
IN_COLS = MOBA_COLS + NSA_COLS + RWKV_COLS + RET_COLS

OFF_RWKV = 0
OFF_MOBA = OFF_RWKV + RWKV_COLS
OFF_NSAQ = OFF_MOBA + MOBA_COLS
OFF_NSAKV = OFF_NSAQ + GROUP_WIDTH
OFF_RET = OFF_NSAKV + 6 * NSA_KV_WIDTH
OFF_GATE = OFF_RET + RET_COLS
N_PAD = OFF_GATE + 512

LANE = 128
VMEM_LIMIT = 56 * 1024 * 1024


def _cparams(sem):
    return pltpu.CompilerParams(dimension_semantics=sem, vmem_limit_bytes=VMEM_LIMIT)


def _dot(a, b, precision=None):
    return jnp.dot(a, b, preferred_element_type=F32, precision=precision)


def _dot_nt(a, b, precision=None):
    return lax.dot_general(a, b, (((1,), (1,)), ((), ())), preferred_element_type=F32, precision=precision)


def _dot_tn(a, b, precision=None):
    return lax.dot_general(a, b, (((0,), (0,)), ((), ())), preferred_element_type=F32, precision=precision)


def _head_blockdiag(n, scale=1.0):
    r = lax.broadcasted_iota(I32, (n, n), 0) // HEAD_DIM
    c = lax.broadcasted_iota(I32, (n, n), 1) // HEAD_DIM
    return jnp.where(r == c, scale, 0.0).astype(F32)


def _sigmoid(x):
    return 1.0 / (1.0 + jnp.exp(-x))


def _inproj_body(x_ref, g_ref, w_ref, o_ref, h_scr):
    @pl.when(pl.program_id(1) == 0)
    def _():
        x = x_ref[...]
        ms = jnp.mean(x * x, axis=-1, keepdims=True)
        h_scr[...] = (x * lax.rsqrt(ms + RMS_EPS) * g_ref[...]).astype(BF16)

    o_ref[...] = _dot(h_scr[...], w_ref[...])


def _inproj(x, g, w, tm, tn):
    m, d = x.shape
    n = w.shape[1]
    return pl.pallas_call(
        _inproj_body,
        grid=(m // tm, n // tn),
        in_specs=[pl.BlockSpec((tm, d), lambda i, j: (i, 0)),
                  pl.BlockSpec((1, d), lambda i, j: (0, 0)),
                  pl.BlockSpec((d, tn), lambda i, j: (0, j))],
        out_specs=pl.BlockSpec((tm, tn), lambda i, j: (i, j)),
        out_shape=jax.ShapeDtypeStruct((m, n), F32),
        scratch_shapes=[pltpu.VMEM((tm, d), BF16)],
        compiler_params=_cparams(("parallel", "arbitrary")),
        name="inproj",
    )(x, g, w)


def _outproj_body(a_ref, b_ref, c_ref, d_ref, x_ref, g_ref, w_ref, o_ref):
    gw = GROUP_WIDTH
    y = _dot(a_ref[...], w_ref[0:gw, :])
    y += _dot(b_ref[...], w_ref[gw:2 * gw, :])
    y += _dot(c_ref[...], w_ref[2 * gw:3 * gw, :])
    y += _dot(d_ref[...], w_ref[3 * gw:4 * gw, :])
    ms = jnp.mean(y * y, axis=-1, keepdims=True)
    o_ref[...] = x_ref[...] + y * lax.rsqrt(ms + RMS_EPS) * g_ref[...]


def _outproj(parts, x, g, w, tm):
    m, d = x.shape
    gw = GROUP_WIDTH
    part_spec = pl.BlockSpec((tm, gw), lambda i: (i, 0))
    return pl.pallas_call(
        _outproj_body,
        grid=(m // tm,),
        in_specs=[part_spec, part_spec, part_spec, part_spec,
                  pl.BlockSpec((tm, d), lambda i: (i, 0)),
                  pl.BlockSpec((1, d), lambda i: (0, 0)),
                  pl.BlockSpec((4 * gw, d), lambda i: (0, 0))],
        out_specs=pl.BlockSpec((tm, d), lambda i: (i, 0)),
        out_shape=jax.ShapeDtypeStruct((m, d), F32),
        compiler_params=_cparams(("parallel",)),
        name="outproj",
    )(*parts, x, g, w)


def _ffn_body(x_ref, g2_ref, g3_ref, wu_ref, wd_ref, o_ref, h_scr, acc_scr):
    f = pl.program_id(1)

    @pl.when(f == 0)
    def _():
        x = x_ref[...]
        ms = jnp.mean(x * x, axis=-1, keepdims=True)
        h_scr[...] = (x * lax.rsqrt(ms + RMS_EPS) * g2_ref[...]).astype(BF16)
        acc_scr[...] = jnp.zeros_like(acc_scr)

    u = jnp.maximum(_dot(h_scr[...], wu_ref[...]), 0.0)
    acc_scr[...] += _dot((u * u).astype(BF16), wd_ref[...])

    @pl.when(f == pl.num_programs(1) - 1)
    def _():
        y = acc_scr[...]
        ms = jnp.mean(y * y, axis=-1, keepdims=True)
        o_ref[...] = x_ref[...] + y * lax.rsqrt(ms + RMS_EPS) * g3_ref[...]


def _ffn(x, g2, g3, wu, wd, tm, tf):
    m, d = x.shape
    f = wu.shape[1]
    return pl.pallas_call(
        _ffn_body,
        grid=(m // tm, f // tf),
        in_specs=[pl.BlockSpec((tm, d), lambda i, j: (i, 0)),
                  pl.BlockSpec((1, d), lambda i, j: (0, 0)),
                  pl.BlockSpec((1, d), lambda i, j: (0, 0)),
                  pl.BlockSpec((d, tf), lambda i, j: (0, j)),
                  pl.BlockSpec((tf, d), lambda i, j: (j, 0))],
        out_specs=pl.BlockSpec((tm, d), lambda i, j: (i, 0)),
        out_shape=jax.ShapeDtypeStruct((m, d), F32),
        scratch_shapes=[pltpu.VMEM((tm, d), BF16), pltpu.VMEM((tm, d), F32)],
        compiler_params=_cparams(("parallel", "arbitrary")),
        name="ffn",
    )(x, g2, g3, wu, wd)


MOBA_HEAD_GROUP = 4


def _moba_prompt_body(q_ref, k_ref, v_ref, o_ref, kmean_scr, kext, vext, *, nb):
    blk = MOBA_BLOCK
    hg = MOBA_HEAD_GROUP
    qt = pl.program_id(2)
    pair_of = lambda j: slice(LANE * (j // 2), LANE * (j // 2 + 1))
    dst_of = lambda j: slice(LANE * j, LANE * (j + 1))

    @pl.when(qt == 0)
    def _():
        kmean_scr[...] = jnp.zeros_like(kmean_scr)
        mine_t = lax.broadcasted_iota(I32, (k_ref.shape[0], LANE), 1) // HEAD_DIM
        for j in range(hg):
            kp = jnp.where(mine_t == j % 2, k_ref[:, pair_of(j)], 0.0)
            kext[:, dst_of(j)] = kp.astype(BF16)
            vext[:, dst_of(j)] = jnp.where(mine_t == j % 2, v_ref[:, pair_of(j)], 1.0).astype(BF16)
            for n in range(nb):
                kmean_scr[n:n + 1, dst_of(j)] = jnp.sum(kp[n * blk:(n + 1) * blk], axis=0, keepdims=True) * (1.0 / blk)

    row = lax.broadcasted_iota(I32, (blk, blk), 0)
    col = lax.broadcasted_iota(I32, (blk, blk), 1)
    bias_own = jnp.where(col <= row, 0.0, NEG_INF)
    blk_id = lax.broadcasted_iota(I32, (blk, LANE), 1)
    eligible = blk_id < qt
    not_sel, qb = [], []
    for j in range(hg):
        q = q_ref[:, pair_of(j)]
        gate = jnp.where(eligible, _dot_nt(q, kmean_scr[:, dst_of(j)], HI), -jnp.inf)
        rank = jnp.zeros((blk, LANE), I32)
        for m in range(nb):
            gm = gate[:, m:m + 1]
            rank += ((gm > gate) | ((gm == gate) & (m < blk_id))).astype(I32)
        not_sel.append(jnp.where(eligible & (rank < MOBA_TOPK), 0.0, 1.0).astype(BF16))
        qb.append((q * ATTN_SCALE).astype(BF16))

    def attend(carry, start, biases):
        out = []
        for j in range(hg):
            m_i, acc = carry[j]
            s = _dot_nt(qb[j], kext[pl.ds(start, blk), dst_of(j)]) + biases[j]
            m_new = jnp.maximum(m_i, jnp.max(s, axis=-1, keepdims=True))
            p = jnp.exp(s - m_new)
            out.append((m_new, jnp.exp(m_i - m_new) * acc + _dot(p.astype(BF16), vext[pl.ds(start, blk), dst_of(j)])))
        return tuple(out)

    def body(n, carry):
        pick = (lax.broadcasted_iota(I32, (LANE, blk), 0) == n).astype(BF16)
        return attend(carry, pl.multiple_of(n * blk, blk), [_dot(ns, pick) * NEG_INF for ns in not_sel])

    init = tuple((jnp.full((blk, 1), NEG_INF, F32), jnp.zeros((blk, LANE), F32)) for _ in range(hg))
    carry = lax.fori_loop(0, qt, body, init)
    fin = attend(carry, pl.multiple_of(qt * blk, blk), [bias_own] * hg)
    half = lax.broadcasted_iota(I32, (blk, LANE), 1) // HEAD_DIM
    for jp in range(hg // 2):
        outs = [fin[2 * jp + e][1] / pltpu.roll(fin[2 * jp + e][1], HEAD_DIM, 1) for e in range(2)]
        o_ref[:, dst_of(jp)] = jnp.where(half == 0, outs[0], outs[1]).astype(o_ref.dtype)


def _moba_prompt(cols, b, t):
    blk = MOBA_BLOCK
    nb = t // blk
    w = MOBA_HEAD_GROUP * HEAD_DIM
    qoff = OFF_MOBA // w
    koff = (OFF_MOBA + GROUP_WIDTH) // w
    voff = (OFF_MOBA + 2 * GROUP_WIDTH) // w
    return pl.pallas_call(
        functools.partial(_moba_prompt_body, nb=nb),
        grid=(b, GROUP_WIDTH // w, nb),
        in_specs=[pl.BlockSpec((blk, w), lambda i, h, q: (i * nb + q, qoff + h)),
                  pl.BlockSpec((t, w), lambda i, h, q: (i, koff + h)),
                  pl.BlockSpec((t, w), lambda i, h, q: (i, voff + h))],
        out_specs=pl.BlockSpec((blk, w), lambda i, h, q: (i * nb + q, h)),
        out_shape=jax.ShapeDtypeStruct((b * t, GROUP_WIDTH), BF16),
        scratch_shapes=[pltpu.VMEM((LANE, MOBA_HEAD_GROUP * LANE), F32),
                        pltpu.VMEM((t, MOBA_HEAD_GROUP * LANE), BF16), pltpu.VMEM((t, MOBA_HEAD_GROUP * LANE), BF16)],
        compiler_params=_cparams(("parallel", "parallel", "arbitrary")),
        name="moba_prompt",
    )(cols, cols, cols)


def _gelu_tanh(x):
    return x * (0.5 * (1.0 + jnp.tanh(np.sqrt(2.0 / np.pi).astype(np.float32) * (x + 0.044715 * (x * x * x)))))


def _cmp_bias(pe_ref, w1_ref, b1_ref, kv):
    pe8 = jnp.broadcast_to(pe_ref[kv], (8, CMP_LEN * HEAD_DIM))
    return _dot(pe8, w1_ref[kv], HI)[0:1, :] + b1_ref[kv]


def _compress(load_rows, n, w1r_ref, kv, bias, w2):
    acc = [jnp.zeros((n, 2 * CMP_HIDDEN), F32) for _ in range(NSA_KV_HEADS)]
    for r in range(CMP_STRIDE):
        x = load_rows(r).astype(BF16)
        for g in range(NSA_KV_HEADS):
            acc[g] += _dot(x[:, g * HEAD_DIM:(g + 1) * HEAD_DIM], w1r_ref[kv, r])
    out = []
    for g in range(NSA_KV_HEADS):
        hid = acc[g][:, :CMP_HIDDEN] + pltpu.roll(acc[g][:, CMP_HIDDEN:], n - 1, 0) + bias
        out.append(_dot(_gelu_tanh(hid).astype(BF16), w2.astype(BF16)))
    return out


def _softmax_rows(l, mask):
    m = jnp.max(jnp.where(mask, l, NEG_INF), axis=-1, keepdims=True)
    e = jnp.where(mask, jnp.exp(l - m), 0.0)
    s = jnp.sum(e, axis=-1, keepdims=True)
    return jnp.where(s > 0.0, e / jnp.where(s > 0.0, s, 1.0), 0.0)


def _slc_matrix(n_cmp_pad, n_cmp, n_slc_pad):
    n = lax.broadcasted_iota(I32, (n_cmp_pad, n_slc_pad), 0)
    j = lax.broadcasted_iota(I32, (n_cmp_pad, n_slc_pad), 1)
    ratio = SLC_BLOCK // CMP_STRIDE
    return ((n >= ratio * j - 1) & (n <= ratio * j + ratio - 1) & (n < n_cmp)).astype(F32)


def _topk_rows(score, ids, n_cand, k):
    rank = jnp.zeros(score.shape, I32)
    for m in range(n_cand):
        sm = score[:, m:m + 1]
        rank += ((sm > score) | ((sm == score) & (m < ids))).astype(I32)
    return rank < k


def _nsa_prompt_body(q0_ref, q1_ref, kc_ref, vc_ref, ksvs_ref, kwvw_ref, gate_ref, w1r_ref, w1_ref, b1_ref, w2_ref,
                     pe_ref, o_ref, ck_scr, cv_scr, ks_ext, vs_ext, kw_ext, vw_ext, *, t):
    tq = 256
    n_chunk = t // CMP_STRIDE
    n_cmp = n_chunk - CMP_LEN // CMP_STRIDE + 1
    qt = pl.program_id(1)
    hd = HEAD_DIM

    @pl.when(qt == 0)
    def _():
        for kv, (src, dst) in enumerate(((kc_ref, ck_scr), (vc_ref, cv_scr))):
            bias = _cmp_bias(pe_ref, w1_ref, b1_ref, kv)
            out = _compress(lambda r: src[pl.ds(r, n_chunk, stride=CMP_STRIDE), :],
                            n_chunk, w1r_ref, kv, bias, w2_ref[kv])
            for g in range(NSA_KV_HEADS):
                dst[g] = out[g]
        half_t = lax.broadcasted_iota(I32, (t, LANE), 1) // hd
        for src, k_ext, v_ext in ((ksvs_ref, ks_ext, vs_ext), (kwvw_ref, kw_ext, vw_ext)):
            for g in range(NSA_KV_HEADS):
                k_ext[g] = jnp.where(half_t == g, src[:, 0:LANE], 0.0).astype(BF16)
                v_ext[g] = jnp.where(half_t == g, src[:, LANE:2 * LANE], 1.0).astype(BF16)

    tpos = qt * tq + lax.broadcasted_iota(I32, (tq, 1), 0)
    tpos4 = jnp.concatenate([tpos] * NSA_GROUP, axis=0)
    gates = _sigmoid(gate_ref[...])
    slc_ids = lax.broadcasted_iota(I32, (tq, LANE), 1)
    n_slc = t // SLC_BLOCK
    kcol = lax.broadcasted_iota(I32, (tq, tq), 1)
    jrow = lax.broadcasted_iota(I32, (LANE, tq), 0)
    kcol_e = lax.broadcasted_iota(I32, (LANE, tq), 1)
    for g in range(NSA_KV_HEADS):
        q_ref = q0_ref if g == 0 else q1_ref
        q4f = jnp.concatenate([q_ref[:, j * hd:(j + 1) * hd] for j in range(NSA_GROUP)], axis=0)
        q4 = q4f.astype(BF16)
        cmp_end = lax.broadcasted_iota(I32, (NSA_GROUP * tq, n_chunk), 1) * CMP_STRIDE + (CMP_LEN - 1)
        l_cmp = _dot_nt(q4, ck_scr[g].astype(BF16)) * ATTN_SCALE
        p_cmp = _softmax_rows(l_cmp, cmp_end <= tpos4)
        o_cmp = _dot(p_cmp.astype(BF16), cv_scr[g].astype(BF16))
        imp = p_cmp[0:tq]
        for j in range(1, NSA_GROUP):
            imp = imp + p_cmp[j * tq:(j + 1) * tq]
        imp_slc = _dot(imp, _slc_matrix(n_chunk, n_cmp, LANE), HI)
        cur = tpos // SLC_BLOCK
        eligible = (slc_ids <= cur) & (slc_ids < n_slc)
        forced = (slc_ids == 0) | (slc_ids == cur) | (slc_ids == cur - 1)
        score = jnp.where(eligible, jnp.where(forced, jnp.inf, imp_slc), -jnp.inf)
        sel = (eligible & _topk_rows(score, slc_ids, n_slc, SLC_TOPK)).astype(BF16)

        half_q = lax.broadcasted_iota(I32, (tq, LANE), 1) // hd
        q4e = []
        for j in range(NSA_GROUP):
            qp = q_ref[:, (j // 2) * LANE:(j // 2 + 1) * LANE]
            qp = qp if j % 2 == g else pltpu.roll(qp, hd, 1)
            q4e.append(jnp.where(half_q == g, qp * ATTN_SCALE, 0.0).astype(BF16))
        q4e = jnp.concatenate(q4e, axis=0)

        def attend(carry, start, bias, k_ext, v_ext):
            m_i, acc = carry
            s = (_dot_nt(q4e, k_ext[g, pl.ds(start, tq), :]).reshape(NSA_GROUP, tq, tq) + bias[None]
                 ).reshape(NSA_GROUP * tq, tq)
            m_new = jnp.maximum(m_i, jnp.max(s, axis=-1, keepdims=True))
            p = jnp.exp(s - m_new)
            return m_new, jnp.exp(m_i - m_new) * acc + _dot(p.astype(BF16), v_ext[g, pl.ds(start, tq), :])

        def sel_body(n, carry):
            expand = (jrow == (tq // SLC_BLOCK) * n + kcol_e // SLC_BLOCK).astype(BF16)
            mask = (_dot(sel, expand) > 0.5) & (n * tq + kcol <= tpos)
            return attend(carry, pl.multiple_of(n * tq, tq), jnp.where(mask, 0.0, NEG_INF), ks_ext, vs_ext)

        def win_body(n, carry):
            dist = tpos - (n * tq + kcol)
            bias = jnp.where((dist >= 0) & (dist < WINDOW), 0.0, NEG_INF)
            return attend(carry, pl.multiple_of(n * tq, tq), bias, kw_ext, vw_ext)

        rows = NSA_GROUP * tq
        init = (jnp.full((rows, 1), NEG_INF, F32), jnp.zeros((rows, LANE), F32))
        _, acc_s = lax.fori_loop(0, qt + 1, sel_body, init)
        _, acc_w = lax.fori_loop(jnp.maximum(qt - (WINDOW // tq), 0), qt + 1, win_body, init)
        o_sel = (acc_s / pltpu.roll(acc_s, hd, 1))[:, g * hd:(g + 1) * hd]
        o_win = (acc_w / pltpu.roll(acc_w, hd, 1))[:, g * hd:(g + 1) * hd]
        for j in range(NSA_GROUP):
            h = g * NSA_GROUP + j
            rs = slice(j * tq, (j + 1) * tq)
            o = (gates[:, h:h + 1] * o_cmp[rs] + gates[:, N_HEADS + h:N_HEADS + h + 1] * o_sel[rs]
                 + gates[:, 2 * N_HEADS + h:2 * N_HEADS + h + 1] * o_win[rs])
            o_ref[:, h * hd:(h + 1) * hd] = o.astype(o_ref.dtype)


def _nsa_prompt(cols, cmp_w, b, t):
    w1r, w1, b1, w2, pe = cmp_w
    tq = 256
    nq = t // tq
    n_chunk = t // CMP_STRIDE
    qoff = OFF_NSAQ // 256
    kvoff = OFF_NSAKV // 256
    full = lambda a: pl.BlockSpec(a.shape, lambda i, q: (0,) * a.ndim)
    return pl.pallas_call(
        functools.partial(_nsa_prompt_body, t=t),
        grid=(b, nq),
        in_specs=[pl.BlockSpec((tq, 256), lambda i, q: (i * nq + q, qoff)),
                  pl.BlockSpec((tq, 256), lambda i, q: (i * nq + q, qoff + 1)),
                  pl.BlockSpec((t, LANE), lambda i, q: (i, 2 * kvoff)),
                  pl.BlockSpec((t, LANE), lambda i, q: (i, 2 * kvoff + 1)),
                  pl.BlockSpec((t, 256), lambda i, q: (i, kvoff + 1)),
                  pl.BlockSpec((t, 256), lambda i, q: (i, kvoff + 2)),
                  pl.BlockSpec((tq, LANE), lambda i, q: (i * nq + q, OFF_GATE // LANE)),
                  full(w1r), full(w1), full(b1), full(w2), full(pe)],
        out_specs=pl.BlockSpec((tq, GROUP_WIDTH), lambda i, q: (i * nq + q, 0)),
        out_shape=jax.ShapeDtypeStruct((b * t, GROUP_WIDTH), BF16),
        scratch_shapes=[pltpu.VMEM((NSA_KV_HEADS, n_chunk, HEAD_DIM), F32),
                        pltpu.VMEM((NSA_KV_HEADS, n_chunk, HEAD_DIM), F32)]
                       + [pltpu.VMEM((NSA_KV_HEADS, t, LANE), BF16)] * 4,
        compiler_params=_cparams(("parallel", "arbitrary")),
        name="nsa_prompt",
    )(cols, cols, cols, cols, cols, cols, cols, w1r, w1, b1, w2, pe)


def _rope(x, cos, sin):
    half = HEAD_DIM // 2
    lane = lax.broadcasted_iota(I32, x.shape, 1) % HEAD_DIM
    nxt = pltpu.roll(x, x.shape[1] - half, 1)
    prv = pltpu.roll(x, half, 1)
    return x * cos + jnp.where(lane < half, -nxt, prv) * sin


def _head_norm(y, g, b, eps):
    avg = _head_blockdiag(y.shape[1], 1.0 / HEAD_DIM)
    mu = _dot(y, avg, HI)
    d = y - mu
    var = _dot(d * d, avg, HI)
    return d * lax.rsqrt(var + eps) * g + b


def _ret_prompt_body(q_ref, k_ref, v_ref, gate_ref, cos_ref, sin_ref, dmask_ref, xi_ref, zeta_ref, cd_ref,
                     gn_g_ref, gn_b_ref, s0_ref, o_ref, s_ref, o_scr):
    @pl.when(pl.program_id(1) == 0)
    def _():
        s_ref[...] = s0_ref[...]

    cos = cos_ref[...]
    sin = sin_ref[...]
    q = _rope(q_ref[...], cos, sin)
    k = _rope(k_ref[...], cos, sin) * ATTN_SCALE
    kz = (k * zeta_ref[...]).astype(BF16)
    qb = q.astype(BF16)
    kb = k.astype(BF16)
    for h in range(N_HEADS):
        sl = slice(h * HEAD_DIM, (h + 1) * HEAD_DIM)
        vb = v_ref[:, sl].astype(BF16)
        s = s_ref[0, h]
        att = _dot_nt(qb[:, sl], kb[:, sl]) * dmask_ref[h]
        o_scr[:, sl] = _dot(att.astype(BF16), vb) + _dot(qb[:, sl], s.astype(BF16)) * xi_ref[:, sl]
        s_ref[0, h] = s * cd_ref[:, sl] + _dot_tn(kz[:, sl], vb)
    gate = gate_ref[...]
    y = _head_norm(o_scr[...], gn_g_ref[...], gn_b_ref[...], GN_EPS)
    o_ref[...] = (gate * _sigmoid(gate) * y).astype(o_ref.dtype)


def _ret_tables(log_gamma, c):
    idx = jnp.arange(c, dtype=F32)
    diff = idx[:, None] - idx[None, :]
    dmask = jnp.where(diff >= 0, jnp.exp(jnp.maximum(diff, 0.0)[None] * log_gamma[:, None, None]), 0.0)
    rep = lambda z: jnp.repeat(z, HEAD_DIM, axis=-1)
    xi = rep(jnp.exp((idx + 1.0)[:, None] * log_gamma[None, :]))
    zeta = rep(jnp.exp((c - 1.0 - idx)[:, None] * log_gamma[None, :]))
    cd = rep(jnp.exp(c * log_gamma)[None, :])
    return dmask, xi, zeta, cd


def _rope_tables(pos):
    half = HEAD_DIM // 2
    inv = ROPE_BASE ** (-jnp.arange(half, dtype=F32) / half)
    ang = pos.astype(F32)[:, None] * inv[None, :]
    tile = lambda z: jnp.tile(z, (1, 2 * N_HEADS))
    return tile(jnp.cos(ang)), tile(jnp.sin(ang))


def _ret_prompt(cols, s0, tables, rope, gn_g, gn_b, b, t):
    c = RET_CHUNK
    nc = t // c
    gw = GROUP_WIDTH
    off = OFF_RET // gw
    dmask, xi, zeta, cd = tables
    cos, sin = rope
    col_spec = lambda j: pl.BlockSpec((c, gw), lambda i, n: (i * nc + n, off + j))
    full = lambda a: pl.BlockSpec(a.shape, lambda i, n: (0,) * a.ndim)
    st_spec = pl.BlockSpec((1, N_HEADS, HEAD_DIM, HEAD_DIM), lambda i, n: (i, 0, 0, 0))
    return pl.pallas_call(
        _ret_prompt_body,
        grid=(b, nc),
        in_specs=[col_spec(0), col_spec(1), col_spec(2), col_spec(3),
                  pl.BlockSpec((c, gw), lambda i, n: (n, 0)), pl.BlockSpec((c, gw), lambda i, n: (n, 0)),
                  full(dmask), full(xi), full(zeta), full(cd), full(gn_g), full(gn_b), st_spec],
        out_specs=[pl.BlockSpec((c, gw), lambda i, n: (i * nc + n, 0)), st_spec],
        out_shape=[jax.ShapeDtypeStruct((b * t, gw), BF16),
                   jax.ShapeDtypeStruct((b, N_HEADS, HEAD_DIM, HEAD_DIM), F32)],
        scratch_shapes=[pltpu.VMEM((c, gw), F32)],
        compiler_params=_cparams(("parallel", "arbitrary")),
        name="ret_prompt",
    )(cols, cols, cols, cols, cos, sin, dmask, xi, zeta, cd, gn_g, gn_b, s0)


def _softplus(x):
    return jnp.maximum(x, 0.0) + jnp.log(1.0 + jnp.exp(-jnp.abs(x)))


def _rwkv_prep(c, prev, p):
    mu, w0, w2, a0, a2, g2, k_k, k_a = p
    gw = GROUP_WIDTH
    mixed = c + (prev - c) * mu
    r, k, v = mixed[:, :gw], mixed[:, gw:2 * gw], mixed[:, 2 * gw:3 * gw]
    o1 = 3 * gw
    o2 = o1 + RWKV_DECAY_RANK
    o3 = o2 + RWKV_AAA_RANK
    xw, xa, xg = mixed[:, o1:o2], mixed[:, o2:o3], mixed[:, o3:]
    w_log = -_softplus(-(w0 + _dot(jnp.tanh(xw), w2, HI))) - 0.5
    decay = jnp.exp(-jnp.exp(w_log))
    a = _sigmoid(a0 + _dot(xa, a2, HI))
    gate = _dot(_sigmoid(xg).astype(BF16), g2.astype(BF16))
    kk = k * k_k
    norm = jnp.sqrt(_dot(kk * kk, _head_blockdiag(gw), HI))
    kk = kk / jnp.maximum(norm, 1e-12)
    k = k * (1.0 + (a - 1.0) * k_a)
    return r, decay, k, v, kk, kk * a, gate


def _rwkv_steps(vecs, get_state, put_state, n_steps):
    r8, w8, k8, v8, kk8, ka8 = vecs
    lane = lax.broadcasted_iota(I32, (HEAD_DIM, LANE), 1)
    pad = jnp.zeros((LANE - 8, LANE), F32)
    ys = []
    for hp in range(N_HEADS // 2):
        vt = jnp.concatenate([v8[:, hp * LANE:(hp + 1) * LANE], pad], axis=0).T
        yts = []
        for h2 in range(2):
            h = 2 * hp + h2
            sl = slice(h * HEAD_DIM, (h + 1) * HEAD_DIM)
            yt = jnp.zeros((HEAD_DIM, LANE), F32)
            s = get_state(h, 0)
            for j in range(8):
                if n_steps == 1:
                    s = get_state(h, j)
                vcol = vt[h2 * HEAD_DIM:(h2 + 1) * HEAD_DIM, j:j + 1]
                sa = jnp.sum(s * kk8[j:j + 1, sl], axis=-1, keepdims=True)
                s = s * w8[j:j + 1, sl] - sa * ka8[j:j + 1, sl] + vcol * k8[j:j + 1, sl]
                ycol = jnp.sum(s * r8[j:j + 1, sl], axis=-1, keepdims=True)
                yt = jnp.where(lane == j, ycol, yt)
                if n_steps == 1:
                    put_state(h, j, s)
            if n_steps != 1:
                put_state(h, 0, s)
            yts.append(yt)
        ys.append(jnp.concatenate(yts, axis=0).T[0:8, :])
    return jnp.concatenate(ys, axis=1)


def _rwkv_post(ys, r, k, v, gate, r_k, ln_g, ln_b):
    y = _head_norm(ys, ln_g, ln_b, RWKV_LN_EPS)
    y = y + _dot(r * k * r_k, _head_blockdiag(GROUP_WIDTH), HI) * v
    return y * gate


def _rwkv_prep_body(c_ref, shift_ref, mu_ref, w0_ref, w2_ref, a0_ref, a2_ref, g2_ref, kk_ref, ka_ref, rk_ref,
                    w_o, kk_o, kka_o, k_o, r_o, v_o, bonus_o, gate_o, carry):
    tc = c_ref.shape[0]

    @pl.when(pl.program_id(1) == 0)
    def _():
        carry[...] = shift_ref[...]

    c = c_ref[...]
    row = lax.broadcasted_iota(I32, c.shape, 0)
    prev = jnp.where(row == 0, carry[...], pltpu.roll(c, 1, 0))
    carry[...] = c[tc - 1:tc, :]
    p = (mu_ref[...], w0_ref[...], w2_ref[...], a0_ref[...], a2_ref[...], g2_ref[...], kk_ref[...], ka_ref[...])
    r, w, k, v, kk, kka, gate = _rwkv_prep(c, prev, p)
    w_o[...] = w
    kk_o[...] = kk
    kka_o[...] = kka
    k_o[...] = k
    r_o[...] = r
    v_o[...] = v
    bonus_o[...] = _dot(r * k * rk_ref[...], _head_blockdiag(GROUP_WIDTH), HI) * v
    gate_o[...] = gate


def _rwkv_scan_body(w_ref, kk_ref, kka_ref, k_ref, r_ref, v_ref, s0_ref, y_ref, s_ref):
    @pl.when(pl.program_id(0) == 0)
    def _():
        s_ref[...] = s0_ref[...]

    n_vq = s_ref.shape[0]

    def step(t, _):
        vrows = v_ref[t]
        for vq in range(n_vq):
            s = s_ref[vq]
            sa = jnp.sum(s * kk_ref[t], axis=0, keepdims=True)
            s = s * w_ref[t] - sa * kka_ref[t] + vrows[vq:vq + 1, :] * k_ref[t]
            s_ref[vq] = s
            y_ref[t, vq:vq + 1, :] = jnp.sum(s * r_ref[t], axis=0, keepdims=True)
        return 0

    lax.fori_loop(0, v_ref.shape[0], step, 0)


def _rwkv_post_body(y_ref, bonus_ref, gate_ref, lng_ref, lnb_ref, o_ref):
    y = _head_norm(y_ref[...], lng_ref[...], lnb_ref[...], RWKV_LN_EPS)
    o_ref[...] = ((y + bonus_ref[...]) * gate_ref[...]).astype(o_ref.dtype)


def _rwkv_prompt(cols, shift_prev, s0, params, b, t, tc=256, tscan=64):
    mu, w0, w2, a0, a2, g2, k_k, k_a, r_k, ln_g, ln_b = params
    nt = t // tc
    gw = GROUP_WIDTH
    hd = HEAD_DIM
    chains = b * N_HEADS
    rep = LANE // chains
    assert rep * chains == LANE and hd % rep == 0
    n_vq = hd // rep
    full = lambda a: pl.BlockSpec(a.shape, lambda i, n: (0,) * a.ndim)
    tok = pl.BlockSpec((tc, gw), lambda i, n: (i * nt + n, 0))
    vec = jax.ShapeDtypeStruct((b * t, gw), F32)
    prep_in = (mu, w0, w2, a0, a2, g2, k_k, k_a, r_k)
    w, kk, kka, k, r, v, bonus, gate = pl.pallas_call(
        _rwkv_prep_body,
        grid=(b, nt),
        in_specs=[pl.BlockSpec((tc, RWKV_COLS), lambda i, n: (i * nt + n, OFF_RWKV // RWKV_COLS)),
                  pl.BlockSpec((None, 1, RWKV_COLS), lambda i, n: (i, 0, 0))] + [full(a) for a in prep_in],
        out_specs=[tok] * 8,
        out_shape=[vec] * 8,
        scratch_shapes=[pltpu.VMEM((1, RWKV_COLS), F32)],
        compiler_params=_cparams(("parallel", "arbitrary")),
        name="rwkv_prep",
    )(cols, shift_prev.reshape(b, 1, RWKV_COLS), *prep_in)

    def key_tiles(x):
        x = jnp.transpose(x.reshape(b, t, N_HEADS, hd), (1, 3, 0, 2)).reshape(t, hd, chains)
        return jnp.tile(x, (1, 1, rep))
    xts = [key_tiles(z) for z in (w, kk, kka, k, r)]
    vt = jnp.transpose(v.reshape(b, t, N_HEADS, rep, n_vq), (1, 4, 3, 0, 2)).reshape(t, n_vq, LANE)
    s0t = jnp.transpose(s0.reshape(b, N_HEADS, rep, n_vq, hd), (3, 4, 2, 0, 1)).reshape(n_vq, hd, LANE)

    ns = t // tscan
    st_spec = pl.BlockSpec((n_vq, hd, LANE), lambda n: (0, 0, 0))
    yt, st = pl.pallas_call(
        _rwkv_scan_body,
        grid=(ns,),
        in_specs=[pl.BlockSpec((tscan, hd, LANE), lambda n: (n, 0, 0))] * 5
                 + [pl.BlockSpec((tscan, n_vq, LANE), lambda n: (n, 0, 0)), st_spec],
        out_specs=[pl.BlockSpec((tscan, n_vq, LANE), lambda n: (n, 0, 0)), st_spec],
        out_shape=[jax.ShapeDtypeStruct((t, n_vq, LANE), F32), jax.ShapeDtypeStruct((n_vq, hd, LANE), F32)],
        compiler_params=_cparams(("arbitrary",)),
        name="rwkv_scan",
    )(*xts, vt, s0t)
    ys = jnp.transpose(yt.reshape(t, n_vq, rep, b, N_HEADS), (3, 0, 4, 2, 1)).reshape(b * t, gw)
    s_fin = jnp.transpose(st.reshape(n_vq, hd, rep, b, N_HEADS), (3, 4, 2, 0, 1)).reshape(b, N_HEADS, hd, hd)

    tm = min(512, b * t)
    tokm = pl.BlockSpec((tm, gw), lambda i: (i, 0))
    o = pl.pallas_call(
        _rwkv_post_body,
        grid=(b * t // tm,),
        in_specs=[tokm, tokm, tokm, pl.BlockSpec((1, gw), lambda i: (0, 0)), pl.BlockSpec((1, gw), lambda i: (0, 0))],
        out_specs=tokm,
        out_shape=jax.ShapeDtypeStruct((b * t, gw), BF16),
        compiler_params=_cparams(("parallel",)),
        name="rwkv_post",
    )(ys, bonus, gate, ln_g, ln_b)
    return o, s_fin


PAGES_PER_STEP = 4


def _past_stats_body(pt_ref, *refs, n_pages):
    pps = PAGES_PER_STEP
    kt_refs, kct_refs, vct_refs = refs[:pps], refs[pps:2 * pps], refs[2 * pps:3 * pps]
    (q_ref, w1r_ref, w1_ref, b1_ref, w2_ref, pe_ref, top_ref, cmp_ref, qb, gsum, kc_rows, vc_rows) = refs[3 * pps:]
    step = pl.program_id(1)
    pages_per_blk = MOBA_BLOCK // PAGE_SIZE
    nb = n_pages // pages_per_blk
    n_chunk = n_pages * PAGE_SIZE // CMP_STRIDE

    @pl.when(step == 0)
    def _():
        gsum[...] = jnp.zeros_like(gsum)
        qb[...] = jnp.broadcast_to(q_ref[...], qb.shape)

    for i in range(pps):
        p = step * pps + i
        gsum[p // pages_per_blk] += jnp.sum(kt_refs[i][...] * qb[...], axis=1)
        row0 = pl.multiple_of(p * PAGE_SIZE, PAGE_SIZE)
        kc_rows[pl.ds(row0, PAGE_SIZE), :] = kct_refs[i][...].reshape(LANE, PAGE_SIZE).T
        vc_rows[pl.ds(row0, PAGE_SIZE), :] = vct_refs[i][...].reshape(LANE, PAGE_SIZE).T

    @pl.when(step == n_pages // pps - 1)
    def _():
        gate = jnp.sum(gsum[...], axis=-1) * (1.0 / MOBA_BLOCK)
        ids = lax.broadcasted_iota(I32, (nb, N_HEADS), 0)
        rows8 = lax.broadcasted_iota(I32, (8, N_HEADS), 0)
        top = jnp.zeros((8, N_HEADS), I32)
        for j in range(MOBA_TOPK):
            best = jnp.max(gate, axis=0, keepdims=True)
            arg = jnp.min(jnp.where(gate == best, ids, nb), axis=0, keepdims=True)
            top = jnp.where(rows8 == j, arg, top)
            gate = jnp.where(ids == arg, -jnp.inf, gate)
        top_ref[...] = top
        for kv, src in enumerate((kc_rows, vc_rows)):
            bias = _cmp_bias(pe_ref, w1_ref, b1_ref, kv)
            out = _compress(lambda r: src[pl.ds(r, n_chunk, stride=CMP_STRIDE), :], n_chunk, w1r_ref, kv, bias,
                            w2_ref[kv])
            for g in range(NSA_KV_HEADS):
                cmp_ref[kv * NSA_KV_HEADS + g] = out[g]


def _past_stats(page_table, moba_t, nsa_t, q_col, cmp_w, layer):
    w1r, w1, b1, w2, pe = cmp_w
    bs, n_pages = page_table.shape
    nb = n_pages * PAGE_SIZE // MOBA_BLOCK
    rows = n_pages * PAGE_SIZE
    n_chunk = rows // CMP_STRIDE
    pps = PAGES_PER_STEP
    assert n_pages % pps == 0
    full = lambda a: pl.BlockSpec(a.shape, lambda i, p, pt: (0,) * a.ndim)
    k_page = lambda j: pl.BlockSpec((None, None, None, N_HEADS, HEAD_DIM, PAGE_SIZE),
                                    lambda i, p, pt: (layer, pt[i, p * pps + j], 0, 0, 0, 0))
    nsa_page = lambda t, j: pl.BlockSpec((None, None, None, NSA_KV_HEADS, HEAD_DIM, PAGE_SIZE),
                                         lambda i, p, pt: (layer, pt[i, p * pps + j], t, 0, 0, 0))
    return pl.pallas_call(
        functools.partial(_past_stats_body, n_pages=n_pages),
        grid_spec=pltpu.PrefetchScalarGridSpec(
            num_scalar_prefetch=1,
            grid=(bs, n_pages // pps),
            in_specs=[k_page(j) for j in range(pps)] + [nsa_page(0, j) for j in range(pps)]
                     + [nsa_page(1, j) for j in range(pps)]
                     + [pl.BlockSpec((None, N_HEADS, HEAD_DIM, 1), lambda i, p, pt: (i, 0, 0, 0)),
                        full(w1r), full(w1), full(b1), full(w2), full(pe)],
            out_specs=[pl.BlockSpec((None, 8, N_HEADS), lambda i, p, pt: (i, 0, 0)),
                       pl.BlockSpec((None, 2 * NSA_KV_HEADS, n_chunk, HEAD_DIM), lambda i, p, pt: (i, 0, 0, 0))],
            scratch_shapes=[pltpu.VMEM((N_HEADS, HEAD_DIM, PAGE_SIZE), F32), pltpu.VMEM((nb, N_HEADS, PAGE_SIZE), F32),
                            pltpu.VMEM((rows, LANE), F32), pltpu.VMEM((rows, LANE), F32)]),
        out_shape=[jax.ShapeDtypeStruct((bs, 8, N_HEADS), I32),
                   jax.ShapeDtypeStruct((bs, 2 * NSA_KV_HEADS, n_chunk, HEAD_DIM), F32)],
        compiler_params=_cparams(("parallel", "arbitrary")),
        name="past_stats",
    )(page_table, *([moba_t] * pps), *([nsa_t] * (2 * pps)), q_col, w1r, w1, b1, w2, pe)


def _moba_sample_body(pt_ref, top_ref, *refs):
    n_pg = (len(refs) - 4) // 2
    kt_refs, vt_refs = refs[:n_pg], refs[n_pg:2 * n_pg]
    q_ref, kn_ref, vn_ref, o_ref = refs[2 * n_pg:]
    q = q_ref[...] * ATTN_SCALE
    q8 = jnp.broadcast_to(q, (8, HEAD_DIM)).astype(BF16)
    s_all = [_dot(q8, kt[...].astype(BF16)) for kt in kt_refs]
    s_self = jnp.sum(q * kn_ref[...], axis=-1, keepdims=True)
    m = s_self
    for s in s_all:
        m = jnp.maximum(m, jnp.max(s, axis=-1, keepdims=True))
    e_self = jnp.exp(s_self - m)
    den = e_self
    acc = e_self * vn_ref[...]
    for s, vt in zip(s_all, vt_refs):
        pr = jnp.exp(s - m)
        den = den + jnp.sum(pr, axis=-1, keepdims=True)
        acc = acc + _dot_nt(pr.astype(BF16), vt[...].astype(BF16))
    o_ref[...] = (acc / den)[0:1, :]


def _moba_sample(page_table, top, moba_t, q, k_new, v_new, layer):
    bs = page_table.shape[0]
    ppb = MOBA_BLOCK // PAGE_SIZE

    def page_spec(kv, j, r):
        return pl.BlockSpec((None, None, None, None, HEAD_DIM, PAGE_SIZE),
                            lambda i, h, pt, tp: (layer, pt[i, tp[i, j, h] * ppb + r], kv, h, 0, 0))

    pages = [(j, r) for j in range(MOBA_TOPK) for r in range(ppb)]
    head = pl.BlockSpec((None, None, 1, HEAD_DIM), lambda i, h, pt, tp: (i, h, 0, 0))
    return pl.pallas_call(
        _moba_sample_body,
        grid_spec=pltpu.PrefetchScalarGridSpec(
            num_scalar_prefetch=2,
            grid=(bs, N_HEADS),
            in_specs=[page_spec(0, j, r) for j, r in pages] + [page_spec(1, j, r) for j, r in pages]
                     + [head, head, head],
            out_specs=head),
        out_shape=jax.ShapeDtypeStruct((bs, N_HEADS, 1, HEAD_DIM), F32),
        compiler_params=_cparams(("parallel", "parallel")),
        name="moba_sample",
    )(page_table, top, *([moba_t] * (2 * len(pages))), q, k_new, v_new)


def _heads_to_rows(row, g):
    parts = [row[:, (g * NSA_GROUP + j) * HEAD_DIM:(g * NSA_GROUP + j + 1) * HEAD_DIM] for j in range(NSA_GROUP)]
    return jnp.concatenate(parts + [jnp.zeros((8 - NSA_GROUP, HEAD_DIM), F32)], axis=0)


def _nsa_sample_sel_body(q_ref, cmp_ref, win_ref, kvn_ref, gate_ref, idx_ref, part_ref, *, past):
    hd = HEAD_DIM
    n_chunk = past // CMP_STRIDE
    n_slc = past // SLC_BLOCK + 1
    n_slc_pad = -(-n_slc // LANE) * LANE
    cur = past // SLC_BLOCK
    qrow = q_ref[...]
    gates = _sigmoid(gate_ref[...])
    kvn = kvn_ref[...]
    rows8 = lax.broadcasted_iota(I32, (8, 1), 0)
    ids = lax.broadcasted_iota(I32, (1, n_slc_pad), 1)
    m_iota = lax.broadcasted_iota(I32, (n_slc_pad, n_slc_pad), 0)
    j_iota = lax.broadcasted_iota(I32, (n_slc_pad, n_slc_pad), 1)
    lane = lax.broadcasted_iota(I32, (1, LANE), 1)
    idx_out = jnp.zeros((8, LANE), I32)
    for g in range(NSA_KV_HEADS):
        q8 = _heads_to_rows(qrow, g)
        q8b = q8.astype(BF16)
        n_ids = lax.broadcasted_iota(I32, (8, n_chunk), 1)
        l_cmp = _dot_nt(q8b, cmp_ref[g].astype(BF16)) * ATTN_SCALE
        p_cmp = _softmax_rows(l_cmp, n_ids * CMP_STRIDE + (CMP_LEN - 1) <= past)
        o_cmp = _dot(p_cmp.astype(BF16), cmp_ref[NSA_KV_HEADS + g].astype(BF16))
        imp = jnp.sum(jnp.where(rows8 < NSA_GROUP, p_cmp, 0.0), axis=0, keepdims=True)
        imp_slc = _dot(jnp.broadcast_to(imp, (8, n_chunk)), _slc_matrix(n_chunk, n_chunk - 1, n_slc_pad), HI)
        eligible = (ids <= cur) & (ids < n_slc)
        forced = (ids == 0) | (ids == cur) | (ids == cur - 1)
        score = jnp.where(eligible, jnp.where(forced, jnp.inf, imp_slc[0:1, :]), -jnp.inf)
        s_col = jnp.broadcast_to(score, (LANE, n_slc_pad)).T[:, 0:1]
        beats = (s_col > score) | ((s_col == score) & (m_iota < j_iota))
        rank = jnp.sum(beats.astype(F32), axis=0, keepdims=True)
        sel = jnp.where(eligible & (rank < SLC_TOPK), 1.0, 0.0)
        sel_col = jnp.broadcast_to(sel, (LANE, n_slc_pad)).T[:, 0:1]
        before = jnp.sum(jnp.where(m_iota < j_iota, sel_col, 0.0), axis=0, keepdims=True)
        idx_row = jnp.zeros((1, LANE), I32)
        for i in range(SLC_TOPK):
            hit = (sel > 0.5) & (before == i)
            idx_i = jnp.sum(jnp.where(hit, ids, 0), axis=-1, keepdims=True)
            idx_row = jnp.where(lane == i, idx_i, idx_row)
        idx_out = jnp.where(lax.broadcasted_iota(I32, (8, LANE), 0) == g, idx_row, idx_out)
        kw = win_ref[0, g].astype(BF16)
        vw = win_ref[1, g].astype(BF16)
        n_buf = win_ref.shape[-1]
        w_ids = lax.broadcasted_iota(I32, (8, n_buf), 1)
        l_win = _dot(q8b, kw) * ATTN_SCALE
        w_mask = w_ids >= n_buf - (WINDOW - 1)
        kw_new = kvn[:, 4 * NSA_KV_WIDTH + g * hd:4 * NSA_KV_WIDTH + (g + 1) * hd]
        vw_new = kvn[:, 5 * NSA_KV_WIDTH + g * hd:5 * NSA_KV_WIDTH + (g + 1) * hd]
        s_self = jnp.sum(q8 * kw_new, axis=-1, keepdims=True) * ATTN_SCALE
        m = jnp.maximum(jnp.max(jnp.where(w_mask, l_win, NEG_INF), axis=-1, keepdims=True), s_self)
        e = jnp.where(w_mask, jnp.exp(l_win - m), 0.0)
        e_self = jnp.exp(s_self - m)
        o_win = (_dot_nt(e.astype(BF16), vw) + e_self * vw_new) / (jnp.sum(e, axis=-1, keepdims=True) + e_self)
        for j in range(NSA_GROUP):
            h = g * NSA_GROUP + j
            part_ref[:, h * hd:(h + 1) * hd] = (gates[:, h:h + 1] * o_cmp[j:j + 1]
                                                + gates[:, 2 * N_HEADS + h:2 * N_HEADS + h + 1] * o_win[j:j + 1])
    idx_ref[...] = idx_out


def _nsa_sample_sel(q, cmp_tok, win_state, kv_new, gates, layer, past):
    bs = q.shape[0]
    n_buf = win_state.shape[-1]
    n_chunk = cmp_tok.shape[2]
    return pl.pallas_call(
        functools.partial(_nsa_sample_sel_body, past=past),
        grid=(bs,),
        in_specs=[pl.BlockSpec((None, 1, GROUP_WIDTH), lambda i: (i, 0, 0)),
                  pl.BlockSpec((None, 2 * NSA_KV_HEADS, n_chunk, HEAD_DIM), lambda i: (i, 0, 0, 0)),
                  pl.BlockSpec((None, None, 2, NSA_KV_HEADS, HEAD_DIM, n_buf), lambda i: (layer, i, 0, 0, 0, 0)),
                  pl.BlockSpec((None, 1, 6 * NSA_KV_WIDTH), lambda i: (i, 0, 0)),
                  pl.BlockSpec((None, 1, LANE), lambda i: (i, 0, 0))],
        out_specs=[pl.BlockSpec((None, 8, LANE), lambda i: (i, 0, 0)),
                   pl.BlockSpec((None, 1, GROUP_WIDTH), lambda i: (i, 0, 0))],
        out_shape=[jax.ShapeDtypeStruct((bs, 8, LANE), I32), jax.ShapeDtypeStruct((bs, 1, GROUP_WIDTH), F32)],
        compiler_params=_cparams(("parallel",)),
        name="nsa_sample_sel",
    )(q, cmp_tok, win_state, kv_new, gates)


NSA_BLOCKS_PER_STEP = 4


def _nsa_sample_attn_body(pt_ref, idx_ref, *refs, n_past_blk):
    bps = NSA_BLOCKS_PER_STEP
    n_pg = NSA_KV_HEADS * bps
    ks_refs, vs_refs = refs[:n_pg], refs[n_pg:2 * n_pg]
    q_ref, kvn_ref, gate_ref, part_ref, o_ref, m_s, l_s, acc_s = refs[2 * n_pg:]
    b, i = pl.program_id(0), pl.program_id(1)
    hd = HEAD_DIM
    qrow = q_ref[...]
    kvn = kvn_ref[...]
    half = lax.broadcasted_iota(I32, (8, PAGE_SIZE), 1) // SLC_BLOCK
    for g in range(NSA_KV_HEADS):
        q8 = _heads_to_rows(qrow, g)
        ks_new = kvn[:, 2 * NSA_KV_WIDTH + g * hd:2 * NSA_KV_WIDTH + (g + 1) * hd]
        vs_new = kvn[:, 3 * NSA_KV_WIDTH + g * hd:3 * NSA_KV_WIDTH + (g + 1) * hd]

        @pl.when(i == 0)
        def _():
            m_s[g] = jnp.broadcast_to(jnp.sum(q8 * ks_new, axis=-1, keepdims=True) * ATTN_SCALE, (8, hd))
            l_s[g] = jnp.ones((8, hd), F32)
            acc_s[g] = jnp.broadcast_to(vs_new, (8, hd))

        q8b = (q8 * ATTN_SCALE).astype(BF16)
        m_i = m_s[g][:, 0:1]
        l_i = l_s[g][:, 0:1]
        acc = acc_s[g]
        for u in range(bps):
            blk = idx_ref[b, g, i * bps + u]
            mask = half == jnp.where(blk < n_past_blk, blk % (PAGE_SIZE // SLC_BLOCK), -1)
            s = jnp.where(mask, _dot(q8b, ks_refs[g * bps + u][...].astype(BF16)), NEG_INF)
            m_new = jnp.maximum(m_i, jnp.max(s, axis=-1, keepdims=True))
            pr = jnp.where(mask, jnp.exp(s - m_new), 0.0)
            alpha = jnp.exp(m_i - m_new)
            l_i = alpha * l_i + jnp.sum(pr, axis=-1, keepdims=True)
            acc = alpha * acc + _dot_nt(pr.astype(BF16), vs_refs[g * bps + u][...].astype(BF16))
            m_i = m_new
        l_s[g] = jnp.broadcast_to(l_i, (8, hd))
        acc_s[g] = acc
        m_s[g] = jnp.broadcast_to(m_i, (8, hd))

    @pl.when(i == pl.num_programs(1) - 1)
    def _():
        gates = _sigmoid(gate_ref[...])
        for g in range(NSA_KV_HEADS):
            o_sel = acc_s[g] / l_s[g]
            for j in range(NSA_GROUP):
                h = g * NSA_GROUP + j
                sl = slice(h * hd, (h + 1) * hd)
                o_ref[:, sl] = (part_ref[:, sl] + gates[:, N_HEADS + h:N_HEADS + h + 1] * o_sel[j:j + 1]
                                ).astype(o_ref.dtype)


def _nsa_sample_attn(page_table, sel_idx, nsa_cache, q, kv_new, gates, part, layer, past):
    bs = page_table.shape[0]
    n_past_blk = past // SLC_BLOCK
    per_page = PAGE_SIZE // SLC_BLOCK

    bps = NSA_BLOCKS_PER_STEP

    def blk_spec(g, u, t):
        def imap(b, i, pt, ix):
            blk = jnp.minimum(ix[b, g, i * bps + u], n_past_blk - 1)
            return (layer, pt[b, blk // per_page], t, g, 0, 0)
        return pl.BlockSpec((None, None, None, None, HEAD_DIM, PAGE_SIZE), imap)

    row = lambda w: pl.BlockSpec((None, 1, w), lambda b, i, pt, ix: (b, 0, 0))
    gu = [(g, u) for g in range(NSA_KV_HEADS) for u in range(bps)]
    return pl.pallas_call(
        functools.partial(_nsa_sample_attn_body, n_past_blk=n_past_blk),
        grid_spec=pltpu.PrefetchScalarGridSpec(
            num_scalar_prefetch=2,
            grid=(bs, SLC_TOPK // bps),
            in_specs=[blk_spec(g, u, 2) for g, u in gu] + [blk_spec(g, u, 3) for g, u in gu]
                     + [row(GROUP_WIDTH), row(6 * NSA_KV_WIDTH), row(LANE), row(GROUP_WIDTH)],
            out_specs=row(GROUP_WIDTH),
            scratch_shapes=[pltpu.VMEM((NSA_KV_HEADS, 8, HEAD_DIM), F32)] * 3),
        out_shape=jax.ShapeDtypeStruct((bs, 1, GROUP_WIDTH), F32),
        compiler_params=_cparams(("parallel", "arbitrary")),
        name="nsa_sample_attn",
    )(page_table, sel_idx, *([nsa_cache] * (2 * len(gu))), q, kv_new, gates, part)


def _recur_sample_body(c_ref, shift_ref, mu_ref, w0_ref, w2_ref, a0_ref, a2_ref, g2_ref, kk_ref, ka_ref, rk_ref,
                       lng_ref, lnb_ref, s_rw_ref, ret_ref, cos_ref, sin_ref, gam_ref, gn_g_ref, gn_b_ref, s_rt_ref,
                       o_rw_ref, s_rw_out, o_rt_ref, s_rt_out, o_scr):
    gw = GROUP_WIDTH
    hd = HEAD_DIM
    p = (mu_ref[...], w0_ref[...], w2_ref[...], a0_ref[...], a2_ref[...], g2_ref[...], kk_ref[...], ka_ref[...])
    r, w, k, v, kk, kka, gate = _rwkv_prep(c_ref[...], shift_ref[...], p)

    def put_state(h, j, s):
        s_rw_out[j, h] = s

    ys = _rwkv_steps((r, w, k, v, kk, kka), lambda h, j: s_rw_ref[j, h], put_state, 1)
    o_rw_ref[...] = _rwkv_post(ys, r, k, v, gate, rk_ref[...], lng_ref[...], lnb_ref[...]).astype(o_rw_ref.dtype)

    cos = cos_ref[...]
    sin = sin_ref[...]
    q = _rope(ret_ref[:, 0:gw], cos, sin)
    kr = _rope(ret_ref[:, gw:2 * gw], cos, sin) * ATTN_SCALE
    vr = ret_ref[:, 2 * gw:3 * gw]
    gam = gam_ref[...]
    qk = _dot(q * kr, _head_blockdiag(gw), HI)
    pad = jnp.zeros((LANE - 8, LANE), F32)
    for hp in range(N_HEADS // 2):
        ps = slice(hp * LANE, (hp + 1) * LANE)
        qt = jnp.concatenate([q[:, ps], pad], axis=0).T
        kt = jnp.concatenate([kr[:, ps], pad], axis=0).T
        for h2 in range(2):
            h = 2 * hp + h2
            sl = slice(h * hd, (h + 1) * hd)
            for j in range(8):
                s = s_rt_ref[j, h]
                qcol = qt[h2 * hd:(h2 + 1) * hd, j:j + 1]
                kcol = kt[h2 * hd:(h2 + 1) * hd, j:j + 1]
                g_h = gam[:, sl]
                o_scr[j:j + 1, sl] = (qk[j:j + 1, sl] * vr[j:j + 1, sl]
                                      + g_h * jnp.sum(qcol * s, axis=0, keepdims=True))
                s_rt_out[j, h] = s * g_h + kcol * vr[j:j + 1, sl]
    gt = ret_ref[:, 3 * gw:4 * gw]
    y = _head_norm(o_scr[...], gn_g_ref[...], gn_b_ref[...], GN_EPS)
    o_rt_ref[...] = (gt * _sigmoid(gt) * y).astype(o_rt_ref.dtype)


def _recur_sample(c_rwkv, shift, rwkv_params, s_rwkv, c_ret, rope, gamma, gn_g, gn_b, s_ret):
    bs = c_rwkv.shape[0]
    cos, sin = rope
    st = jax.ShapeDtypeStruct((bs, N_HEADS, HEAD_DIM, HEAD_DIM), F32)
    ob = jax.ShapeDtypeStruct((bs, GROUP_WIDTH), BF16)
    return pl.pallas_call(
        _recur_sample_body,
        out_shape=[ob, st, ob, st],
        scratch_shapes=[pltpu.VMEM((bs, GROUP_WIDTH), F32)],
        compiler_params=pltpu.CompilerParams(vmem_limit_bytes=VMEM_LIMIT),
        name="recur_sample",
    )(c_rwkv, shift, *rwkv_params, s_rwkv, c_ret, cos, sin, gamma, gn_g, gn_b, s_ret)


def _prep_rwkv_params(mu, w0, w2, a0, a2, g2, k_k, k_a, r_k, ln_g, ln_b):
    row = lambda z: z.reshape(1, -1)
    return (row(mu), row(w0), w2, row(a0), a2, g2, row(k_k), row(k_a), row(r_k), row(ln_g), row(ln_b))


def _prep_cmp_weights(w1, b1, w2, pe):
    span = CMP_LEN // CMP_STRIDE
    w1r = w1.reshape(2, span, CMP_STRIDE, HEAD_DIM, CMP_HIDDEN)
    w1r = jnp.transpose(w1r, (0, 2, 3, 1, 4)).reshape(2, CMP_STRIDE, HEAD_DIM, span * CMP_HIDDEN).astype(BF16)
    return (w1r, w1, b1.reshape(2, 1, CMP_HIDDEN), w2, pe.reshape(2, 1, CMP_LEN * HEAD_DIM))


def _pad_w_in(w):
    o_nsa = MOBA_COLS
    o_gate = o_nsa + GROUP_WIDTH + 6 * NSA_KV_WIDTH
    o_rwkv = o_nsa + NSA_COLS
    o_ret = o_rwkv + RWKV_COLS
    zeros = jnp.zeros((w.shape[0], N_PAD - OFF_GATE - 3 * N_HEADS), w.dtype)
    return jnp.concatenate([w[:, o_rwkv:o_ret], w[:, :o_gate], w[:, o_ret:], w[:, o_gate:o_rwkv], zeros], axis=1)


def kernel(x_prompt, x_sample, cache_moba_kv, cache_nsa_kv, state_nsa_win, state_rwkv, state_rwkv_shift, state_ret,
           page_table, norm_g, w_in, w_out, w_up, w_down, nsa_cmp_pe, nsa_cmp_w1, nsa_cmp_b1, nsa_cmp_w2, rwkv_mu,
           rwkv_w0, rwkv_w2, rwkv_a0, rwkv_a2, rwkv_g2, rwkv_k_k, rwkv_k_a, rwkv_r_k, rwkv_ln_g, rwkv_ln_b, ret_gn_g,
           ret_gn_b):
    bp, t, d = x_prompt.shape
    bs = x_sample.shape[0]
    assert x_sample.shape[1] == 1 and d == D_MODEL
    depth = w_in.shape[0]
    past = page_table.shape[1] * PAGE_SIZE
    gw = GROUP_WIDTH
    hd = HEAD_DIM

    log_gamma = jnp.log(1.0 - jnp.exp2(-5.0 - jnp.arange(N_HEADS, dtype=F32)))
    ret_tables = _ret_tables(log_gamma, RET_CHUNK)
    rope_p = _rope_tables(jnp.arange(t, dtype=I32))
    rope_s = _rope_tables(jnp.full((1,), past, I32))
    gamma_row = jnp.repeat(jnp.exp(log_gamma), hd)[None, :]
    moba_t = jnp.transpose(cache_moba_kv, (0, 1, 3, 4, 5, 2))
    nsa_t = jnp.transpose(cache_nsa_kv, (0, 1, 3, 4, 5, 2))
    win_t = jnp.transpose(state_nsa_win, (0, 1, 3, 4, 5, 2))

    xp = x_prompt.reshape(bp * t, d)
    xs = x_sample.reshape(bs, d)
    zero_state = jnp.zeros((bp, N_HEADS, hd, hd), F32)
    zero_shift = jnp.zeros((bp, RWKV_COLS), F32)
    st_p, st_s = [], []
    for l in range(depth):
        g = norm_g[l].reshape(4, 1, d)
        w_in_l = _pad_w_in(w_in[l]).astype(BF16)
        w_out_l = w_out[l].astype(BF16)
        w_up_l = w_up[l].astype(BF16)
        w_down_l = w_down[l].astype(BF16)
        cmp_w = _prep_cmp_weights(nsa_cmp_w1[l], nsa_cmp_b1[l], nsa_cmp_w2[l], nsa_cmp_pe[l])
        rwkv_p = _prep_rwkv_params(rwkv_mu[l], rwkv_w0[l], rwkv_w2[l], rwkv_a0[l], rwkv_a2[l], rwkv_g2[l],
                                   rwkv_k_k[l], rwkv_k_a[l], rwkv_r_k[l], rwkv_ln_g[l], rwkv_ln_b[l])
        gn_g = ret_gn_g[l].reshape(1, gw)
        gn_b = ret_gn_b[l].reshape(1, gw)

        cols = _inproj(xp, g[0], w_in_l, min(1024, bp * t), 512)
        o_moba = _moba_prompt(cols, bp, t)
        o_nsa = _nsa_prompt(cols, cmp_w, bp, t)
        o_rwkv, rwkv_s = _rwkv_prompt(cols, zero_shift, zero_state, rwkv_p, bp, t)
        o_ret, ret_s = _ret_prompt(cols, zero_state, ret_tables, rope_p, gn_g, gn_b, bp, t)
        xp = _outproj((o_moba, o_nsa, o_rwkv, o_ret), xp, g[1], w_out_l, 256)
        xp = _ffn(xp, g[2], g[3], w_up_l, w_down_l, min(512, bp * t), 1024)
        c3 = cols.reshape(bp, t, N_PAD)
        win_keep = min(WINDOW, t)
        st_p.append((c3[:, :, OFF_MOBA + gw:OFF_MOBA + 3 * gw].reshape(bp, t, 2, N_HEADS, hd),
                     c3[:, :, OFF_NSAKV:OFF_NSAKV + 4 * NSA_KV_WIDTH].reshape(bp, t, 4, NSA_KV_HEADS, hd),
                     c3[:, t - win_keep:, OFF_NSAKV + 4 * NSA_KV_WIDTH:OFF_NSAKV + 6 * NSA_KV_WIDTH
                        ].reshape(bp, win_keep, 2, NSA_KV_HEADS, hd),
                     rwkv_s, c3[:, t - 1, OFF_RWKV:OFF_RWKV + RWKV_COLS], ret_s))

        cs = _inproj(xs, g[0], w_in_l, bs, 512)
        heads = lambda z: z.reshape(bs, N_HEADS, 1, hd)
        q_m = cs[:, OFF_MOBA:OFF_MOBA + gw]
        k_m = cs[:, OFF_MOBA + gw:OFF_MOBA + 2 * gw]
        v_m = cs[:, OFF_MOBA + 2 * gw:OFF_MOBA + 3 * gw]
        top, cmp_tok = _past_stats(page_table, moba_t, nsa_t, q_m.reshape(bs, N_HEADS, hd, 1), cmp_w, l)
        o_moba_s = _moba_sample(page_table, top[:, :MOBA_TOPK, :], moba_t, heads(q_m), heads(k_m), heads(v_m), l)
        q_n = cs[:, OFF_NSAQ:OFF_NSAQ + gw].reshape(bs, 1, gw)
        kv_new = cs[:, OFF_NSAKV:OFF_NSAKV + 6 * NSA_KV_WIDTH].reshape(bs, 1, 6 * NSA_KV_WIDTH)
        gates = cs[:, OFF_GATE:OFF_GATE + LANE].reshape(bs, 1, LANE)
        sel_idx, part = _nsa_sample_sel(q_n, cmp_tok, win_t, kv_new, gates, l, past)
        o_nsa_s = _nsa_sample_attn(page_table, sel_idx[:, :NSA_KV_HEADS, :SLC_TOPK], nsa_t, q_n, kv_new, gates, part,
                                   l, past)
        o_rwkv_s, rwkv_s_s, o_ret_s, ret_s_s = _recur_sample(
            cs[:, OFF_RWKV:OFF_RWKV + RWKV_COLS], state_rwkv_shift[l], rwkv_p, state_rwkv[l],
            cs[:, OFF_RET:OFF_RET + RET_COLS], rope_s, gamma_row, gn_g, gn_b, state_ret[l])
        parts_s = (o_moba_s.reshape(bs, gw).astype(BF16), o_nsa_s.reshape(bs, gw).astype(BF16), o_rwkv_s, o_ret_s)
        xs = _outproj(parts_s, xs, g[1], w_out_l, bs)
        xs = _ffn(xs, g[2], g[3], w_up_l, w_down_l, bs, 512)
        win_new = cs[:, OFF_NSAKV + 4 * NSA_KV_WIDTH:OFF_NSAKV + 6 * NSA_KV_WIDTH].reshape(bs, 1, 2, NSA_KV_HEADS, hd)
        win_all = jnp.concatenate([state_nsa_win[l], win_new], axis=1)
        keep_s = min(WINDOW, win_all.shape[1])
        st_s.append((cs[:, OFF_MOBA + gw:OFF_MOBA + 3 * gw].reshape(bs, 1, 2, N_HEADS, hd),
                     cs[:, OFF_NSAKV:OFF_NSAKV + 4 * NSA_KV_WIDTH].reshape(bs, 1, 4, NSA_KV_HEADS, hd),
                     win_all[:, win_all.shape[1] - keep_s:],
                     rwkv_s_s, cs[:, OFF_RWKV:OFF_RWKV + RWKV_COLS], ret_s_s))

    stk = lambda sts, i: jnp.stack([s[i] for s in sts], axis=0)
    outs = [xp.reshape(bp, t, d), xs.reshape(bs, 1, d)]
    for i in range(6):
        outs += [stk(st_p, i), stk(st_s, i)]
    return tuple(outs)
```

```python
import functools

import jax
import jax.numpy as jnp
import numpy as np
from jax import lax
from jax.experimental import pallas as pl
from jax.experimental.pallas import tpu as pltpu

F32 = jnp.float32
BF16 = jnp.bfloat16
I32 = jnp.int32
HI = lax.Precision.HIGHEST

D_MODEL = 2048
HEAD_DIM = 64
GROUP_WIDTH = D_MODEL // 4
N_HEADS = GROUP_WIDTH // HEAD_DIM
D_FF = 4 * D_MODEL
RMS_EPS = 1e-6
GN_EPS = 1e-5
NEG_INF = -1e30
ATTN_SCALE = HEAD_DIM ** -0.5
PAGE_SIZE = 128
MOBA_BLOCK = 256
MOBA_TOPK = 3
NSA_KV_HEADS = 2
NSA_GROUP = N_HEADS // NSA_KV_HEADS
NSA_KV_WIDTH = NSA_KV_HEADS * HEAD_DIM
CMP_LEN = 32
CMP_STRIDE = 16
CMP_HIDDEN = 2 * HEAD_DIM
SLC_BLOCK = 64
SLC_TOPK = 16
WINDOW = 512
RWKV_DECAY_RANK = 64
RWKV_AAA_RANK = 64
RWKV_GATE_RANK = 128
RWKV_LN_EPS = 64e-5
RET_CHUNK = 128
ROPE_BASE = 10000.0

MOBA_COLS = 3 * GROUP_WIDTH
NSA_COLS = GROUP_WIDTH + 6 * NSA_KV_WIDTH + 3 * N_HEADS
RWKV_COLS = 3 * GROUP_WIDTH + RWKV_DECAY_RANK + RWKV_AAA_RANK + RWKV_GATE_RANK
RET_COLS = 4 * GROUP_WIDTH
IN_COLS = MOBA_COLS + NSA_COLS + RWKV_COLS + RET_COLS

OFF_RWKV = 0
OFF_MOBA = OFF_RWKV + RWKV_COLS
OFF_NSAQ = OFF_MOBA + MOBA_COLS
OFF_NSAKV = OFF_NSAQ + GROUP_WIDTH
OFF_RET = OFF_NSAKV + 6 * NSA_KV_WIDTH
OFF_GATE = OFF_RET + RET_COLS
N_PAD = OFF_GATE + 512

LANE = 128
VMEM_LIMIT = 56 * 1024 * 1024


def _cparams(sem):
    return pltpu.CompilerParams(dimension_semantics=sem, vmem_limit_bytes=VMEM_LIMIT)


def _dot(a, b, precision=None):
    return jnp.dot(a, b, preferred_element_type=F32, precision=precision)


def _dot_nt(a, b, precision=None):
    return lax.dot_general(a, b, (((1,), (1,)), ((), ())), preferred_element_type=F32, precision=precision)


def _dot_tn(a, b, precision=None):
    return lax.dot_general(a, b, (((0,), (0,)), ((), ())), preferred_element_type=F32, precision=precision)


def _head_blockdiag(n, scale=1.0):
    r = lax.broadcasted_iota(I32, (n, n), 0) // HEAD_DIM
    c = lax.broadcasted_iota(I32, (n, n), 1) // HEAD_DIM
    return jnp.where(r == c, scale, 0.0).astype(F32)


def _sigmoid(x):
    return 1.0 / (1.0 + jnp.exp(-x))


def _inproj_body(x_ref, g_ref, w_ref, o_ref, h_scr):
    @pl.when(pl.program_id(1) == 0)
    def _():
        x = x_ref[...]
        ms = jnp.mean(x * x, axis=-1, keepdims=True)
        h_scr[...] = (x * lax.rsqrt(ms + RMS_EPS) * g_ref[...]).astype(BF16)

    o_ref[...] = _dot(h_scr[...], w_ref[...])


def _inproj(x, g, w, tm, tn):
    m, d = x.shape
    n = w.shape[1]
    return pl.pallas_call(
        _inproj_body,
        grid=(m // tm, n // tn),
        in_specs=[pl.BlockSpec((tm, d), lambda i, j: (i, 0)),
                  pl.BlockSpec((1, d), lambda i, j: (0, 0)),
                  pl.BlockSpec((d, tn), lambda i, j: (0, j))],
        out_specs=pl.BlockSpec((tm, tn), lambda i, j: (i, j)),
        out_shape=jax.ShapeDtypeStruct((m, n), F32),
        scratch_shapes=[pltpu.VMEM((tm, d), BF16)],
        compiler_params=_cparams(("parallel", "arbitrary")),
        name="inproj",
    )(x, g, w)


def _outproj_body(a_ref, b_ref, c_ref, d_ref, x_ref, g_ref, w_ref, o_ref):
    gw = GROUP_WIDTH
    y = _dot(a_ref[...], w_ref[0:gw, :])
    y += _dot(b_ref[...], w_ref[gw:2 * gw, :])
    y += _dot(c_ref[...], w_ref[2 * gw:3 * gw, :])
    y += _dot(d_ref[...], w_ref[3 * gw:4 * gw, :])
    ms = jnp.mean(y * y, axis=-1, keepdims=True)
    o_ref[...] = x_ref[...] + y * lax.rsqrt(ms + RMS_EPS) * g_ref[...]


def _outproj(parts, x, g, w, tm):
    m, d = x.shape
    gw = GROUP_WIDTH
    part_spec = pl.BlockSpec((tm, gw), lambda i: (i, 0))
    return pl.pallas_call(
        _outproj_body,
        grid=(m // tm,),
        in_specs=[part_spec, part_spec, part_spec, part_spec,
                  pl.BlockSpec((tm, d), lambda i: (i, 0)),
                  pl.BlockSpec((1, d), lambda i: (0, 0)),
                  pl.BlockSpec((4 * gw, d), lambda i: (0, 0))],
        out_specs=pl.BlockSpec((tm, d), lambda i: (i, 0)),
        out_shape=jax.ShapeDtypeStruct((m, d), F32),
        compiler_params=_cparams(("parallel",)),
        name="outproj",
    )(*parts, x, g, w)


def _ffn_body(x_ref, g2_ref, g3_ref, wu_ref, wd_ref, o_ref, h_scr, acc_scr):
    f = pl.program_id(1)

    @pl.when(f == 0)
    def _():
        x = x_ref[...]
        ms = jnp.mean(x * x, axis=-1, keepdims=True)
        h_scr[...] = (x * lax.rsqrt(ms + RMS_EPS) * g2_ref[...]).astype(BF16)
        acc_scr[...] = jnp.zeros_like(acc_scr)

    u = jnp.maximum(_dot(h_scr[...], wu_ref[...]), 0.0)
    acc_scr[...] += _dot((u * u).astype(BF16), wd_ref[...])

    @pl.when(f == pl.num_programs(1) - 1)
    def _():
        y = acc_scr[...]
        ms = jnp.mean(y * y, axis=-1, keepdims=True)
        o_ref[...] = x_ref[...] + y * lax.rsqrt(ms + RMS_EPS) * g3_ref[...]


def _ffn(x, g2, g3, wu, wd, tm, tf):
    m, d = x.shape
    f = wu.shape[1]
    return pl.pallas_call(
        _ffn_body,
        grid=(m // tm, f // tf),
        in_specs=[pl.BlockSpec((tm, d), lambda i, j: (i, 0)),
                  pl.BlockSpec((1, d), lambda i, j: (0, 0)),
                  pl.BlockSpec((1, d), lambda i, j: (0, 0)),
                  pl.BlockSpec((d, tf), lambda i, j: (0, j)),
                  pl.BlockSpec((tf, d), lambda i, j: (j, 0))],
        out_specs=pl.BlockSpec((tm, d), lambda i, j: (i, 0)),
        out_shape=jax.ShapeDtypeStruct((m, d), F32),
        scratch_shapes=[pltpu.VMEM((tm, d), BF16), pltpu.VMEM((tm, d), F32)],
        compiler_params=_cparams(("parallel", "arbitrary")),
        name="ffn",
    )(x, g2, g3, wu, wd)


MOBA_HEAD_GROUP = 4


def _moba_prompt_body(q_ref, k_ref, v_ref, o_ref, kmean_scr, kext, vext, *, nb):
    blk = MOBA_BLOCK
    hg = MOBA_HEAD_GROUP
    qt = pl.program_id(2)
    pair_of = lambda j: slice(LANE * (j // 2), LANE * (j // 2 + 1))
    dst_of = lambda j: slice(LANE * j, LANE * (j + 1))

    @pl.when(qt == 0)
    def _():
        kmean_scr[...] = jnp.zeros_like(kmean_scr)
        mine_t = lax.broadcasted_iota(I32, (k_ref.shape[0], LANE), 1) // HEAD_DIM
        for j in range(hg):
            kp = jnp.where(mine_t == j % 2, k_ref[:, pair_of(j)], 0.0)
            kext[:, dst_of(j)] = kp.astype(BF16)
            vext[:, dst_of(j)] = jnp.where(mine_t == j % 2, v_ref[:, pair_of(j)], 1.0).astype(BF16)
            for n in range(nb):
                kmean_scr[n:n + 1, dst_of(j)] = jnp.sum(kp[n * blk:(n + 1) * blk], axis=0, keepdims=True) * (1.0 / blk)

    row = lax.broadcasted_iota(I32, (blk, blk), 0)
    col = lax.broadcasted_iota(I32, (blk, blk), 1)
    bias_own = jnp.where(col <= row, 0.0, NEG_INF)
    blk_id = lax.broadcasted_iota(I32, (blk, LANE), 1)
    eligible = blk_id < qt
    not_sel, qb = [], []
    for j in range(hg):
        q = q_ref[:, pair_of(j)]
        gate = jnp.where(eligible, _dot_nt(q, kmean_scr[:, dst_of(j)], HI), -jnp.inf)
        rank = jnp.zeros((blk, LANE), I32)
        for m in range(nb):
            gm = gate[:, m:m + 1]
            rank += ((gm > gate) | ((gm == gate) & (m < blk_id))).astype(I32)
        not_sel.append(jnp.where(eligible & (rank < MOBA_TOPK), 0.0, 1.0).astype(BF16))
        qb.append((q * ATTN_SCALE).astype(BF16))

    def attend(carry, start, biases):
        out = []
        for j in range(hg):
            m_i, acc = carry[j]
            s = _dot_nt(qb[j], kext[pl.ds(start, blk), dst_of(j)]) + biases[j]
            m_new = jnp.maximum(m_i, jnp.max(s, axis=-1, keepdims=True))
            p = jnp.exp(s - m_new)
            out.append((m_new, jnp.exp(m_i - m_new) * acc + _dot(p.astype(BF16), vext[pl.ds(start, blk), dst_of(j)])))
        return tuple(out)

    def body(n, carry):
        pick = (lax.broadcasted_iota(I32, (LANE, blk), 0) == n).astype(BF16)
        return attend(carry, pl.multiple_of(n * blk, blk), [_dot(ns, pick) * NEG_INF for ns in not_sel])

    init = tuple((jnp.full((blk, 1), NEG_INF, F32), jnp.zeros((blk, LANE), F32)) for _ in range(hg))
    carry = lax.fori_loop(0, qt, body, init)
    fin = attend(carry, pl.multiple_of(qt * blk, blk), [bias_own] * hg)
    half = lax.broadcasted_iota(I32, (blk, LANE), 1) // HEAD_DIM
    for jp in range(hg // 2):
        outs = [fin[2 * jp + e][1] / pltpu.roll(fin[2 * jp + e][1], HEAD_DIM, 1) for e in range(2)]
        o_ref[:, dst_of(jp)] = jnp.where(half == 0, outs[0], outs[1]).astype(o_ref.dtype)


def _moba_prompt(cols, b, t):
    blk = MOBA_BLOCK
    nb = t // blk
    w = MOBA_HEAD_GROUP * HEAD_DIM
    qoff = OFF_MOBA // w
    koff = (OFF_MOBA + GROUP_WIDTH) // w
    voff = (OFF_MOBA + 2 * GROUP_WIDTH) // w
    return pl.pallas_call(
        functools.partial(_moba_prompt_body, nb=nb),
        grid=(b, GROUP_WIDTH // w, nb),
        in_specs=[pl.BlockSpec((blk, w), lambda i, h, q: (i * nb + q, qoff + h)),
                  pl.BlockSpec((t, w), lambda i, h, q: (i, koff + h)),
                  pl.BlockSpec((t, w), lambda i, h, q: (i, voff + h))],
        out_specs=pl.BlockSpec((blk, w), lambda i, h, q: (i * nb + q, h)),
        out_shape=jax.ShapeDtypeStruct((b * t, GROUP_WIDTH), BF16),
        scratch_shapes=[pltpu.VMEM((LANE, MOBA_HEAD_GROUP * LANE), F32),
                        pltpu.VMEM((t, MOBA_HEAD_GROUP * LANE), BF16), pltpu.VMEM((t, MOBA_HEAD_GROUP * LANE), BF16)],
        compiler_params=_cparams(("parallel", "parallel", "arbitrary")),
        name="moba_prompt",
    )(cols, cols, cols)


def _gelu_tanh(x):
    return x * (0.5 * (1.0 + jnp.tanh(np.sqrt(2.0 / np.pi).astype(np.float32) * (x + 0.044715 * (x * x * x)))))


def _cmp_bias(pe_ref, w1_ref, b1_ref, kv):
    pe8 = jnp.broadcast_to(pe_ref[kv], (8, CMP_LEN * HEAD_DIM))
    return _dot(pe8, w1_ref[kv], HI)[0:1, :] + b1_ref[kv]


def _compress(load_rows, n, w1r_ref, kv, bias, w2):
    acc = [jnp.zeros((n, 2 * CMP_HIDDEN), F32) for _ in range(NSA_KV_HEADS)]
    for r in range(CMP_STRIDE):
        x = load_rows(r).astype(BF16)
        for g in range(NSA_KV_HEADS):
            acc[g] += _dot(x[:, g * HEAD_DIM:(g + 1) * HEAD_DIM], w1r_ref[kv, r])
    out = []
    for g in range(NSA_KV_HEADS):
        hid = acc[g][:, :CMP_HIDDEN] + pltpu.roll(acc[g][:, CMP_HIDDEN:], n - 1, 0) + bias
        out.append(_dot(_gelu_tanh(hid).astype(BF16), w2.astype(BF16)))
    return out


def _softmax_rows(l, mask):
    m = jnp.max(jnp.where(mask, l, NEG_INF), axis=-1, keepdims=True)
    e = jnp.where(mask, jnp.exp(l - m), 0.0)
    s = jnp.sum(e, axis=-1, keepdims=True)
    return jnp.where(s > 0.0, e / jnp.where(s > 0.0, s, 1.0), 0.0)


def _slc_matrix(n_cmp_pad, n_cmp, n_slc_pad):
    n = lax.broadcasted_iota(I32, (n_cmp_pad, n_slc_pad), 0)
    j = lax.broadcasted_iota(I32, (n_cmp_pad, n_slc_pad), 1)
    ratio = SLC_BLOCK // CMP_STRIDE
    return ((n >= ratio * j - 1) & (n <= ratio * j + ratio - 1) & (n < n_cmp)).astype(F32)


def _topk_rows(score, ids, n_cand, k):
    rank = jnp.zeros(score.shape, I32)
    for m in range(n_cand):
        sm = score[:, m:m + 1]
        rank += ((sm > score) | ((sm == score) & (m < ids))).astype(I32)
    return rank < k


def _nsa_prompt_body(q0_ref, q1_ref, kc_ref, vc_ref, ksvs_ref, kwvw_ref, gate_ref, w1r_ref, w1_ref, b1_ref, w2_ref,
                     pe_ref, o_ref, ck_scr, cv_scr, ks_ext, vs_ext, kw_ext, vw_ext, *, t):
    tq = 256
    n_chunk = t // CMP_STRIDE
    n_cmp = n_chunk - CMP_LEN // CMP_STRIDE + 1
    qt = pl.program_id(1)
    hd = HEAD_DIM

    @pl.when(qt == 0)
    def _():
        for kv, (src, dst) in enumerate(((kc_ref, ck_scr), (vc_ref, cv_scr))):
            bias = _cmp_bias(pe_ref, w1_ref, b1_ref, kv)
            out = _compress(lambda r: src[pl.ds(r, n_chunk, stride=CMP_STRIDE), :],
                            n_chunk, w1r_ref, kv, bias, w2_ref[kv])
            for g in range(NSA_KV_HEADS):
                dst[g] = out[g]
        half_t = lax.broadcasted_iota(I32, (t, LANE), 1) // hd
        for src, k_ext, v_ext in ((ksvs_ref, ks_ext, vs_ext), (kwvw_ref, kw_ext, vw_ext)):
            for g in range(NSA_KV_HEADS):
                k_ext[g] = jnp.where(half_t == g, src[:, 0:LANE], 0.0).astype(BF16)
                v_ext[g] = jnp.where(half_t == g, src[:, LANE:2 * LANE], 1.0).astype(BF16)

    tpos = qt * tq + lax.broadcasted_iota(I32, (tq, 1), 0)
    tpos4 = jnp.concatenate([tpos] * NSA_GROUP, axis=0)
    gates = _sigmoid(gate_ref[...])
    slc_ids = lax.broadcasted_iota(I32, (tq, LANE), 1)
    n_slc = t // SLC_BLOCK
    kcol = lax.broadcasted_iota(I32, (tq, tq), 1)
    jrow = lax.broadcasted_iota(I32, (LANE, tq), 0)
    kcol_e = lax.broadcasted_iota(I32, (LANE, tq), 1)
    for g in range(NSA_KV_HEADS):
        q_ref = q0_ref if g == 0 else q1_ref
        q4f = jnp.concatenate([q_ref[:, j * hd:(j + 1) * hd] for j in range(NSA_GROUP)], axis=0)
        q4 = q4f.astype(BF16)
        cmp_end = lax.broadcasted_iota(I32, (NSA_GROUP * tq, n_chunk), 1) * CMP_STRIDE + (CMP_LEN - 1)
        l_cmp = _dot_nt(q4, ck_scr[g].astype(BF16)) * ATTN_SCALE
        p_cmp = _softmax_rows(l_cmp, cmp_end <= tpos4)
        o_cmp = _dot(p_cmp.astype(BF16), cv_scr[g].astype(BF16))
        imp = p_cmp[0:tq]
        for j in range(1, NSA_GROUP):
            imp = imp + p_cmp[j * tq:(j + 1) * tq]
        n_slc_r = -(-n_slc // 8) * 8
        jn = lax.broadcasted_iota(I32, (n_slc_r, n_chunk), 0)
        nn = lax.broadcasted_iota(I32, (n_slc_r, n_chunk), 1)
        ratio = SLC_BLOCK // CMP_STRIDE
        slc_t = ((nn >= ratio * jn - 1) & (nn <= ratio * jn + ratio - 1) & (nn < n_cmp)).astype(F32)
        imp_t = _dot_nt(slc_t, imp, HI)
        jt = lax.broadcasted_iota(I32, (n_slc_r, tq), 0)
        cur_t = (qt * tq + lax.broadcasted_iota(I32, (n_slc_r, tq), 1)) // SLC_BLOCK
        eligible = (jt <= cur_t) & (jt < n_slc)
        forced = (jt == 0) | (jt == cur_t) | (jt == cur_t - 1)
        score = jnp.where(eligible, jnp.where(forced, jnp.inf, imp_t), -jnp.inf)
        rank = jnp.zeros((n_slc_r, tq), I32)
        for m in range(n_slc):
            sm = score[m:m + 1, :]
            rank += ((sm > score) | ((sm == score) & (m < jt))).astype(I32)
        sel_t = jnp.where(eligible & (rank < SLC_TOPK), 1.0, 0.0)
        sel = jnp.concatenate([sel_t, jnp.zeros((LANE - n_slc_r, tq), F32)], axis=0).T.astype(BF16)

        half_q = lax.broadcasted_iota(I32, (tq, LANE), 1) // hd
        q4e = []
        for j in range(NSA_GROUP):
            qp = q_ref[:, (j // 2) * LANE:(j // 2 + 1) * LANE]
            qp = qp if j % 2 == g else pltpu.roll(qp, hd, 1)
            q4e.append(jnp.where(half_q == g, qp * ATTN_SCALE, 0.0).astype(BF16))
        q4e = jnp.concatenate(q4e, axis=0)

        def attend(carry, start, bias, k_ext, v_ext):
            m_i, acc = carry
            s = (_dot_nt(q4e, k_ext[g, pl.ds(start, tq), :]).reshape(NSA_GROUP, tq, tq) + bias[None]
                 ).reshape(NSA_GROUP * tq, tq)
            m_new = jnp.maximum(m_i, jnp.max(s, axis=-1, keepdims=True))
            p = jnp.exp(s - m_new)
            return m_new, jnp.exp(m_i - m_new) * acc + _dot(p.astype(BF16), v_ext[g, pl.ds(start, tq), :])

        def sel_body(n, carry):
            expand = (jrow == (tq // SLC_BLOCK) * n + kcol_e // SLC_BLOCK).astype(BF16)
            mask = (_dot(sel, expand) > 0.5) & (n * tq + kcol <= tpos)
            return attend(carry, pl.multiple_of(n * tq, tq), jnp.where(mask, 0.0, NEG_INF), ks_ext, vs_ext)

        def win_body(n, carry):
            dist = tpos - (n * tq + kcol)
            bias = jnp.where((dist >= 0) & (dist < WINDOW), 0.0, NEG_INF)
            return attend(carry, pl.multiple_of(n * tq, tq), bias, kw_ext, vw_ext)

        rows = NSA_GROUP * tq
        init = (jnp.full((rows, 1), NEG_INF, F32), jnp.zeros((rows, LANE), F32))
        _, acc_s = lax.fori_loop(0, qt + 1, sel_body, init)
        _, acc_w = lax.fori_loop(jnp.maximum(qt - (WINDOW // tq), 0), qt + 1, win_body, init)
        o_sel = (acc_s / pltpu.roll(acc_s, hd, 1))[:, g * hd:(g + 1) * hd]
        o_win = (acc_w / pltpu.roll(acc_w, hd, 1))[:, g * hd:(g + 1) * hd]
        for j in range(NSA_GROUP):
            h = g * NSA_GROUP + j
            rs = slice(j * tq, (j + 1) * tq)
            o = (gates[:, h:h + 1] * o_cmp[rs] + gates[:, N_HEADS + h:N_HEADS + h + 1] * o_sel[rs]
                 + gates[:, 2 * N_HEADS + h:2 * N_HEADS + h + 1] * o_win[rs])
            o_ref[:, h * hd:(h + 1) * hd] = o.astype(o_ref.dtype)


def _nsa_prompt(cols, cmp_w, b, t):
    w1r, w1, b1, w2, pe = cmp_w
    tq = 256
    nq = t // tq
    n_chunk = t // CMP_STRIDE
    qoff = OFF_NSAQ // 256
    kvoff = OFF_NSAKV // 256
    full = lambda a: pl.BlockSpec(a.shape, lambda i, q: (0,) * a.ndim)
    return pl.pallas_call(
        functools.partial(_nsa_prompt_body, t=t),
        grid=(b, nq),
        in_specs=[pl.BlockSpec((tq, 256), lambda i, q: (i * nq + q, qoff)),
                  pl.BlockSpec((tq, 256), lambda i, q: (i * nq + q, qoff + 1)),
                  pl.BlockSpec((t, LANE), lambda i, q: (i, 2 * kvoff)),
                  pl.BlockSpec((t, LANE), lambda i, q: (i, 2 * kvoff + 1)),
                  pl.BlockSpec((t, 256), lambda i, q: (i, kvoff + 1)),
                  pl.BlockSpec((t, 256), lambda i, q: (i, kvoff + 2)),
                  pl.BlockSpec((tq, LANE), lambda i, q: (i * nq + q, OFF_GATE // LANE)),
                  full(w1r), full(w1), full(b1), full(w2), full(pe)],
        out_specs=pl.BlockSpec((tq, GROUP_WIDTH), lambda i, q: (i * nq + q, 0)),
        out_shape=jax.ShapeDtypeStruct((b * t, GROUP_WIDTH), BF16),
        scratch_shapes=[pltpu.VMEM((NSA_KV_HEADS, n_chunk, HEAD_DIM), F32),
                        pltpu.VMEM((NSA_KV_HEADS, n_chunk, HEAD_DIM), F32)]
                       + [pltpu.VMEM((NSA_KV_HEADS, t, LANE), BF16)] * 4,
        compiler_params=_cparams(("parallel", "arbitrary")),
        name="nsa_prompt",
    )(cols, cols, cols, cols, cols, cols, cols, w1r, w1, b1, w2, pe)


def _rope(x, cos, sin):
    half = HEAD_DIM // 2
    lane = lax.broadcasted_iota(I32, x.shape, 1) % HEAD_DIM
    nxt = pltpu.roll(x, x.shape[1] - half, 1)
    prv = pltpu.roll(x, half, 1)
    return x * cos + jnp.where(lane < half, -nxt, prv) * sin


def _head_norm(y, g, b, eps):
    avg = _head_blockdiag(y.shape[1], 1.0 / HEAD_DIM)
    mu = _dot(y, avg, HI)
    d = y - mu
    var = _dot(d * d, avg, HI)
    return d * lax.rsqrt(var + eps) * g + b


def _ret_prompt_body(q_ref, k_ref, v_ref, gate_ref, cos_ref, sin_ref, dmask_ref, xi_ref, zeta_ref, cd_ref,
                     gn_g_ref, gn_b_ref, s0_ref, o_ref, s_ref, o_scr):
    @pl.when(pl.program_id(1) == 0)
    def _():
        s_ref[...] = s0_ref[...]

    cos = cos_ref[...]
    sin = sin_ref[...]
    q = _rope(q_ref[...], cos, sin)
    k = _rope(k_ref[...], cos, sin) * ATTN_SCALE
    kz = (k * zeta_ref[...]).astype(BF16)
    qb = q.astype(BF16)
    kb = k.astype(BF16)
    for h in range(N_HEADS):
        sl = slice(h * HEAD_DIM, (h + 1) * HEAD_DIM)
        vb = v_ref[:, sl].astype(BF16)
        s = s_ref[0, h]
        att = _dot_nt(qb[:, sl], kb[:, sl]) * dmask_ref[h]
        o_scr[:, sl] = _dot(att.astype(BF16), vb) + _dot(qb[:, sl], s.astype(BF16)) * xi_ref[:, sl]
        s_ref[0, h] = s * cd_ref[:, sl] + _dot_tn(kz[:, sl], vb)
    gate = gate_ref[...]
    y = _head_norm(o_scr[...], gn_g_ref[...], gn_b_ref[...], GN_EPS)
    o_ref[...] = (gate * _sigmoid(gate) * y).astype(o_ref.dtype)


def _ret_tables(log_gamma, c):
    idx = jnp.arange(c, dtype=F32)
    diff = idx[:, None] - idx[None, :]
    dmask = jnp.where(diff >= 0, jnp.exp(jnp.maximum(diff, 0.0)[None] * log_gamma[:, None, None]), 0.0)
    rep = lambda z: jnp.repeat(z, HEAD_DIM, axis=-1)
    xi = rep(jnp.exp((idx + 1.0)[:, None] * log_gamma[None, :]))
    zeta = rep(jnp.exp((c - 1.0 - idx)[:, None] * log_gamma[None, :]))
    cd = rep(jnp.exp(c * log_gamma)[None, :])
    return dmask, xi, zeta, cd


def _rope_tables(pos):
    half = HEAD_DIM // 2
    inv = ROPE_BASE ** (-jnp.arange(half, dtype=F32) / half)
    ang = pos.astype(F32)[:, None] * inv[None, :]
    tile = lambda z: jnp.tile(z, (1, 2 * N_HEADS))
    return tile(jnp.cos(ang)), tile(jnp.sin(ang))


def _ret_prompt(cols, s0, tables, rope, gn_g, gn_b, b, t):
    c = RET_CHUNK
    nc = t // c
    gw = GROUP_WIDTH
    off = OFF_RET // gw
    dmask, xi, zeta, cd = tables
    cos, sin = rope
    col_spec = lambda j: pl.BlockSpec((c, gw), lambda i, n: (i * nc + n, off + j))
    full = lambda a: pl.BlockSpec(a.shape, lambda i, n: (0,) * a.ndim)
    st_spec = pl.BlockSpec((1, N_HEADS, HEAD_DIM, HEAD_DIM), lambda i, n: (i, 0, 0, 0))
    return pl.pallas_call(
        _ret_prompt_body,
        grid=(b, nc),
        in_specs=[col_spec(0), col_spec(1), col_spec(2), col_spec(3),
                  pl.BlockSpec((c, gw), lambda i, n: (n, 0)), pl.BlockSpec((c, gw), lambda i, n: (n, 0)),
                  full(dmask), full(xi), full(zeta), full(cd), full(gn_g), full(gn_b), st_spec],
        out_specs=[pl.BlockSpec((c, gw), lambda i, n: (i * nc + n, 0)), st_spec],
        out_shape=[jax.ShapeDtypeStruct((b * t, gw), BF16),
                   jax.ShapeDtypeStruct((b, N_HEADS, HEAD_DIM, HEAD_DIM), F32)],
        scratch_shapes=[pltpu.VMEM((c, gw), F32)],
        compiler_params=_cparams(("parallel", "arbitrary")),
        name="ret_prompt",
    )(cols, cols, cols, cols, cos, sin, dmask, xi, zeta, cd, gn_g, gn_b, s0)


def _softplus(x):
    return jnp.maximum(x, 0.0) + jnp.log(1.0 + jnp.exp(-jnp.abs(x)))


def _rwkv_prep(c, prev, p):
    mu, w0, w2, a0, a2, g2, k_k, k_a = p
    gw = GROUP_WIDTH
    mixed = c + (prev - c) * mu
    r, k, v = mixed[:, :gw], mixed[:, gw:2 * gw], mixed[:, 2 * gw:3 * gw]
    o1 = 3 * gw
    o2 = o1 + RWKV_DECAY_RANK
    o3 = o2 + RWKV_AAA_RANK
    xw, xa, xg = mixed[:, o1:o2], mixed[:, o2:o3], mixed[:, o3:]
    w_log = -_softplus(-(w0 + _dot(jnp.tanh(xw), w2, HI))) - 0.5
    decay = jnp.exp(-jnp.exp(w_log))
    a = _sigmoid(a0 + _dot(xa, a2, HI))
    gate = _dot(_sigmoid(xg).astype(BF16), g2.astype(BF16))
    kk = k * k_k
    norm = jnp.sqrt(_dot(kk * kk, _head_blockdiag(gw), HI))
    kk = kk / jnp.maximum(norm, 1e-12)
    k = k * (1.0 + (a - 1.0) * k_a)
    return r, decay, k, v, kk, kk * a, gate


def _rwkv_steps(vecs, get_state, put_state, n_steps):
    r8, w8, k8, v8, kk8, ka8 = vecs
    lane = lax.broadcasted_iota(I32, (HEAD_DIM, LANE), 1)
    pad = jnp.zeros((LANE - 8, LANE), F32)
    ys = []
    for hp in range(N_HEADS // 2):
        vt = jnp.concatenate([v8[:, hp * LANE:(hp + 1) * LANE], pad], axis=0).T
        yts = []
        for h2 in range(2):
            h = 2 * hp + h2
            sl = slice(h * HEAD_DIM, (h + 1) * HEAD_DIM)
            yt = jnp.zeros((HEAD_DIM, LANE), F32)
            s = get_state(h, 0)
            for j in range(8):
                if n_steps == 1:
                    s = get_state(h, j)
                vcol = vt[h2 * HEAD_DIM:(h2 + 1) * HEAD_DIM, j:j + 1]
                sa = jnp.sum(s * kk8[j:j + 1, sl], axis=-1, keepdims=True)
                s = s * w8[j:j + 1, sl] - sa * ka8[j:j + 1, sl] + vcol * k8[j:j + 1, sl]
                ycol = jnp.sum(s * r8[j:j + 1, sl], axis=-1, keepdims=True)
                yt = jnp.where(lane == j, ycol, yt)
                if n_steps == 1:
                    put_state(h, j, s)
            if n_steps != 1:
                put_state(h, 0, s)
            yts.append(yt)
        ys.append(jnp.concatenate(yts, axis=0).T[0:8, :])
    return jnp.concatenate(ys, axis=1)


def _rwkv_post(ys, r, k, v, gate, r_k, ln_g, ln_b):
    y = _head_norm(ys, ln_g, ln_b, RWKV_LN_EPS)
    y = y + _dot(r * k * r_k, _head_blockdiag(GROUP_WIDTH), HI) * v
    return y * gate


def _rwkv_prep_body(c_ref, shift_ref, mu_ref, w0_ref, w2_ref, a0_ref, a2_ref, g2_ref, kk_ref, ka_ref, rk_ref,
                    w_o, kk_o, kka_o, k_o, r_o, v_o, bonus_o, gate_o, carry):
    tc = c_ref.shape[0]

    @pl.when(pl.program_id(1) == 0)
    def _():
        carry[...] = shift_ref[...]

    c = c_ref[...]
    row = lax.broadcasted_iota(I32, c.shape, 0)
    prev = jnp.where(row == 0, carry[...], pltpu.roll(c, 1, 0))
    carry[...] = c[tc - 1:tc, :]
    p = (mu_ref[...], w0_ref[...], w2_ref[...], a0_ref[...], a2_ref[...], g2_ref[...], kk_ref[...], ka_ref[...])
    r, w, k, v, kk, kka, gate = _rwkv_prep(c, prev, p)
    w_o[...] = w.T
    kk_o[...] = kk.T
    kka_o[...] = kka.T
    k_o[...] = k.T
    r_o[...] = r.T
    v_o[...] = v.T
    bonus_o[...] = _dot(r * k * rk_ref[...], _head_blockdiag(GROUP_WIDTH), HI) * v
    gate_o[...] = gate


def _rwkv_scan_body(w_ref, kk_ref, kka_ref, k_ref, r_ref, v_ref, s0_ref, y_ref, s_ref):
    @pl.when(pl.program_id(0) == 0)
    def _():
        s_ref[...] = s0_ref[...]

    n_vq = s_ref.shape[0]

    def step(t, _):
        vrows = v_ref[t]
        for vq in range(n_vq):
            s = s_ref[vq]
            sa = jnp.sum(s * kk_ref[t], axis=0, keepdims=True)
            s = s * w_ref[t] - sa * kka_ref[t] + vrows[vq:vq + 1, :] * k_ref[t]
            s_ref[vq] = s
            y_ref[t, vq:vq + 1, :] = jnp.sum(s * r_ref[t], axis=0, keepdims=True)
        return 0

    lax.fori_loop(0, v_ref.shape[0], step, 0)


def _rwkv_post_body(y_ref, bonus_ref, gate_ref, lng_ref, lnb_ref, o_ref):
    y = _head_norm(y_ref[...], lng_ref[...], lnb_ref[...], RWKV_LN_EPS)
    o_ref[...] = ((y + bonus_ref[...]) * gate_ref[...]).astype(o_ref.dtype)


def _rwkv_prompt(cols, shift_prev, s0, params, b, t, tc=256, tscan=64):
    mu, w0, w2, a0, a2, g2, k_k, k_a, r_k, ln_g, ln_b = params
    nt = t // tc
    gw = GROUP_WIDTH
    hd = HEAD_DIM
    chains = b * N_HEADS
    rep = LANE // chains
    assert rep * chains == LANE and hd % rep == 0
    n_vq = hd // rep
    full = lambda a: pl.BlockSpec(a.shape, lambda i, n: (0,) * a.ndim)
    tok = pl.BlockSpec((tc, gw), lambda i, n: (i * nt + n, 0))
    vec = jax.ShapeDtypeStruct((b * t, gw), F32)
    prep_in = (mu, w0, w2, a0, a2, g2, k_k, k_a, r_k)
    w, kk, kka, k, r, v, bonus, gate = pl.pallas_call(
        _rwkv_prep_body,
        grid=(b, nt),
        in_specs=[pl.BlockSpec((tc, RWKV_COLS), lambda i, n: (i * nt + n, OFF_RWKV // RWKV_COLS)),
                  pl.BlockSpec((None, 1, RWKV_COLS), lambda i, n: (i, 0, 0))] + [full(a) for a in prep_in],
        out_specs=[pl.BlockSpec((gw, tc), lambda i, n: (0, i * nt + n))] * 6 + [tok] * 2,
        out_shape=[jax.ShapeDtypeStruct((gw, b * t), F32)] * 6 + [vec] * 2,
        scratch_shapes=[pltpu.VMEM((1, RWKV_COLS), F32)],
        compiler_params=_cparams(("parallel", "arbitrary")),
        name="rwkv_prep",
    )(cols, shift_prev.reshape(b, 1, RWKV_COLS), *prep_in)

    def key_tiles(x):
        x = jnp.transpose(x.reshape(N_HEADS, hd, b, t), (3, 1, 2, 0)).reshape(t, hd, chains)
        return jnp.tile(x, (1, 1, rep))
    xts = [key_tiles(z) for z in (w, kk, kka, k, r)]
    vt = jnp.transpose(v.reshape(N_HEADS, rep, n_vq, b, t), (4, 2, 1, 3, 0)).reshape(t, n_vq, LANE)
    s0t = jnp.transpose(s0.reshape(b, N_HEADS, rep, n_vq, hd), (3, 4, 2, 0, 1)).reshape(n_vq, hd, LANE)

    ns = t // tscan
    st_spec = pl.BlockSpec((n_vq, hd, LANE), lambda n: (0, 0, 0))
    yt, st = pl.pallas_call(
        _rwkv_scan_body,
        grid=(ns,),
        in_specs=[pl.BlockSpec((tscan, hd, LANE), lambda n: (n, 0, 0))] * 5
                 + [pl.BlockSpec((tscan, n_vq, LANE), lambda n: (n, 0, 0)), st_spec],
        out_specs=[pl.BlockSpec((tscan, n_vq, LANE), lambda n: (n, 0, 0)), st_spec],
        out_shape=[jax.ShapeDtypeStruct((t, n_vq, LANE), F32), jax.ShapeDtypeStruct((n_vq, hd, LANE), F32)],
        compiler_params=_cparams(("arbitrary",)),
        name="rwkv_scan",
    )(*xts, vt, s0t)
    ys = jnp.transpose(yt.reshape(t, n_vq, rep, b, N_HEADS), (3, 0, 4, 2, 1)).reshape(b * t, gw)
    s_fin = jnp.transpose(st.reshape(n_vq, hd, rep, b, N_HEADS), (3, 4, 2, 0, 1)).reshape(b, N_HEADS, hd, hd)

    tm = min(512, b * t)
    tokm = pl.BlockSpec((tm, gw), lambda i: (i, 0))
    o = pl.pallas_call(
        _rwkv_post_body,
        grid=(b * t // tm,),
        in_specs=[tokm, tokm, tokm, pl.BlockSpec((1, gw), lambda i: (0, 0)), pl.BlockSpec((1, gw), lambda i: (0, 0))],
        out_specs=tokm,
        out_shape=jax.ShapeDtypeStruct((b * t, gw), BF16),
        compiler_params=_cparams(("parallel",)),
        name="rwkv_post",
    )(ys, bonus, gate, ln_g, ln_b)
    return o, s_fin


PAGES_PER_STEP = 4


def _past_stats_body(pt_ref, *refs, n_pages):
    pps = PAGES_PER_STEP
    kt_refs, kct_refs, vct_refs = refs[:pps], refs[pps:2 * pps], refs[2 * pps:3 * pps]
    (q_ref, w1r_ref, w1_ref, b1_ref, w2_ref, pe_ref, top_ref, cmp_ref, qb, gsum, kc_rows, vc_rows) = refs[3 * pps:]
    step = pl.program_id(1)
    pages_per_blk = MOBA_BLOCK // PAGE_SIZE
    nb = n_pages // pages_per_blk
    n_chunk = n_pages * PAGE_SIZE // CMP_STRIDE

    @pl.when(step == 0)
    def _():
        gsum[...] = jnp.zeros_like(gsum)
        qb[...] = jnp.broadcast_to(q_ref[...], qb.shape)

    for i in range(pps):
        p = step * pps + i
        gsum[p // pages_per_blk] += jnp.sum(kt_refs[i][...] * qb[...], axis=1)
        row0 = pl.multiple_of(p * PAGE_SIZE, PAGE_SIZE)
        kc_rows[pl.ds(row0, PAGE_SIZE), :] = kct_refs[i][...].reshape(LANE, PAGE_SIZE).T
        vc_rows[pl.ds(row0, PAGE_SIZE), :] = vct_refs[i][...].reshape(LANE, PAGE_SIZE).T

    @pl.when(step == n_pages // pps - 1)
    def _():
        gate = jnp.sum(gsum[...], axis=-1) * (1.0 / MOBA_BLOCK)
        ids = lax.broadcasted_iota(I32, (nb, N_HEADS), 0)
        rows8 = lax.broadcasted_iota(I32, (8, N_HEADS), 0)
        top = jnp.zeros((8, N_HEADS), I32)
        for j in range(MOBA_TOPK):
            best = jnp.max(gate, axis=0, keepdims=True)
            arg = jnp.min(jnp.where(gate == best, ids, nb), axis=0, keepdims=True)
            top = jnp.where(rows8 == j, arg, top)
            gate = jnp.where(ids == arg, -jnp.inf, gate)
        top_ref[...] = top
        for kv, src in enumerate((kc_rows, vc_rows)):
            bias = _cmp_bias(pe_ref, w1_ref, b1_ref, kv)
            out = _compress(lambda r: src[pl.ds(r, n_chunk, stride=CMP_STRIDE), :], n_chunk, w1r_ref, kv, bias,
                            w2_ref[kv])
            for g in range(NSA_KV_HEADS):
                cmp_ref[kv * NSA_KV_HEADS + g] = out[g]


def _past_stats(page_table, moba_t, nsa_t, q_col, cmp_w, layer):
    w1r, w1, b1, w2, pe = cmp_w
    bs, n_pages = page_table.shape
    nb = n_pages * PAGE_SIZE // MOBA_BLOCK
    rows = n_pages * PAGE_SIZE
    n_chunk = rows // CMP_STRIDE
    pps = PAGES_PER_STEP
    assert n_pages % pps == 0
    full = lambda a: pl.BlockSpec(a.shape, lambda i, p, pt: (0,) * a.ndim)
    k_page = lambda j: pl.BlockSpec((None, None, None, N_HEADS, HEAD_DIM, PAGE_SIZE),
                                    lambda i, p, pt: (layer, pt[i, p * pps + j], 0, 0, 0, 0))
    nsa_page = lambda t, j: pl.BlockSpec((None, None, None, NSA_KV_HEADS, HEAD_DIM, PAGE_SIZE),
                                         lambda i, p, pt: (layer, pt[i, p * pps + j], t, 0, 0, 0))
    return pl.pallas_call(
        functools.partial(_past_stats_body, n_pages=n_pages),
        grid_spec=pltpu.PrefetchScalarGridSpec(
            num_scalar_prefetch=1,
            grid=(bs, n_pages // pps),
            in_specs=[k_page(j) for j in range(pps)] + [nsa_page(0, j) for j in range(pps)]
                     + [nsa_page(1, j) for j in range(pps)]
                     + [pl.BlockSpec((None, N_HEADS, HEAD_DIM, 1), lambda i, p, pt: (i, 0, 0, 0)),
                        full(w1r), full(w1), full(b1), full(w2), full(pe)],
            out_specs=[pl.BlockSpec((None, 8, N_HEADS), lambda i, p, pt: (i, 0, 0)),
                       pl.BlockSpec((None, 2 * NSA_KV_HEADS, n_chunk, HEAD_DIM), lambda i, p, pt: (i, 0, 0, 0))],
            scratch_shapes=[pltpu.VMEM((N_HEADS, HEAD_DIM, PAGE_SIZE), F32), pltpu.VMEM((nb, N_HEADS, PAGE_SIZE), F32),
                            pltpu.VMEM((rows, LANE), F32), pltpu.VMEM((rows, LANE), F32)]),
        out_shape=[jax.ShapeDtypeStruct((bs, 8, N_HEADS), I32),
                   jax.ShapeDtypeStruct((bs, 2 * NSA_KV_HEADS, n_chunk, HEAD_DIM), F32)],
        compiler_params=_cparams(("parallel", "arbitrary")),
        name="past_stats",
    )(page_table, *([moba_t] * pps), *([nsa_t] * (2 * pps)), q_col, w1r, w1, b1, w2, pe)


def _moba_sample_body(pt_ref, top_ref, *refs):
    n_pg = (len(refs) - 4) // 2
    kt_refs, vt_refs = refs[:n_pg], refs[n_pg:2 * n_pg]
    q_ref, kn_ref, vn_ref, o_ref = refs[2 * n_pg:]
    q = q_ref[...] * ATTN_SCALE
    q8 = jnp.broadcast_to(q, (8, HEAD_DIM)).astype(BF16)
    s_all = [_dot(q8, kt[...].astype(BF16)) for kt in kt_refs]
    s_self = jnp.sum(q * kn_ref[...], axis=-1, keepdims=True)
    m = s_self
    for s in s_all:
        m = jnp.maximum(m, jnp.max(s, axis=-1, keepdims=True))
    e_self = jnp.exp(s_self - m)
    den = e_self
    acc = e_self * vn_ref[...]
    for s, vt in zip(s_all, vt_refs):
        pr = jnp.exp(s - m)
        den = den + jnp.sum(pr, axis=-1, keepdims=True)
        acc = acc + _dot_nt(pr.astype(BF16), vt[...].astype(BF16))
    o_ref[...] = (acc / den)[0:1, :]


def _moba_sample(page_table, top, moba_t, q, k_new, v_new, layer):
    bs = page_table.shape[0]
    ppb = MOBA_BLOCK // PAGE_SIZE

    def page_spec(kv, j, r):
        return pl.BlockSpec((None, None, None, None, HEAD_DIM, PAGE_SIZE),
                            lambda i, h, pt, tp: (layer, pt[i, tp[i, j, h] * ppb + r], kv, h, 0, 0))

    pages = [(j, r) for j in range(MOBA_TOPK) for r in range(ppb)]
    head = pl.BlockSpec((None, None, 1, HEAD_DIM), lambda i, h, pt, tp: (i, h, 0, 0))
    return pl.pallas_call(
        _moba_sample_body,
        grid_spec=pltpu.PrefetchScalarGridSpec(
            num_scalar_prefetch=2,
            grid=(bs, N_HEADS),
            in_specs=[page_spec(0, j, r) for j, r in pages] + [page_spec(1, j, r) for j, r in pages]
                     + [head, head, head],
            out_specs=head),
        out_shape=jax.ShapeDtypeStruct((bs, N_HEADS, 1, HEAD_DIM), F32),
        compiler_params=_cparams(("parallel", "parallel")),
        name="moba_sample",
    )(page_table, top, *([moba_t] * (2 * len(pages))), q, k_new, v_new)


def _heads_to_rows(row, g):
    parts = [row[:, (g * NSA_GROUP + j) * HEAD_DIM:(g * NSA_GROUP + j + 1) * HEAD_DIM] for j in range(NSA_GROUP)]
    return jnp.concatenate(parts + [jnp.zeros((8 - NSA_GROUP, HEAD_DIM), F32)], axis=0)


def _nsa_sample_sel_body(q_ref, cmp_ref, win_ref, kvn_ref, gate_ref, idx_ref, part_ref, *, past):
    hd = HEAD_DIM
    n_chunk = past // CMP_STRIDE
    n_slc = past // SLC_BLOCK + 1
    n_slc_pad = -(-n_slc // LANE) * LANE
    cur = past // SLC_BLOCK
    qrow = q_ref[...]
    gates = _sigmoid(gate_ref[...])
    kvn = kvn_ref[...]
    rows8 = lax.broadcasted_iota(I32, (8, 1), 0)
    ids = lax.broadcasted_iota(I32, (1, n_slc_pad), 1)
    m_iota = lax.broadcasted_iota(I32, (n_slc_pad, n_slc_pad), 0)
    j_iota = lax.broadcasted_iota(I32, (n_slc_pad, n_slc_pad), 1)
    lane = lax.broadcasted_iota(I32, (1, LANE), 1)
    idx_out = jnp.zeros((8, LANE), I32)
    for g in range(NSA_KV_HEADS):
        q8 = _heads_to_rows(qrow, g)
        q8b = q8.astype(BF16)
        n_ids = lax.broadcasted_iota(I32, (8, n_chunk), 1)
        l_cmp = _dot_nt(q8b, cmp_ref[g].astype(BF16)) * ATTN_SCALE
        p_cmp = _softmax_rows(l_cmp, n_ids * CMP_STRIDE + (CMP_LEN - 1) <= past)
        o_cmp = _dot(p_cmp.astype(BF16), cmp_ref[NSA_KV_HEADS + g].astype(BF16))
        imp = jnp.sum(jnp.where(rows8 < NSA_GROUP, p_cmp, 0.0), axis=0, keepdims=True)
        imp_slc = _dot(jnp.broadcast_to(imp, (8, n_chunk)), _slc_matrix(n_chunk, n_chunk - 1, n_slc_pad), HI)
        eligible = (ids <= cur) & (ids < n_slc)
        forced = (ids == 0) | (ids == cur) | (ids == cur - 1)
        score = jnp.where(eligible, jnp.where(forced, jnp.inf, imp_slc[0:1, :]), -jnp.inf)
        s_col = jnp.broadcast_to(score, (LANE, n_slc_pad)).T[:, 0:1]
        beats = (s_col > score) | ((s_col == score) & (m_iota < j_iota))
        rank = jnp.sum(beats.astype(F32), axis=0, keepdims=True)
        sel = jnp.where(eligible & (rank < SLC_TOPK), 1.0, 0.0)
        sel_col = jnp.broadcast_to(sel, (LANE, n_slc_pad)).T[:, 0:1]
        before = jnp.sum(jnp.where(m_iota < j_iota, sel_col, 0.0), axis=0, keepdims=True)
        idx_row = jnp.zeros((1, LANE), I32)
        for i in range(SLC_TOPK):
            hit = (sel > 0.5) & (before == i)
            idx_i = jnp.sum(jnp.where(hit, ids, 0), axis=-1, keepdims=True)
            idx_row = jnp.where(lane == i, idx_i, idx_row)
        idx_out = jnp.where(lax.broadcasted_iota(I32, (8, LANE), 0) == g, idx_row, idx_out)
        kw = win_ref[0, g].astype(BF16)
        vw = win_ref[1, g].astype(BF16)
        n_buf = win_ref.shape[-1]
        w_ids = lax.broadcasted_iota(I32, (8, n_buf), 1)
        l_win = _dot(q8b, kw) * ATTN_SCALE
        w_mask = w_ids >= n_buf - (WINDOW - 1)
        kw_new = kvn[:, 4 * NSA_KV_WIDTH + g * hd:4 * NSA_KV_WIDTH + (g + 1) * hd]
        vw_new = kvn[:, 5 * NSA_KV_WIDTH + g * hd:5 * NSA_KV_WIDTH + (g + 1) * hd]
        s_self = jnp.sum(q8 * kw_new, axis=-1, keepdims=True) * ATTN_SCALE
        m = jnp.maximum(jnp.max(jnp.where(w_mask, l_win, NEG_INF), axis=-1, keepdims=True), s_self)
        e = jnp.where(w_mask, jnp.exp(l_win - m), 0.0)
        e_self = jnp.exp(s_self - m)
        o_win = (_dot_nt(e.astype(BF16), vw) + e_self * vw_new) / (jnp.sum(e, axis=-1, keepdims=True) + e_self)
        for j in range(NSA_GROUP):
            h = g * NSA_GROUP + j
            part_ref[:, h * hd:(h + 1) * hd] = (gates[:, h:h + 1] * o_cmp[j:j + 1]
                                                + gates[:, 2 * N_HEADS + h:2 * N_HEADS + h + 1] * o_win[j:j + 1])
    idx_ref[...] = idx_out


def _nsa_sample_sel(q, cmp_tok, win_state, kv_new, gates, layer, past):
    bs = q.shape[0]
    n_buf = win_state.shape[-1]
    n_chunk = cmp_tok.shape[2]
    return pl.pallas_call(
        functools.partial(_nsa_sample_sel_body, past=past),
        grid=(bs,),
        in_specs=[pl.BlockSpec((None, 1, GROUP_WIDTH), lambda i: (i, 0, 0)),
                  pl.BlockSpec((None, 2 * NSA_KV_HEADS, n_chunk, HEAD_DIM), lambda i: (i, 0, 0, 0)),
                  pl.BlockSpec((None, None, 2, NSA_KV_HEADS, HEAD_DIM, n_buf), lambda i: (layer, i, 0, 0, 0, 0)),
                  pl.BlockSpec((None, 1, 6 * NSA_KV_WIDTH), lambda i: (i, 0, 0)),
                  pl.BlockSpec((None, 1, LANE), lambda i: (i, 0, 0))],
        out_specs=[pl.BlockSpec((None, 8, LANE), lambda i: (i, 0, 0)),
                   pl.BlockSpec((None, 1, GROUP_WIDTH), lambda i: (i, 0, 0))],
        out_shape=[jax.ShapeDtypeStruct((bs, 8, LANE), I32), jax.ShapeDtypeStruct((bs, 1, GROUP_WIDTH), F32)],
        compiler_params=_cparams(("parallel",)),
        name="nsa_sample_sel",
    )(q, cmp_tok, win_state, kv_new, gates)


NSA_BLOCKS_PER_STEP = 4


def _nsa_sample_attn_body(pt_ref, idx_ref, *refs, n_past_blk):
    bps = NSA_BLOCKS_PER_STEP
    n_pg = NSA_KV_HEADS * bps
    ks_refs, vs_refs = refs[:n_pg], refs[n_pg:2 * n_pg]
    q_ref, kvn_ref, gate_ref, part_ref, o_ref, m_s, l_s, acc_s = refs[2 * n_pg:]
    b, i = pl.program_id(0), pl.program_id(1)
    hd = HEAD_DIM
    qrow = q_ref[...]
    kvn = kvn_ref[...]
    half = lax.broadcasted_iota(I32, (8, PAGE_SIZE), 1) // SLC_BLOCK
    for g in range(NSA_KV_HEADS):
        q8 = _heads_to_rows(qrow, g)
        ks_new = kvn[:, 2 * NSA_KV_WIDTH + g * hd:2 * NSA_KV_WIDTH + (g + 1) * hd]
        vs_new = kvn[:, 3 * NSA_KV_WIDTH + g * hd:3 * NSA_KV_WIDTH + (g + 1) * hd]

        @pl.when(i == 0)
        def _():
            m_s[g] = jnp.broadcast_to(jnp.sum(q8 * ks_new, axis=-1, keepdims=True) * ATTN_SCALE, (8, hd))
            l_s[g] = jnp.ones((8, hd), F32)
            acc_s[g] = jnp.broadcast_to(vs_new, (8, hd))

        q8b = (q8 * ATTN_SCALE).astype(BF16)
        m_i = m_s[g][:, 0:1]
        l_i = l_s[g][:, 0:1]
        acc = acc_s[g]
        for u in range(bps):
            blk = idx_ref[b, g, i * bps + u]
            mask = half == jnp.where(blk < n_past_blk, blk % (PAGE_SIZE // SLC_BLOCK), -1)
            s = jnp.where(mask, _dot(q8b, ks_refs[g * bps + u][...].astype(BF16)), NEG_INF)
            m_new = jnp.maximum(m_i, jnp.max(s, axis=-1, keepdims=True))
            pr = jnp.where(mask, jnp.exp(s - m_new), 0.0)
            alpha = jnp.exp(m_i - m_new)
            l_i = alpha * l_i + jnp.sum(pr, axis=-1, keepdims=True)
            acc = alpha * acc + _dot_nt(pr.astype(BF16), vs_refs[g * bps + u][...].astype(BF16))
            m_i = m_new
        l_s[g] = jnp.broadcast_to(l_i, (8, hd))
        acc_s[g] = acc
        m_s[g] = jnp.broadcast_to(m_i, (8, hd))

    @pl.when(i == pl.num_programs(1) - 1)
    def _():
        gates = _sigmoid(gate_ref[...])
        for g in range(NSA_KV_HEADS):
            o_sel = acc_s[g] / l_s[g]
            for j in range(NSA_GROUP):
                h = g * NSA_GROUP + j
                sl = slice(h * hd, (h + 1) * hd)
                o_ref[:, sl] = (part_ref[:, sl] + gates[:, N_HEADS + h:N_HEADS + h + 1] * o_sel[j:j + 1]
                                ).astype(o_ref.dtype)


def _nsa_sample_attn(page_table, sel_idx, nsa_cache, q, kv_new, gates, part, layer, past):
    bs = page_table.shape[0]
    n_past_blk = past // SLC_BLOCK
    per_page = PAGE_SIZE // SLC_BLOCK

    bps = NSA_BLOCKS_PER_STEP

    def blk_spec(g, u, t):
        def imap(b, i, pt, ix):
            blk = jnp.minimum(ix[b, g, i * bps + u], n_past_blk - 1)
            return (layer, pt[b, blk // per_page], t, g, 0, 0)
        return pl.BlockSpec((None, None, None, None, HEAD_DIM, PAGE_SIZE), imap)

    row = lambda w: pl.BlockSpec((None, 1, w), lambda b, i, pt, ix: (b, 0, 0))
    gu = [(g, u) for g in range(NSA_KV_HEADS) for u in range(bps)]
    return pl.pallas_call(
        functools.partial(_nsa_sample_attn_body, n_past_blk=n_past_blk),
        grid_spec=pltpu.PrefetchScalarGridSpec(
            num_scalar_prefetch=2,
            grid=(bs, SLC_TOPK // bps),
            in_specs=[blk_spec(g, u, 2) for g, u in gu] + [blk_spec(g, u, 3) for g, u in gu]
                     + [row(GROUP_WIDTH), row(6 * NSA_KV_WIDTH), row(LANE), row(GROUP_WIDTH)],
            out_specs=row(GROUP_WIDTH),
            scratch_shapes=[pltpu.VMEM((NSA_KV_HEADS, 8, HEAD_DIM), F32)] * 3),
        out_shape=jax.ShapeDtypeStruct((bs, 1, GROUP_WIDTH), F32),
        compiler_params=_cparams(("parallel", "arbitrary")),
        name="nsa_sample_attn",
    )(page_table, sel_idx, *([nsa_cache] * (2 * len(gu))), q, kv_new, gates, part)


def _recur_sample_body(c_ref, shift_ref, mu_ref, w0_ref, w2_ref, a0_ref, a2_ref, g2_ref, kk_ref, ka_ref, rk_ref,
                       lng_ref, lnb_ref, s_rw_ref, ret_ref, cos_ref, sin_ref, gam_ref, gn_g_ref, gn_b_ref, s_rt_ref,
                       o_rw_ref, s_rw_out, o_rt_ref, s_rt_out, o_scr):
    gw = GROUP_WIDTH
    hd = HEAD_DIM
    p = (mu_ref[...], w0_ref[...], w2_ref[...], a0_ref[...], a2_ref[...], g2_ref[...], kk_ref[...], ka_ref[...])
    r, w, k, v, kk, kka, gate = _rwkv_prep(c_ref[...], shift_ref[...], p)

    def put_state(h, j, s):
        s_rw_out[j, h] = s

    ys = _rwkv_steps((r, w, k, v, kk, kka), lambda h, j: s_rw_ref[j, h], put_state, 1)
    o_rw_ref[...] = _rwkv_post(ys, r, k, v, gate, rk_ref[...], lng_ref[...], lnb_ref[...]).astype(o_rw_ref.dtype)

    cos = cos_ref[...]
    sin = sin_ref[...]
    q = _rope(ret_ref[:, 0:gw], cos, sin)
    kr = _rope(ret_ref[:, gw:2 * gw], cos, sin) * ATTN_SCALE
    vr = ret_ref[:, 2 * gw:3 * gw]
    gam = gam_ref[...]
    qk = _dot(q * kr, _head_blockdiag(gw), HI)
    pad = jnp.zeros((LANE - 8, LANE), F32)
    for hp in range(N_HEADS // 2):
        ps = slice(hp * LANE, (hp + 1) * LANE)
        qt = jnp.concatenate([q[:, ps], pad], axis=0).T
        kt = jnp.concatenate([kr[:, ps], pad], axis=0).T
        for h2 in range(2):
            h = 2 * hp + h2
            sl = slice(h * hd, (h + 1) * hd)
            for j in range(8):
                s = s_rt_ref[j, h]
                qcol = qt[h2 * hd:(h2 + 1) * hd, j:j + 1]
                kcol = kt[h2 * hd:(h2 + 1) * hd, j:j + 1]
                g_h = gam[:, sl]
                o_scr[j:j + 1, sl] = (qk[j:j + 1, sl] * vr[j:j + 1, sl]
                                      + g_h * jnp.sum(qcol * s, axis=0, keepdims=True))
                s_rt_out[j, h] = s * g_h + kcol * vr[j:j + 1, sl]
    gt = ret_ref[:, 3 * gw:4 * gw]
    y = _head_norm(o_scr[...], gn_g_ref[...], gn_b_ref[...], GN_EPS)
    o_rt_ref[...] = (gt * _sigmoid(gt) * y).astype(o_rt_ref.dtype)


def _recur_sample(c_rwkv, shift, rwkv_params, s_rwkv, c_ret, rope, gamma, gn_g, gn_b, s_ret):
    bs = c_rwkv.shape[0]
    cos, sin = rope
    st = jax.ShapeDtypeStruct((bs, N_HEADS, HEAD_DIM, HEAD_DIM), F32)
    ob = jax.ShapeDtypeStruct((bs, GROUP_WIDTH), BF16)
    return pl.pallas_call(
        _recur_sample_body,
        out_shape=[ob, st, ob, st],
        scratch_shapes=[pltpu.VMEM((bs, GROUP_WIDTH), F32)],
        compiler_params=pltpu.CompilerParams(vmem_limit_bytes=VMEM_LIMIT),
        name="recur_sample",
    )(c_rwkv, shift, *rwkv_params, s_rwkv, c_ret, cos, sin, gamma, gn_g, gn_b, s_ret)


def _prep_rwkv_params(mu, w0, w2, a0, a2, g2, k_k, k_a, r_k, ln_g, ln_b):
    row = lambda z: z.reshape(1, -1)
    return (row(mu), row(w0), w2, row(a0), a2, g2, row(k_k), row(k_a), row(r_k), row(ln_g), row(ln_b))


def _prep_cmp_weights(w1, b1, w2, pe):
    span = CMP_LEN // CMP_STRIDE
    w1r = w1.reshape(2, span, CMP_STRIDE, HEAD_DIM, CMP_HIDDEN)
    w1r = jnp.transpose(w1r, (0, 2, 3, 1, 4)).reshape(2, CMP_STRIDE, HEAD_DIM, span * CMP_HIDDEN).astype(BF16)
    return (w1r, w1, b1.reshape(2, 1, CMP_HIDDEN), w2, pe.reshape(2, 1, CMP_LEN * HEAD_DIM))


def _pad_w_in(w):
    o_nsa = MOBA_COLS
    o_gate = o_nsa + GROUP_WIDTH + 6 * NSA_KV_WIDTH
    o_rwkv = o_nsa + NSA_COLS
    o_ret = o_rwkv + RWKV_COLS
    zeros = jnp.zeros((w.shape[0], N_PAD - OFF_GATE - 3 * N_HEADS), w.dtype)
    return jnp.concatenate([w[:, o_rwkv:o_ret], w[:, :o_gate], w[:, o_ret:], w[:, o_gate:o_rwkv], zeros], axis=1)


def kernel(x_prompt, x_sample, cache_moba_kv, cache_nsa_kv, state_nsa_win, state_rwkv, state_rwkv_shift, state_ret,
           page_table, norm_g, w_in, w_out, w_up, w_down, nsa_cmp_pe, nsa_cmp_w1, nsa_cmp_b1, nsa_cmp_w2, rwkv_mu,
           rwkv_w0, rwkv_w2, rwkv_a0, rwkv_a2, rwkv_g2, rwkv_k_k, rwkv_k_a, rwkv_r_k, rwkv_ln_g, rwkv_ln_b, ret_gn_g,
           ret_gn_b):
    bp, t, d = x_prompt.shape
    bs = x_sample.shape[0]
    assert x_sample.shape[1] == 1 and d == D_MODEL
    depth = w_in.shape[0]
    past = page_table.shape[1] * PAGE_SIZE
    gw = GROUP_WIDTH
    hd = HEAD_DIM

    log_gamma = jnp.log(1.0 - jnp.exp2(-5.0 - jnp.arange(N_HEADS, dtype=F32)))
    ret_tables = _ret_tables(log_gamma, RET_CHUNK)
    rope_p = _rope_tables(jnp.arange(t, dtype=I32))
    rope_s = _rope_tables(jnp.full((1,), past, I32))
    gamma_row = jnp.repeat(jnp.exp(log_gamma), hd)[None, :]
    moba_t = jnp.transpose(cache_moba_kv, (0, 1, 3, 4, 5, 2))
    nsa_t = jnp.transpose(cache_nsa_kv, (0, 1, 3, 4, 5, 2))
    win_t = jnp.transpose(state_nsa_win, (0, 1, 3, 4, 5, 2))

    xp = x_prompt.reshape(bp * t, d)
    xs = x_sample.reshape(bs, d)
    zero_state = jnp.zeros((bp, N_HEADS, hd, hd), F32)
    zero_shift = jnp.zeros((bp, RWKV_COLS), F32)
    st_p, st_s = [], []
    for l in range(depth):
        g = norm_g[l].reshape(4, 1, d)
        w_in_l = _pad_w_in(w_in[l]).astype(BF16)
        w_out_l = w_out[l].astype(BF16)
        w_up_l = w_up[l].astype(BF16)
        w_down_l = w_down[l].astype(BF16)
        cmp_w = _prep_cmp_weights(nsa_cmp_w1[l], nsa_cmp_b1[l], nsa_cmp_w2[l], nsa_cmp_pe[l])
        rwkv_p = _prep_rwkv_params(rwkv_mu[l], rwkv_w0[l], rwkv_w2[l], rwkv_a0[l], rwkv_a2[l], rwkv_g2[l],
                                   rwkv_k_k[l], rwkv_k_a[l], rwkv_r_k[l], rwkv_ln_g[l], rwkv_ln_b[l])
        gn_g = ret_gn_g[l].reshape(1, gw)
        gn_b = ret_gn_b[l].reshape(1, gw)

        cols = _inproj(xp, g[0], w_in_l, min(1024, bp * t), 512)
        o_moba = _moba_prompt(cols, bp, t)
        o_nsa = _nsa_prompt(cols, cmp_w, bp, t)
        o_rwkv, rwkv_s = _rwkv_prompt(cols, zero_shift, zero_state, rwkv_p, bp, t)
        o_ret, ret_s = _ret_prompt(cols, zero_state, ret_tables, rope_p, gn_g, gn_b, bp, t)
        xp = _outproj((o_moba, o_nsa, o_rwkv, o_ret), xp, g[1], w_out_l, 256)
        xp = _ffn(xp, g[2], g[3], w_up_l, w_down_l, min(512, bp * t), 1024)
        c3 = cols.reshape(bp, t, N_PAD)
        win_keep = min(WINDOW, t)
        st_p.append((c3[:, :, OFF_MOBA + gw:OFF_MOBA + 3 * gw].reshape(bp, t, 2, N_HEADS, hd),
                     c3[:, :, OFF_NSAKV:OFF_NSAKV + 4 * NSA_KV_WIDTH].reshape(bp, t, 4, NSA_KV_HEADS, hd),
                     c3[:, t - win_keep:, OFF_NSAKV + 4 * NSA_KV_WIDTH:OFF_NSAKV + 6 * NSA_KV_WIDTH
                        ].reshape(bp, win_keep, 2, NSA_KV_HEADS, hd),
                     rwkv_s, c3[:, t - 1, OFF_RWKV:OFF_RWKV + RWKV_COLS], ret_s))

        cs = _inproj(xs, g[0], w_in_l, bs, 512)
        heads = lambda z: z.reshape(bs, N_HEADS, 1, hd)
        q_m = cs[:, OFF_MOBA:OFF_MOBA + gw]
        k_m = cs[:, OFF_MOBA + gw:OFF_MOBA + 2 * gw]
        v_m = cs[:, OFF_MOBA + 2 * gw:OFF_MOBA + 3 * gw]
        top, cmp_tok = _past_stats(page_table, moba_t, nsa_t, q_m.reshape(bs, N_HEADS, hd, 1), cmp_w, l)
        o_moba_s = _moba_sample(page_table, top[:, :MOBA_TOPK, :], moba_t, heads(q_m), heads(k_m), heads(v_m), l)
        q_n = cs[:, OFF_NSAQ:OFF_NSAQ + gw].reshape(bs, 1, gw)
        kv_new = cs[:, OFF_NSAKV:OFF_NSAKV + 6 * NSA_KV_WIDTH].reshape(bs, 1, 6 * NSA_KV_WIDTH)
        gates = cs[:, OFF_GATE:OFF_GATE + LANE].reshape(bs, 1, LANE)
        sel_idx, part = _nsa_sample_sel(q_n, cmp_tok, win_t, kv_new, gates, l, past)
        o_nsa_s = _nsa_sample_attn(page_table, sel_idx[:, :NSA_KV_HEADS, :SLC_TOPK], nsa_t, q_n, kv_new, gates, part,
                                   l, past)
        o_rwkv_s, rwkv_s_s, o_ret_s, ret_s_s = _recur_sample(
            cs[:, OFF_RWKV:OFF_RWKV + RWKV_COLS], state_rwkv_shift[l], rwkv_p, state_rwkv[l],
            cs[:, OFF_RET:OFF_RET + RET_COLS], rope_s, gamma_row, gn_g, gn_b, state_ret[l])
        parts_s = (o_moba_s.reshape(bs, gw).astype(BF16), o_nsa_s.reshape(bs, gw).astype(BF16), o_rwkv_s, o_ret_s)
        xs = _outproj(parts_s, xs, g[1], w_out_l, bs)
        xs = _ffn(xs, g[2], g[3], w_up_l, w_down_l, bs, 512)
        win_new = cs[:, OFF_NSAKV + 4 * NSA_KV_WIDTH:OFF_NSAKV + 6 * NSA_KV_WIDTH].reshape(bs, 1, 2, NSA_KV_HEADS, hd)
        win_all = jnp.concatenate([state_nsa_win[l], win_new], axis=1)
        keep_s = min(WINDOW, win_all.shape[1])
        st_s.append((cs[:, OFF_MOBA + gw:OFF_MOBA + 3 * gw].reshape(bs, 1, 2, N_HEADS, hd),
                     cs[:, OFF_NSAKV:OFF_NSAKV + 4 * NSA_KV_WIDTH].reshape(bs, 1, 4, NSA_KV_HEADS, hd),
                     win_all[:, win_all.shape[1] - keep_s:],
                     rwkv_s_s, cs[:, OFF_RWKV:OFF_RWKV + RWKV_COLS], ret_s_s))

    stk = lambda sts, i: jnp.stack([s[i] for s in sts], axis=0)
    outs = [xp.reshape(bp, t, d), xs.reshape(bs, 1, d)]
    for i in range(6):
        outs += [stk(st_p, i), stk(st_s, i)]
    return tuple(outs)
```

```python
import functools

import jax
import jax.numpy as jnp
import numpy as np
from jax import lax
from jax.experimental import pallas as pl
from jax.experimental.pallas import tpu as pltpu

F32 = jnp.float32
BF16 = jnp.bfloat16
I32 = jnp.int32
HI = lax.Precision.HIGHEST

D_MODEL = 2048
HEAD_DIM = 64
GROUP_WIDTH = D_MODEL // 4
N_HEADS = GROUP_WIDTH // HEAD_DIM
D_FF = 4 * D_MODEL
RMS_EPS = 1e-6
GN_EPS = 1e-5
NEG_INF = -1e30
ATTN_SCALE = HEAD_DIM ** -0.5
PAGE_SIZE = 128
MOBA_BLOCK = 256
MOBA_TOPK = 3
NSA_KV_HEADS = 2
NSA_GROUP = N_HEADS // NSA_KV_HEADS
NSA_KV_WIDTH = NSA_KV_HEADS * HEAD_DIM
CMP_LEN = 32
CMP_STRIDE = 16
CMP_HIDDEN = 2 * HEAD_DIM
SLC_BLOCK = 64
SLC_TOPK = 16
WINDOW = 512
RWKV_DECAY_RANK = 64
RWKV_AAA_RANK = 64
RWKV_GATE_RANK = 128
RWKV_LN_EPS = 64e-5
RET_CHUNK = 128
ROPE_BASE = 10000.0

MOBA_COLS = 3 * GROUP_WIDTH
NSA_COLS = GROUP_WIDTH + 6 * NSA_KV_WIDTH + 3 * N_HEADS
RWKV_COLS = 3 * GROUP_WIDTH + RWKV_DECAY_RANK + RWKV_AAA_RANK + RWKV_GATE_RANK
RET_COLS = 4 * GROUP_WIDTH
IN_COLS = MOBA_COLS + NSA_COLS + RWKV_COLS + RET_COLS

OFF_RWKV = 0
OFF_MOBA = OFF_RWKV + RWKV_COLS
OFF_NSAQ = OFF_MOBA + MOBA_COLS
OFF_NSAKV = OFF_NSAQ + GROUP_WIDTH
OFF_RET = OFF_NSAKV + 6 * NSA_KV_WIDTH
OFF_GATE = OFF_RET + RET_COLS
N_PAD = OFF_GATE + 512

LANE = 128
VMEM_LIMIT = 56 * 1024 * 1024


def _cparams(sem):
    return pltpu.CompilerParams(dimension_semantics=sem, vmem_limit_bytes=VMEM_LIMIT)


def _dot(a, b, precision=None):
    return jnp.dot(a, b, preferred_element_type=F32, precision=precision)


def _dot_nt(a, b, precision=None):
    return lax.dot_general(a, b, (((1,), (1,)), ((), ())), preferred_element_type=F32, precision=precision)


def _dot_tn(a, b, precision=None):
    return lax.dot_general(a, b, (((0,), (0,)), ((), ())), preferred_element_type=F32, precision=precision)


def _head_blockdiag(n, scale=1.0):
    r = lax.broadcasted_iota(I32, (n, n), 0) // HEAD_DIM
    c = lax.broadcasted_iota(I32, (n, n), 1) // HEAD_DIM
    return jnp.where(r == c, scale, 0.0).astype(F32)


def _sigmoid(x):
    return 1.0 / (1.0 + jnp.exp(-x))


def _inproj_body(x_ref, g_ref, w_ref, o_ref, h_scr):
    @pl.when(pl.program_id(1) == 0)
    def _():
        x = x_ref[...]
        ms = jnp.mean(x * x, axis=-1, keepdims=True)
        h_scr[...] = (x * lax.rsqrt(ms + RMS_EPS) * g_ref[...]).astype(BF16)

    o_ref[...] = _dot(h_scr[...], w_ref[...])


def _inproj(x, g, w, tm, tn):
    m, d = x.shape
    n = w.shape[1]
    return pl.pallas_call(
        _inproj_body,
        grid=(m // tm, n // tn),
        in_specs=[pl.BlockSpec((tm, d), lambda i, j: (i, 0)),
                  pl.BlockSpec((1, d), lambda i, j: (0, 0)),
                  pl.BlockSpec((d, tn), lambda i, j: (0, j))],
        out_specs=pl.BlockSpec((tm, tn), lambda i, j: (i, j)),
        out_shape=jax.ShapeDtypeStruct((m, n), F32),
        scratch_shapes=[pltpu.VMEM((tm, d), BF16)],
        compiler_params=_cparams(("parallel", "arbitrary")),
        name="inproj",
    )(x, g, w)


def _outproj_body(a_ref, b_ref, c_ref, d_ref, x_ref, g_ref, w_ref, o_ref):
    gw = GROUP_WIDTH
    y = _dot(a_ref[...], w_ref[0:gw, :])
    y += _dot(b_ref[...], w_ref[gw:2 * gw, :])
    y += _dot(c_ref[...], w_ref[2 * gw:3 * gw, :])
    y += _dot(d_ref[...], w_ref[3 * gw:4 * gw, :])
    ms = jnp.mean(y * y, axis=-1, keepdims=True)
    o_ref[...] = x_ref[...] + y * lax.rsqrt(ms + RMS_EPS) * g_ref[...]


def _outproj(parts, x, g, w, tm):
    m, d = x.shape
    gw = GROUP_WIDTH
    part_spec = pl.BlockSpec((tm, gw), lambda i: (i, 0))
    return pl.pallas_call(
        _outproj_body,
        grid=(m // tm,),
        in_specs=[part_spec, part_spec, part_spec, part_spec,
                  pl.BlockSpec((tm, d), lambda i: (i, 0)),
                  pl.BlockSpec((1, d), lambda i: (0, 0)),
                  pl.BlockSpec((4 * gw, d), lambda i: (0, 0))],
        out_specs=pl.BlockSpec((tm, d), lambda i: (i, 0)),
        out_shape=jax.ShapeDtypeStruct((m, d), F32),
        compiler_params=_cparams(("parallel",)),
        name="outproj",
    )(*parts, x, g, w)


def _ffn_body(x_ref, g2_ref, g3_ref, wu_ref, wd_ref, o_ref, h_scr, acc_scr):
    f = pl.program_id(1)

    @pl.when(f == 0)
    def _():
        x = x_ref[...]
        ms = jnp.mean(x * x, axis=-1, keepdims=True)
        h_scr[...] = (x * lax.rsqrt(ms + RMS_EPS) * g2_ref[...]).astype(BF16)
        acc_scr[...] = jnp.zeros_like(acc_scr)

    u = jnp.maximum(_dot(h_scr[...], wu_ref[...]), 0.0)
    acc_scr[...] += _dot((u * u).astype(BF16), wd_ref[...])

    @pl.when(f == pl.num_programs(1) - 1)
    def _():
        y = acc_scr[...]
        ms = jnp.mean(y * y, axis=-1, keepdims=True)
        o_ref[...] = x_ref[...] + y * lax.rsqrt(ms + RMS_EPS) * g3_ref[...]


def _ffn(x, g2, g3, wu, wd, tm, tf):
    m, d = x.shape
    f = wu.shape[1]
    return pl.pallas_call(
        _ffn_body,
        grid=(m // tm, f // tf),
        in_specs=[pl.BlockSpec((tm, d), lambda i, j: (i, 0)),
                  pl.BlockSpec((1, d), lambda i, j: (0, 0)),
                  pl.BlockSpec((1, d), lambda i, j: (0, 0)),
                  pl.BlockSpec((d, tf), lambda i, j: (0, j)),
                  pl.BlockSpec((tf, d), lambda i, j: (j, 0))],
        out_specs=pl.BlockSpec((tm, d), lambda i, j: (i, 0)),
        out_shape=jax.ShapeDtypeStruct((m, d), F32),
        scratch_shapes=[pltpu.VMEM((tm, d), BF16), pltpu.VMEM((tm, d), F32)],
        compiler_params=_cparams(("parallel", "arbitrary")),
        name="ffn",
    )(x, g2, g3, wu, wd)


MOBA_HEAD_GROUP = 4


def _moba_prompt_body(q_ref, k_ref, v_ref, o_ref, kmean_scr, kext, vext, *, nb):
    blk = MOBA_BLOCK
    hg = MOBA_HEAD_GROUP
    qt = pl.program_id(2)
    pair_of = lambda j: slice(LANE * (j // 2), LANE * (j // 2 + 1))
    dst_of = lambda j: slice(LANE * j, LANE * (j + 1))

    @pl.when(qt == 0)
    def _():
        kmean_scr[...] = jnp.zeros_like(kmean_scr)
        mine_t = lax.broadcasted_iota(I32, (k_ref.shape[0], LANE), 1) // HEAD_DIM
        for j in range(hg):
            kp = jnp.where(mine_t == j % 2, k_ref[:, pair_of(j)], 0.0)
            kext[:, dst_of(j)] = kp.astype(BF16)
            vext[:, dst_of(j)] = jnp.where(mine_t == j % 2, v_ref[:, pair_of(j)], 1.0).astype(BF16)
            for n in range(nb):
                kmean_scr[n:n + 1, dst_of(j)] = jnp.sum(kp[n * blk:(n + 1) * blk], axis=0, keepdims=True) * (1.0 / blk)

    row = lax.broadcasted_iota(I32, (blk, blk), 0)
    col = lax.broadcasted_iota(I32, (blk, blk), 1)
    bias_own = jnp.where(col <= row, 0.0, NEG_INF)
    nb_r = kmean_scr.shape[0]
    blk_id = lax.broadcasted_iota(I32, (nb_r, blk), 0)
    eligible = blk_id < qt
    not_sel, qb = [], []
    for j in range(hg):
        q = q_ref[:, pair_of(j)]
        gate = jnp.where(eligible, _dot_nt(kmean_scr[:, dst_of(j)], q, HI), -jnp.inf)
        rank = jnp.zeros((nb_r, blk), I32)
        for m in range(nb):
            gm = gate[m:m + 1, :]
            rank += ((gm > gate) | ((gm == gate) & (m < blk_id))).astype(I32)
        ns_t = jnp.where(eligible & (rank < MOBA_TOPK), 0.0, 1.0)
        not_sel.append(jnp.concatenate([ns_t, jnp.ones((LANE - nb_r, blk), F32)], axis=0).T.astype(BF16))
        qb.append((q * ATTN_SCALE).astype(BF16))

    def attend(carry, start, biases):
        out = []
        for j in range(hg):
            m_i, acc = carry[j]
            s = _dot_nt(qb[j], kext[pl.ds(start, blk), dst_of(j)]) + biases[j]
            m_new = jnp.maximum(m_i, jnp.max(s, axis=-1, keepdims=True))
            p = jnp.exp(s - m_new)
            out.append((m_new, jnp.exp(m_i - m_new) * acc + _dot(p.astype(BF16), vext[pl.ds(start, blk), dst_of(j)])))
        return tuple(out)

    def body(n, carry):
        pick = (lax.broadcasted_iota(I32, (LANE, blk), 0) == n).astype(BF16)
        return attend(carry, pl.multiple_of(n * blk, blk), [_dot(ns, pick) * NEG_INF for ns in not_sel])

    init = tuple((jnp.full((blk, 1), NEG_INF, F32), jnp.zeros((blk, LANE), F32)) for _ in range(hg))
    carry = lax.fori_loop(0, qt, body, init)
    fin = attend(carry, pl.multiple_of(qt * blk, blk), [bias_own] * hg)
    half = lax.broadcasted_iota(I32, (blk, LANE), 1) // HEAD_DIM
    for jp in range(hg // 2):
        outs = [fin[2 * jp + e][1] / pltpu.roll(fin[2 * jp + e][1], HEAD_DIM, 1) for e in range(2)]
        o_ref[:, dst_of(jp)] = jnp.where(half == 0, outs[0], outs[1]).astype(o_ref.dtype)


def _moba_prompt(cols, b, t):
    blk = MOBA_BLOCK
    nb = t // blk
    w = MOBA_HEAD_GROUP * HEAD_DIM
    qoff = OFF_MOBA // w
    koff = (OFF_MOBA + GROUP_WIDTH) // w
    voff = (OFF_MOBA + 2 * GROUP_WIDTH) // w
    return pl.pallas_call(
        functools.partial(_moba_prompt_body, nb=nb),
        grid=(b, GROUP_WIDTH // w, nb),
        in_specs=[pl.BlockSpec((blk, w), lambda i, h, q: (i * nb + q, qoff + h)),
                  pl.BlockSpec((t, w), lambda i, h, q: (i, koff + h)),
                  pl.BlockSpec((t, w), lambda i, h, q: (i, voff + h))],
        out_specs=pl.BlockSpec((blk, w), lambda i, h, q: (i * nb + q, h)),
        out_shape=jax.ShapeDtypeStruct((b * t, GROUP_WIDTH), BF16),
        scratch_shapes=[pltpu.VMEM((-(-nb // 8) * 8, MOBA_HEAD_GROUP * LANE), F32),
                        pltpu.VMEM((t, MOBA_HEAD_GROUP * LANE), BF16), pltpu.VMEM((t, MOBA_HEAD_GROUP * LANE), BF16)],
        compiler_params=_cparams(("parallel", "parallel", "arbitrary")),
        name="moba_prompt",
    )(cols, cols, cols)


def _gelu_tanh(x):
    return x * (0.5 * (1.0 + jnp.tanh(np.sqrt(2.0 / np.pi).astype(np.float32) * (x + 0.044715 * (x * x * x)))))


def _cmp_bias(pe_ref, w1_ref, b1_ref, kv):
    pe8 = jnp.broadcast_to(pe_ref[kv], (8, CMP_LEN * HEAD_DIM))
    return _dot(pe8, w1_ref[kv], HI)[0:1, :] + b1_ref[kv]


def _compress(load_rows, n, w1r_ref, kv, bias, w2):
    acc = [jnp.zeros((n, 2 * CMP_HIDDEN), F32) for _ in range(NSA_KV_HEADS)]
    for r in range(CMP_STRIDE):
        x = load_rows(r).astype(BF16)
        for g in range(NSA_KV_HEADS):
            acc[g] += _dot(x[:, g * HEAD_DIM:(g + 1) * HEAD_DIM], w1r_ref[kv, r])
    out = []
    for g in range(NSA_KV_HEADS):
        hid = acc[g][:, :CMP_HIDDEN] + pltpu.roll(acc[g][:, CMP_HIDDEN:], n - 1, 0) + bias
        out.append(_dot(_gelu_tanh(hid).astype(BF16), w2.astype(BF16)))
    return out


def _softmax_rows(l, mask):
    m = jnp.max(jnp.where(mask, l, NEG_INF), axis=-1, keepdims=True)
    e = jnp.where(mask, jnp.exp(l - m), 0.0)
    s = jnp.sum(e, axis=-1, keepdims=True)
    return jnp.where(s > 0.0, e / jnp.where(s > 0.0, s, 1.0), 0.0)


def _slc_matrix(n_cmp_pad, n_cmp, n_slc_pad):
    n = lax.broadcasted_iota(I32, (n_cmp_pad, n_slc_pad), 0)
    j = lax.broadcasted_iota(I32, (n_cmp_pad, n_slc_pad), 1)
    ratio = SLC_BLOCK // CMP_STRIDE
    return ((n >= ratio * j - 1) & (n <= ratio * j + ratio - 1) & (n < n_cmp)).astype(F32)


def _topk_rows(score, ids, n_cand, k):
    rank = jnp.zeros(score.shape, I32)
    for m in range(n_cand):
        sm = score[:, m:m + 1]
        rank += ((sm > score) | ((sm == score) & (m < ids))).astype(I32)
    return rank < k


def _nsa_prompt_body(q0_ref, q1_ref, kc_ref, vc_ref, ksvs_ref, kwvw_ref, gate_ref, w1r_ref, w1_ref, b1_ref, w2_ref,
                     pe_ref, o_ref, ck_scr, cv_scr, ks_ext, vs_ext, kw_ext, vw_ext, *, t):
    tq = 256
    n_chunk = t // CMP_STRIDE
    n_cmp = n_chunk - CMP_LEN // CMP_STRIDE + 1
    qt = pl.program_id(1)
    hd = HEAD_DIM

    @pl.when(qt == 0)
    def _():
        for kv, (src, dst) in enumerate(((kc_ref, ck_scr), (vc_ref, cv_scr))):
            bias = _cmp_bias(pe_ref, w1_ref, b1_ref, kv)
            out = _compress(lambda r: src[pl.ds(r, n_chunk, stride=CMP_STRIDE), :],
                            n_chunk, w1r_ref, kv, bias, w2_ref[kv])
            for g in range(NSA_KV_HEADS):
                dst[g] = out[g]
        half_t = lax.broadcasted_iota(I32, (t, LANE), 1) // hd
        for src, k_ext, v_ext in ((ksvs_ref, ks_ext, vs_ext), (kwvw_ref, kw_ext, vw_ext)):
            for g in range(NSA_KV_HEADS):
                k_ext[g] = jnp.where(half_t == g, src[:, 0:LANE], 0.0).astype(BF16)
                v_ext[g] = jnp.where(half_t == g, src[:, LANE:2 * LANE], 1.0).astype(BF16)

    tpos = qt * tq + lax.broadcasted_iota(I32, (tq, 1), 0)
    tpos4 = jnp.concatenate([tpos] * NSA_GROUP, axis=0)
    gates = _sigmoid(gate_ref[...])
    slc_ids = lax.broadcasted_iota(I32, (tq, LANE), 1)
    n_slc = t // SLC_BLOCK
    kcol = lax.broadcasted_iota(I32, (tq, tq), 1)
    jrow = lax.broadcasted_iota(I32, (LANE, tq), 0)
    kcol_e = lax.broadcasted_iota(I32, (LANE, tq), 1)
    for g in range(NSA_KV_HEADS):
        q_ref = q0_ref if g == 0 else q1_ref
        q4f = jnp.concatenate([q_ref[:, j * hd:(j + 1) * hd] for j in range(NSA_GROUP)], axis=0)
        q4 = q4f.astype(BF16)
        cmp_end = lax.broadcasted_iota(I32, (NSA_GROUP * tq, n_chunk), 1) * CMP_STRIDE + (CMP_LEN - 1)
        l_cmp = _dot_nt(q4, ck_scr[g].astype(BF16)) * ATTN_SCALE
        p_cmp = _softmax_rows(l_cmp, cmp_end <= tpos4)
        o_cmp = _dot(p_cmp.astype(BF16), cv_scr[g].astype(BF16))
        imp = p_cmp[0:tq]
        for j in range(1, NSA_GROUP):
            imp = imp + p_cmp[j * tq:(j + 1) * tq]
        n_slc_r = -(-n_slc // 8) * 8
        jn = lax.broadcasted_iota(I32, (n_slc_r, n_chunk), 0)
        nn = lax.broadcasted_iota(I32, (n_slc_r, n_chunk), 1)
        ratio = SLC_BLOCK // CMP_STRIDE
        slc_t = ((nn >= ratio * jn - 1) & (nn <= ratio * jn + ratio - 1) & (nn < n_cmp)).astype(F32)
        imp_t = _dot_nt(slc_t, imp, HI)
        jt = lax.broadcasted_iota(I32, (n_slc_r, tq), 0)
        cur_t = (qt * tq + lax.broadcasted_iota(I32, (n_slc_r, tq), 1)) // SLC_BLOCK
        eligible = (jt <= cur_t) & (jt < n_slc)
        forced = (jt == 0) | (jt == cur_t) | (jt == cur_t - 1)
        score = jnp.where(eligible, jnp.where(forced, jnp.inf, imp_t), -jnp.inf)
        rank = jnp.zeros((n_slc_r, tq), I32)
        for m in range(n_slc):
            sm = score[m:m + 1, :]
            rank += ((sm > score) | ((sm == score) & (m < jt))).astype(I32)
        sel_t = jnp.where(eligible & (rank < SLC_TOPK), 1.0, 0.0)
        sel = jnp.concatenate([sel_t, jnp.zeros((LANE - n_slc_r, tq), F32)], axis=0).T.astype(BF16)

        half_q = lax.broadcasted_iota(I32, (tq, LANE), 1) // hd
        q4e = []
        for j in range(NSA_GROUP):
            qp = q_ref[:, (j // 2) * LANE:(j // 2 + 1) * LANE]
            qp = qp if j % 2 == g else pltpu.roll(qp, hd, 1)
            q4e.append(jnp.where(half_q == g, qp * ATTN_SCALE, 0.0).astype(BF16))
        q4e = jnp.concatenate(q4e, axis=0)

        def attend(carry, start, bias, k_ext, v_ext):
            m_i, acc = carry
            s = (_dot_nt(q4e, k_ext[g, pl.ds(start, tq), :]).reshape(NSA_GROUP, tq, tq) + bias[None]
                 ).reshape(NSA_GROUP * tq, tq)
            m_new = jnp.maximum(m_i, jnp.max(s, axis=-1, keepdims=True))
            p = jnp.exp(s - m_new)
            return m_new, jnp.exp(m_i - m_new) * acc + _dot(p.astype(BF16), v_ext[g, pl.ds(start, tq), :])

        def sel_body(n, carry):
            expand = (jrow == (tq // SLC_BLOCK) * n + kcol_e // SLC_BLOCK).astype(BF16)
            mask = (_dot(sel, expand) > 0.5) & (n * tq + kcol <= tpos)
            return attend(carry, pl.multiple_of(n * tq, tq), jnp.where(mask, 0.0, NEG_INF), ks_ext, vs_ext)

        def win_body(n, carry):
            dist = tpos - (n * tq + kcol)
            bias = jnp.where((dist >= 0) & (dist < WINDOW), 0.0, NEG_INF)
            return attend(carry, pl.multiple_of(n * tq, tq), bias, kw_ext, vw_ext)

        rows = NSA_GROUP * tq
        init = (jnp.full((rows, 1), NEG_INF, F32), jnp.zeros((rows, LANE), F32))
        _, acc_s = lax.fori_loop(0, qt + 1, sel_body, init)
        _, acc_w = lax.fori_loop(jnp.maximum(qt - (WINDOW // tq), 0), qt + 1, win_body, init)
        o_sel = (acc_s / pltpu.roll(acc_s, hd, 1))[:, g * hd:(g + 1) * hd]
        o_win = (acc_w / pltpu.roll(acc_w, hd, 1))[:, g * hd:(g + 1) * hd]
        for j in range(NSA_GROUP):
            h = g * NSA_GROUP + j
            rs = slice(j * tq, (j + 1) * tq)
            o = (gates[:, h:h + 1] * o_cmp[rs] + gates[:, N_HEADS + h:N_HEADS + h + 1] * o_sel[rs]
                 + gates[:, 2 * N_HEADS + h:2 * N_HEADS + h + 1] * o_win[rs])
            o_ref[:, h * hd:(h + 1) * hd] = o.astype(o_ref.dtype)


def _nsa_prompt(cols, cmp_w, b, t):
    w1r, w1, b1, w2, pe = cmp_w
    tq = 256
    nq = t // tq
    n_chunk = t // CMP_STRIDE
    qoff = OFF_NSAQ // 256
    kvoff = OFF_NSAKV // 256
    full = lambda a: pl.BlockSpec(a.shape, lambda i, q: (0,) * a.ndim)
    return pl.pallas_call(
        functools.partial(_nsa_prompt_body, t=t),
        grid=(b, nq),
        in_specs=[pl.BlockSpec((tq, 256), lambda i, q: (i * nq + q, qoff)),
                  pl.BlockSpec((tq, 256), lambda i, q: (i * nq + q, qoff + 1)),
                  pl.BlockSpec((t, LANE), lambda i, q: (i, 2 * kvoff)),
                  pl.BlockSpec((t, LANE), lambda i, q: (i, 2 * kvoff + 1)),
                  pl.BlockSpec((t, 256), lambda i, q: (i, kvoff + 1)),
                  pl.BlockSpec((t, 256), lambda i, q: (i, kvoff + 2)),
                  pl.BlockSpec((tq, LANE), lambda i, q: (i * nq + q, OFF_GATE // LANE)),
                  full(w1r), full(w1), full(b1), full(w2), full(pe)],
        out_specs=pl.BlockSpec((tq, GROUP_WIDTH), lambda i, q: (i * nq + q, 0)),
        out_shape=jax.ShapeDtypeStruct((b * t, GROUP_WIDTH), BF16),
        scratch_shapes=[pltpu.VMEM((NSA_KV_HEADS, n_chunk, HEAD_DIM), F32),
                        pltpu.VMEM((NSA_KV_HEADS, n_chunk, HEAD_DIM), F32)]
                       + [pltpu.VMEM((NSA_KV_HEADS, t, LANE), BF16)] * 4,
        compiler_params=_cparams(("parallel", "arbitrary")),
        name="nsa_prompt",
    )(cols, cols, cols, cols, cols, cols, cols, w1r, w1, b1, w2, pe)


def _rope(x, cos, sin):
    half = HEAD_DIM // 2
    lane = lax.broadcasted_iota(I32, x.shape, 1) % HEAD_DIM
    nxt = pltpu.roll(x, x.shape[1] - half, 1)
    prv = pltpu.roll(x, half, 1)
    return x * cos + jnp.where(lane < half, -nxt, prv) * sin


def _head_norm(y, g, b, eps):
    avg = _head_blockdiag(y.shape[1], 1.0 / HEAD_DIM)
    mu = _dot(y, avg, HI)
    d = y - mu
    var = _dot(d * d, avg, HI)
    return d * lax.rsqrt(var + eps) * g + b


def _ret_prompt_body(q_ref, k_ref, v_ref, gate_ref, cos_ref, sin_ref, dmask_ref, xi_ref, zeta_ref, cd_ref,
                     gn_g_ref, gn_b_ref, s0_ref, o_ref, s_ref, o_scr):
    @pl.when(pl.program_id(1) == 0)
    def _():
        s_ref[...] = s0_ref[...]

    cos = cos_ref[...]
    sin = sin_ref[...]
    q = _rope(q_ref[...], cos, sin)
    k = _rope(k_ref[...], cos, sin) * ATTN_SCALE
    kz = (k * zeta_ref[...]).astype(BF16)
    qb = q.astype(BF16)
    kb = k.astype(BF16)
    for h in range(N_HEADS):
        sl = slice(h * HEAD_DIM, (h + 1) * HEAD_DIM)
        vb = v_ref[:, sl].astype(BF16)
        s = s_ref[0, h]
        att = _dot_nt(qb[:, sl], kb[:, sl]) * dmask_ref[h]
        o_scr[:, sl] = _dot(att.astype(BF16), vb) + _dot(qb[:, sl], s.astype(BF16)) * xi_ref[:, sl]
        s_ref[0, h] = s * cd_ref[:, sl] + _dot_tn(kz[:, sl], vb)
    gate = gate_ref[...]
    y = _head_norm(o_scr[...], gn_g_ref[...], gn_b_ref[...], GN_EPS)
    o_ref[...] = (gate * _sigmoid(gate) * y).astype(o_ref.dtype)


def _ret_tables(log_gamma, c):
    idx = jnp.arange(c, dtype=F32)
    diff = idx[:, None] - idx[None, :]
    dmask = jnp.where(diff >= 0, jnp.exp(jnp.maximum(diff, 0.0)[None] * log_gamma[:, None, None]), 0.0)
    rep = lambda z: jnp.repeat(z, HEAD_DIM, axis=-1)
    xi = rep(jnp.exp((idx + 1.0)[:, None] * log_gamma[None, :]))
    zeta = rep(jnp.exp((c - 1.0 - idx)[:, None] * log_gamma[None, :]))
    cd = rep(jnp.exp(c * log_gamma)[None, :])
    return dmask, xi, zeta, cd


def _rope_tables(pos):
    half = HEAD_DIM // 2
    inv = ROPE_BASE ** (-jnp.arange(half, dtype=F32) / half)
    ang = pos.astype(F32)[:, None] * inv[None, :]
    tile = lambda z: jnp.tile(z, (1, 2 * N_HEADS))
    return tile(jnp.cos(ang)), tile(jnp.sin(ang))


def _ret_prompt(cols, s0, tables, rope, gn_g, gn_b, b, t):
    c = RET_CHUNK
    nc = t // c
    gw = GROUP_WIDTH
    off = OFF_RET // gw
    dmask, xi, zeta, cd = tables
    cos, sin = rope
    col_spec = lambda j: pl.BlockSpec((c, gw), lambda i, n: (i * nc + n, off + j))
    full = lambda a: pl.BlockSpec(a.shape, lambda i, n: (0,) * a.ndim)
    st_spec = pl.BlockSpec((1, N_HEADS, HEAD_DIM, HEAD_DIM), lambda i, n: (i, 0, 0, 0))
    return pl.pallas_call(
        _ret_prompt_body,
        grid=(b, nc),
        in_specs=[col_spec(0), col_spec(1), col_spec(2), col_spec(3),
                  pl.BlockSpec((c, gw), lambda i, n: (n, 0)), pl.BlockSpec((c, gw), lambda i, n: (n, 0)),
                  full(dmask), full(xi), full(zeta), full(cd), full(gn_g), full(gn_b), st_spec],
        out_specs=[pl.BlockSpec((c, gw), lambda i, n: (i * nc + n, 0)), st_spec],
        out_shape=[jax.ShapeDtypeStruct((b * t, gw), BF16),
                   jax.ShapeDtypeStruct((b, N_HEADS, HEAD_DIM, HEAD_DIM), F32)],
        scratch_shapes=[pltpu.VMEM((c, gw), F32)],
        compiler_params=_cparams(("parallel", "arbitrary")),
        name="ret_prompt",
    )(cols, cols, cols, cols, cos, sin, dmask, xi, zeta, cd, gn_g, gn_b, s0)


def _softplus(x):
    return jnp.maximum(x, 0.0) + jnp.log(1.0 + jnp.exp(-jnp.abs(x)))


def _rwkv_prep(c, prev, p):
    mu, w0, w2, a0, a2, g2, k_k, k_a = p
    gw = GROUP_WIDTH
    mixed = c + (prev - c) * mu
    r, k, v = mixed[:, :gw], mixed[:, gw:2 * gw], mixed[:, 2 * gw:3 * gw]
    o1 = 3 * gw
    o2 = o1 + RWKV_DECAY_RANK
    o3 = o2 + RWKV_AAA_RANK
    xw, xa, xg = mixed[:, o1:o2], mixed[:, o2:o3], mixed[:, o3:]
    w_log = -_softplus(-(w0 + _dot(jnp.tanh(xw), w2, HI))) - 0.5
    decay = jnp.exp(-jnp.exp(w_log))
    a = _sigmoid(a0 + _dot(xa, a2, HI))
    gate = _dot(_sigmoid(xg).astype(BF16), g2.astype(BF16))
    kk = k * k_k
    norm = jnp.sqrt(_dot(kk * kk, _head_blockdiag(gw), HI))
    kk = kk / jnp.maximum(norm, 1e-12)
    k = k * (1.0 + (a - 1.0) * k_a)
    return r, decay, k, v, kk, kk * a, gate


def _rwkv_steps(vecs, get_state, put_state, n_steps):
    r8, w8, k8, v8, kk8, ka8 = vecs
    lane = lax.broadcasted_iota(I32, (HEAD_DIM, LANE), 1)
    pad = jnp.zeros((LANE - 8, LANE), F32)
    ys = []
    for hp in range(N_HEADS // 2):
        vt = jnp.concatenate([v8[:, hp * LANE:(hp + 1) * LANE], pad], axis=0).T
        yts = []
        for h2 in range(2):
            h = 2 * hp + h2
            sl = slice(h * HEAD_DIM, (h + 1) * HEAD_DIM)
            yt = jnp.zeros((HEAD_DIM, LANE), F32)
            s = get_state(h, 0)
            for j in range(8):
                if n_steps == 1:
                    s = get_state(h, j)
                vcol = vt[h2 * HEAD_DIM:(h2 + 1) * HEAD_DIM, j:j + 1]
                sa = jnp.sum(s * kk8[j:j + 1, sl], axis=-1, keepdims=True)
                s = s * w8[j:j + 1, sl] - sa * ka8[j:j + 1, sl] + vcol * k8[j:j + 1, sl]
                ycol = jnp.sum(s * r8[j:j + 1, sl], axis=-1, keepdims=True)
                yt = jnp.where(lane == j, ycol, yt)
                if n_steps == 1:
                    put_state(h, j, s)
            if n_steps != 1:
                put_state(h, 0, s)
            yts.append(yt)
        ys.append(jnp.concatenate(yts, axis=0).T[0:8, :])
    return jnp.concatenate(ys, axis=1)


def _rwkv_post(ys, r, k, v, gate, r_k, ln_g, ln_b):
    y = _head_norm(ys, ln_g, ln_b, RWKV_LN_EPS)
    y = y + _dot(r * k * r_k, _head_blockdiag(GROUP_WIDTH), HI) * v
    return y * gate


def _rwkv_prep_body(c_ref, shift_ref, mu_ref, w0_ref, w2_ref, a0_ref, a2_ref, g2_ref, kk_ref, ka_ref, rk_ref,
                    w_o, kk_o, kka_o, k_o, r_o, v_o, bonus_o, gate_o, carry):
    tc = c_ref.shape[0]

    @pl.when(pl.program_id(1) == 0)
    def _():
        carry[...] = shift_ref[...]

    c = c_ref[...]
    row = lax.broadcasted_iota(I32, c.shape, 0)
    prev = jnp.where(row == 0, carry[...], pltpu.roll(c, 1, 0))
    carry[...] = c[tc - 1:tc, :]
    p = (mu_ref[...], w0_ref[...], w2_ref[...], a0_ref[...], a2_ref[...], g2_ref[...], kk_ref[...], ka_ref[...])
    r, w, k, v, kk, kka, gate = _rwkv_prep(c, prev, p)
    w_o[...] = w.T
    kk_o[...] = kk.T
    kka_o[...] = kka.T
    k_o[...] = k.T
    r_o[...] = r.T
    v_o[...] = v.T
    bonus_o[...] = _dot(r * k * rk_ref[...], _head_blockdiag(GROUP_WIDTH), HI) * v
    gate_o[...] = gate


def _rwkv_scan_body(w_ref, kk_ref, kka_ref, k_ref, r_ref, v_ref, s0_ref, y_ref, s_ref):
    @pl.when(pl.program_id(0) == 0)
    def _():
        s_ref[...] = s0_ref[...]

    n_vq = s_ref.shape[0]

    def step(t, _):
        vrows = v_ref[t]
        for vq in range(n_vq):
            s = s_ref[vq]
            sa = jnp.sum(s * kk_ref[t], axis=0, keepdims=True)
            s = s * w_ref[t] - sa * kka_ref[t] + vrows[vq:vq + 1, :] * k_ref[t]
            s_ref[vq] = s
            y_ref[t, vq:vq + 1, :] = jnp.sum(s * r_ref[t], axis=0, keepdims=True)
        return 0

    lax.fori_loop(0, v_ref.shape[0], step, 0)


def _rwkv_post_body(y_ref, bonus_ref, gate_ref, lng_ref, lnb_ref, o_ref):
    y = _head_norm(y_ref[...], lng_ref[...], lnb_ref[...], RWKV_LN_EPS)
    o_ref[...] = ((y + bonus_ref[...]) * gate_ref[...]).astype(o_ref.dtype)


def _rwkv_prompt(cols, shift_prev, s0, params, b, t, tc=256, tscan=64):
    mu, w0, w2, a0, a2, g2, k_k, k_a, r_k, ln_g, ln_b = params
    nt = t // tc
    gw = GROUP_WIDTH
    hd = HEAD_DIM
    chains = b * N_HEADS
    rep = LANE // chains
    assert rep * chains == LANE and hd % rep == 0
    n_vq = hd // rep
    full = lambda a: pl.BlockSpec(a.shape, lambda i, n: (0,) * a.ndim)
    tok = pl.BlockSpec((tc, gw), lambda i, n: (i * nt + n, 0))
    vec = jax.ShapeDtypeStruct((b * t, gw), F32)
    prep_in = (mu, w0, w2, a0, a2, g2, k_k, k_a, r_k)
    w, kk, kka, k, r, v, bonus, gate = pl.pallas_call(
        _rwkv_prep_body,
        grid=(b, nt),
        in_specs=[pl.BlockSpec((tc, RWKV_COLS), lambda i, n: (i * nt + n, OFF_RWKV // RWKV_COLS)),
                  pl.BlockSpec((None, 1, RWKV_COLS), lambda i, n: (i, 0, 0))] + [full(a) for a in prep_in],
        out_specs=[pl.BlockSpec((gw, tc), lambda i, n: (0, i * nt + n))] * 6 + [tok] * 2,
        out_shape=[jax.ShapeDtypeStruct((gw, b * t), F32)] * 6 + [vec] * 2,
        scratch_shapes=[pltpu.VMEM((1, RWKV_COLS), F32)],
        compiler_params=_cparams(("parallel", "arbitrary")),
        name="rwkv_prep",
    )(cols, shift_prev.reshape(b, 1, RWKV_COLS), *prep_in)

    def key_tiles(x):
        x = jnp.transpose(x.reshape(N_HEADS, hd, b, t), (3, 1, 2, 0)).reshape(t, hd, chains)
        return jnp.tile(x, (1, 1, rep))
    xts = [key_tiles(z) for z in (w, kk, kka, k, r)]
    vt = jnp.transpose(v.reshape(N_HEADS, rep, n_vq, b, t), (4, 2, 1, 3, 0)).reshape(t, n_vq, LANE)
    s0t = jnp.transpose(s0.reshape(b, N_HEADS, rep, n_vq, hd), (3, 4, 2, 0, 1)).reshape(n_vq, hd, LANE)

    ns = t // tscan
    st_spec = pl.BlockSpec((n_vq, hd, LANE), lambda n: (0, 0, 0))
    yt, st = pl.pallas_call(
        _rwkv_scan_body,
        grid=(ns,),
        in_specs=[pl.BlockSpec((tscan, hd, LANE), lambda n: (n, 0, 0))] * 5
                 + [pl.BlockSpec((tscan, n_vq, LANE), lambda n: (n, 0, 0)), st_spec],
        out_specs=[pl.BlockSpec((tscan, n_vq, LANE), lambda n: (n, 0, 0)), st_spec],
        out_shape=[jax.ShapeDtypeStruct((t, n_vq, LANE), F32), jax.ShapeDtypeStruct((n_vq, hd, LANE), F32)],
        compiler_params=_cparams(("arbitrary",)),
        name="rwkv_scan",
    )(*xts, vt, s0t)
    ys = jnp.transpose(yt.reshape(t, n_vq, rep, b, N_HEADS), (3, 0, 4, 2, 1)).reshape(b * t, gw)
    s_fin = jnp.transpose(st.reshape(n_vq, hd, rep, b, N_HEADS), (3, 4, 2, 0, 1)).reshape(b, N_HEADS, hd, hd)

    tm = min(512, b * t)
    tokm = pl.BlockSpec((tm, gw), lambda i: (i, 0))
    o = pl.pallas_call(
        _rwkv_post_body,
        grid=(b * t // tm,),
        in_specs=[tokm, tokm, tokm, pl.BlockSpec((1, gw), lambda i: (0, 0)), pl.BlockSpec((1, gw), lambda i: (0, 0))],
        out_specs=tokm,
        out_shape=jax.ShapeDtypeStruct((b * t, gw), BF16),
        compiler_params=_cparams(("parallel",)),
        name="rwkv_post",
    )(ys, bonus, gate, ln_g, ln_b)
    return o, s_fin


PAGES_PER_STEP = 8


def _past_stats_body(pt_ref, *refs, n_pages):
    pps = PAGES_PER_STEP
    kt_refs, kct_refs, vct_refs = refs[:pps], refs[pps:2 * pps], refs[2 * pps:3 * pps]
    (q_ref, w1r_ref, w1_ref, b1_ref, w2_ref, pe_ref, top_ref, cmp_ref, qb, gsum, kc_rows, vc_rows) = refs[3 * pps:]
    step = pl.program_id(1)
    pages_per_blk = MOBA_BLOCK // PAGE_SIZE
    nb = n_pages // pages_per_blk
    n_chunk = n_pages * PAGE_SIZE // CMP_STRIDE

    @pl.when(step == 0)
    def _():
        gsum[...] = jnp.zeros_like(gsum)
        qb[...] = jnp.broadcast_to(q_ref[...], qb.shape)

    for i in range(pps):
        p = step * pps + i
        prod = (kt_refs[i][...] * qb[...]).reshape(N_HEADS, HEAD_DIM // 8, 8, PAGE_SIZE)
        gsum[p // pages_per_blk] += jnp.sum(prod, axis=1)
        row0 = pl.multiple_of(p * PAGE_SIZE, PAGE_SIZE)
        kc_rows[pl.ds(row0, PAGE_SIZE), :] = kct_refs[i][...].reshape(LANE, PAGE_SIZE).T
        vc_rows[pl.ds(row0, PAGE_SIZE), :] = vct_refs[i][...].reshape(LANE, PAGE_SIZE).T

    @pl.when(step == n_pages // pps - 1)
    def _():
        gate = jnp.sum(jnp.sum(gsum[...], axis=2), axis=-1) * (1.0 / MOBA_BLOCK)
        ids = lax.broadcasted_iota(I32, (nb, N_HEADS), 0)
        rows8 = lax.broadcasted_iota(I32, (8, N_HEADS), 0)
        top = jnp.zeros((8, N_HEADS), I32)
        for j in range(MOBA_TOPK):
            best = jnp.max(gate, axis=0, keepdims=True)
            arg = jnp.min(jnp.where(gate == best, ids, nb), axis=0, keepdims=True)
            top = jnp.where(rows8 == j, arg, top)
            gate = jnp.where(ids == arg, -jnp.inf, gate)
        top_ref[...] = top
        for kv, src in enumerate((kc_rows, vc_rows)):
            bias = _cmp_bias(pe_ref, w1_ref, b1_ref, kv)
            out = _compress(lambda r: src[pl.ds(r, n_chunk, stride=CMP_STRIDE), :], n_chunk, w1r_ref, kv, bias,
                            w2_ref[kv])
            for g in range(NSA_KV_HEADS):
                cmp_ref[kv * NSA_KV_HEADS + g] = out[g]


def _past_stats(page_table, moba_t, nsa_t, q_col, cmp_w, layer):
    w1r, w1, b1, w2, pe = cmp_w
    bs, n_pages = page_table.shape
    nb = n_pages * PAGE_SIZE // MOBA_BLOCK
    rows = n_pages * PAGE_SIZE
    n_chunk = rows // CMP_STRIDE
    pps = PAGES_PER_STEP
    assert n_pages % pps == 0
    full = lambda a: pl.BlockSpec(a.shape, lambda i, p, pt: (0,) * a.ndim)
    k_page = lambda j: pl.BlockSpec((None, None, None, N_HEADS, HEAD_DIM, PAGE_SIZE),
                                    lambda i, p, pt: (layer, pt[i, p * pps + j], 0, 0, 0, 0))
    nsa_page = lambda t, j: pl.BlockSpec((None, None, None, NSA_KV_HEADS, HEAD_DIM, PAGE_SIZE),
                                         lambda i, p, pt: (layer, pt[i, p * pps + j], t, 0, 0, 0))
    return pl.pallas_call(
        functools.partial(_past_stats_body, n_pages=n_pages),
        grid_spec=pltpu.PrefetchScalarGridSpec(
            num_scalar_prefetch=1,
            grid=(bs, n_pages // pps),
            in_specs=[k_page(j) for j in range(pps)] + [nsa_page(0, j) for j in range(pps)]
                     + [nsa_page(1, j) for j in range(pps)]
                     + [pl.BlockSpec((None, N_HEADS, HEAD_DIM, 1), lambda i, p, pt: (i, 0, 0, 0)),
                        full(w1r), full(w1), full(b1), full(w2), full(pe)],
            out_specs=[pl.BlockSpec((None, 8, N_HEADS), lambda i, p, pt: (i, 0, 0)),
                       pl.BlockSpec((None, 2 * NSA_KV_HEADS, n_chunk, HEAD_DIM), lambda i, p, pt: (i, 0, 0, 0))],
            scratch_shapes=[pltpu.VMEM((N_HEADS, HEAD_DIM, PAGE_SIZE), F32), pltpu.VMEM((nb, N_HEADS, 8, PAGE_SIZE), F32),
                            pltpu.VMEM((rows, LANE), F32), pltpu.VMEM((rows, LANE), F32)]),
        out_shape=[jax.ShapeDtypeStruct((bs, 8, N_HEADS), I32),
                   jax.ShapeDtypeStruct((bs, 2 * NSA_KV_HEADS, n_chunk, HEAD_DIM), F32)],
        compiler_params=_cparams(("parallel", "arbitrary")),
        name="past_stats",
    )(page_table, *([moba_t] * pps), *([nsa_t] * (2 * pps)), q_col, w1r, w1, b1, w2, pe)


def _moba_sample_body(pt_ref, top_ref, *refs):
    n_pg = (len(refs) - 4) // 2
    kt_refs, vt_refs = refs[:n_pg], refs[n_pg:2 * n_pg]
    q_ref, kn_ref, vn_ref, o_ref = refs[2 * n_pg:]
    q = q_ref[...] * ATTN_SCALE
    q8 = jnp.broadcast_to(q, (8, HEAD_DIM)).astype(BF16)
    s_all = [_dot(q8, kt[...].astype(BF16)) for kt in kt_refs]
    s_self = jnp.sum(q * kn_ref[...], axis=-1, keepdims=True)
    m = s_self
    for s in s_all:
        m = jnp.maximum(m, jnp.max(s, axis=-1, keepdims=True))
    e_self = jnp.exp(s_self - m)
    den = e_self
    acc = e_self * vn_ref[...]
    for s, vt in zip(s_all, vt_refs):
        pr = jnp.exp(s - m)
        den = den + jnp.sum(pr, axis=-1, keepdims=True)
        acc = acc + _dot_nt(pr.astype(BF16), vt[...].astype(BF16))
    o_ref[...] = (acc / den)[0:1, :]


def _moba_sample(page_table, top, moba_t, q, k_new, v_new, layer):
    bs = page_table.shape[0]
    ppb = MOBA_BLOCK // PAGE_SIZE

    def page_spec(kv, j, r):
        return pl.BlockSpec((None, None, None, None, HEAD_DIM, PAGE_SIZE),
                            lambda i, h, pt, tp: (layer, pt[i, tp[i, j, h] * ppb + r], kv, h, 0, 0))

    pages = [(j, r) for j in range(MOBA_TOPK) for r in range(ppb)]
    head = pl.BlockSpec((None, None, 1, HEAD_DIM), lambda i, h, pt, tp: (i, h, 0, 0))
    return pl.pallas_call(
        _moba_sample_body,
        grid_spec=pltpu.PrefetchScalarGridSpec(
            num_scalar_prefetch=2,
            grid=(bs, N_HEADS),
            in_specs=[page_spec(0, j, r) for j, r in pages] + [page_spec(1, j, r) for j, r in pages]
                     + [head, head, head],
            out_specs=head),
        out_shape=jax.ShapeDtypeStruct((bs, N_HEADS, 1, HEAD_DIM), F32),
        compiler_params=_cparams(("parallel", "parallel")),
        name="moba_sample",
    )(page_table, top, *([moba_t] * (2 * len(pages))), q, k_new, v_new)


def _heads_to_rows(row, g):
    parts = [row[:, (g * NSA_GROUP + j) * HEAD_DIM:(g * NSA_GROUP + j + 1) * HEAD_DIM] for j in range(NSA_GROUP)]
    return jnp.concatenate(parts + [jnp.zeros((8 - NSA_GROUP, HEAD_DIM), F32)], axis=0)


def _nsa_sample_sel_body(q_ref, cmp_ref, win_ref, kvn_ref, gate_ref, idx_ref, part_ref, *, past):
    hd = HEAD_DIM
    n_chunk = past // CMP_STRIDE
    n_slc = past // SLC_BLOCK + 1
    n_slc_pad = -(-n_slc // LANE) * LANE
    cur = past // SLC_BLOCK
    qrow = q_ref[...]
    gates = _sigmoid(gate_ref[...])
    kvn = kvn_ref[...]
    rows8 = lax.broadcasted_iota(I32, (8, 1), 0)
    ids = lax.broadcasted_iota(I32, (1, n_slc_pad), 1)
    m_iota = lax.broadcasted_iota(I32, (n_slc_pad, n_slc_pad), 0)
    j_iota = lax.broadcasted_iota(I32, (n_slc_pad, n_slc_pad), 1)
    lane = lax.broadcasted_iota(I32, (1, LANE), 1)
    idx_out = jnp.zeros((8, LANE), I32)
    for g in range(NSA_KV_HEADS):
        q8 = _heads_to_rows(qrow, g)
        q8b = q8.astype(BF16)
        n_ids = lax.broadcasted_iota(I32, (8, n_chunk), 1)
        l_cmp = _dot_nt(q8b, cmp_ref[g].astype(BF16)) * ATTN_SCALE
        p_cmp = _softmax_rows(l_cmp, n_ids * CMP_STRIDE + (CMP_LEN - 1) <= past)
        o_cmp = _dot(p_cmp.astype(BF16), cmp_ref[NSA_KV_HEADS + g].astype(BF16))
        imp = jnp.sum(jnp.where(rows8 < NSA_GROUP, p_cmp, 0.0), axis=0, keepdims=True)
        imp_slc = _dot(jnp.broadcast_to(imp, (8, n_chunk)), _slc_matrix(n_chunk, n_chunk - 1, n_slc_pad), HI)
        eligible = (ids <= cur) & (ids < n_slc)
        forced = (ids == 0) | (ids == cur) | (ids == cur - 1)
        score = jnp.where(eligible, jnp.where(forced, jnp.inf, imp_slc[0:1, :]), -jnp.inf)
        s_col = jnp.broadcast_to(score, (LANE, n_slc_pad)).T[:, 0:1]
        beats = (s_col > score) | ((s_col == score) & (m_iota < j_iota))
        rank = jnp.sum(beats.astype(F32), axis=0, keepdims=True)
        sel = jnp.where(eligible & (rank < SLC_TOPK), 1.0, 0.0)
        sel_col = jnp.broadcast_to(sel, (LANE, n_slc_pad)).T[:, 0:1]
        before = jnp.sum(jnp.where(m_iota < j_iota, sel_col, 0.0), axis=0, keepdims=True)
        idx_row = jnp.zeros((1, LANE), I32)
        for i in range(SLC_TOPK):
            hit = (sel > 0.5) & (before == i)
            idx_i = jnp.sum(jnp.where(hit, ids, 0), axis=-1, keepdims=True)
            idx_row = jnp.where(lane == i, idx_i, idx_row)
        idx_out = jnp.where(lax.broadcasted_iota(I32, (8, LANE), 0) == g, idx_row, idx_out)
        kw = win_ref[0, g].astype(BF16)
        vw = win_ref[1, g].astype(BF16)
        n_buf = win_ref.shape[-1]
        w_ids = lax.broadcasted_iota(I32, (8, n_buf), 1)
        l_win = _dot(q8b, kw) * ATTN_SCALE
        w_mask = w_ids >= n_buf - (WINDOW - 1)
        kw_new = kvn[:, 4 * NSA_KV_WIDTH + g * hd:4 * NSA_KV_WIDTH + (g + 1) * hd]
        vw_new = kvn[:, 5 * NSA_KV_WIDTH + g * hd:5 * NSA_KV_WIDTH + (g + 1) * hd]
        s_self = jnp.sum(q8 * kw_new, axis=-1, keepdims=True) * ATTN_SCALE
        m = jnp.maximum(jnp.max(jnp.where(w_mask, l_win, NEG_INF), axis=-1, keepdims=True), s_self)
        e = jnp.where(w_mask, jnp.exp(l_win - m), 0.0)
        e_self = jnp.exp(s_self - m)
        o_win = (_dot_nt(e.astype(BF16), vw) + e_self * vw_new) / (jnp.sum(e, axis=-1, keepdims=True) + e_self)
        for j in range(NSA_GROUP):
            h = g * NSA_GROUP + j
            part_ref[:, h * hd:(h + 1) * hd] = (gates[:, h:h + 1] * o_cmp[j:j + 1]
                                                + gates[:, 2 * N_HEADS + h:2 * N_HEADS + h + 1] * o_win[j:j + 1])
    idx_ref[...] = idx_out


def _nsa_sample_sel(q, cmp_tok, win_state, kv_new, gates, layer, past):
    bs = q.shape[0]
    n_buf = win_state.shape[-1]
    n_chunk = cmp_tok.shape[2]
    return pl.pallas_call(
        functools.partial(_nsa_sample_sel_body, past=past),
        grid=(bs,),
        in_specs=[pl.BlockSpec((None, 1, GROUP_WIDTH), lambda i: (i, 0, 0)),
                  pl.BlockSpec((None, 2 * NSA_KV_HEADS, n_chunk, HEAD_DIM), lambda i: (i, 0, 0, 0)),
                  pl.BlockSpec((None, None, 2, NSA_KV_HEADS, HEAD_DIM, n_buf), lambda i: (layer, i, 0, 0, 0, 0)),
                  pl.BlockSpec((None, 1, 6 * NSA_KV_WIDTH), lambda i: (i, 0, 0)),
                  pl.BlockSpec((None, 1, LANE), lambda i: (i, 0, 0))],
        out_specs=[pl.BlockSpec((None, 8, LANE), lambda i: (i, 0, 0)),
                   pl.BlockSpec((None, 1, GROUP_WIDTH), lambda i: (i, 0, 0))],
        out_shape=[jax.ShapeDtypeStruct((bs, 8, LANE), I32), jax.ShapeDtypeStruct((bs, 1, GROUP_WIDTH), F32)],
        compiler_params=_cparams(("parallel",)),
        name="nsa_sample_sel",
    )(q, cmp_tok, win_state, kv_new, gates)


NSA_BLOCKS_PER_STEP = 4


def _nsa_sample_attn_body(pt_ref, idx_ref, *refs, n_past_blk):
    bps = NSA_BLOCKS_PER_STEP
    n_pg = NSA_KV_HEADS * bps
    ks_refs, vs_refs = refs[:n_pg], refs[n_pg:2 * n_pg]
    q_ref, kvn_ref, gate_ref, part_ref, o_ref, m_s, l_s, acc_s = refs[2 * n_pg:]
    b, i = pl.program_id(0), pl.program_id(1)
    hd = HEAD_DIM
    qrow = q_ref[...]
    kvn = kvn_ref[...]
    half = lax.broadcasted_iota(I32, (8, PAGE_SIZE), 1) // SLC_BLOCK
    for g in range(NSA_KV_HEADS):
        q8 = _heads_to_rows(qrow, g)
        ks_new = kvn[:, 2 * NSA_KV_WIDTH + g * hd:2 * NSA_KV_WIDTH + (g + 1) * hd]
        vs_new = kvn[:, 3 * NSA_KV_WIDTH + g * hd:3 * NSA_KV_WIDTH + (g + 1) * hd]

        @pl.when(i == 0)
        def _():
            m_s[g] = jnp.broadcast_to(jnp.sum(q8 * ks_new, axis=-1, keepdims=True) * ATTN_SCALE, (8, hd))
            l_s[g] = jnp.ones((8, hd), F32)
            acc_s[g] = jnp.broadcast_to(vs_new, (8, hd))

        q8b = (q8 * ATTN_SCALE).astype(BF16)
        m_i = m_s[g][:, 0:1]
        l_i = l_s[g][:, 0:1]
        acc = acc_s[g]
        for u in range(bps):
            blk = idx_ref[b, g, i * bps + u]
            mask = half == jnp.where(blk < n_past_blk, blk % (PAGE_SIZE // SLC_BLOCK), -1)
            s = jnp.where(mask, _dot(q8b, ks_refs[g * bps + u][...].astype(BF16)), NEG_INF)
            m_new = jnp.maximum(m_i, jnp.max(s, axis=-1, keepdims=True))
            pr = jnp.where(mask, jnp.exp(s - m_new), 0.0)
            alpha = jnp.exp(m_i - m_new)
            l_i = alpha * l_i + jnp.sum(pr, axis=-1, keepdims=True)
            acc = alpha * acc + _dot_nt(pr.astype(BF16), vs_refs[g * bps + u][...].astype(BF16))
            m_i = m_new
        l_s[g] = jnp.broadcast_to(l_i, (8, hd))
        acc_s[g] = acc
        m_s[g] = jnp.broadcast_to(m_i, (8, hd))

    @pl.when(i == pl.num_programs(1) - 1)
    def _():
        gates = _sigmoid(gate_ref[...])
        for g in range(NSA_KV_HEADS):
            o_sel = acc_s[g] / l_s[g]
            for j in range(NSA_GROUP):
                h = g * NSA_GROUP + j
                sl = slice(h * hd, (h + 1) * hd)
                o_ref[:, sl] = (part_ref[:, sl] + gates[:, N_HEADS + h:N_HEADS + h + 1] * o_sel[j:j + 1]
                                ).astype(o_ref.dtype)


def _nsa_sample_attn(page_table, sel_idx, nsa_cache, q, kv_new, gates, part, layer, past):
    bs = page_table.shape[0]
    n_past_blk = past // SLC_BLOCK
    per_page = PAGE_SIZE // SLC_BLOCK

    bps = NSA_BLOCKS_PER_STEP

    def blk_spec(g, u, t):
        def imap(b, i, pt, ix):
            blk = jnp.minimum(ix[b, g, i * bps + u], n_past_blk - 1)
            return (layer, pt[b, blk // per_page], t, g, 0, 0)
        return pl.BlockSpec((None, None, None, None, HEAD_DIM, PAGE_SIZE), imap)

    row = lambda w: pl.BlockSpec((None, 1, w), lambda b, i, pt, ix: (b, 0, 0))
    gu = [(g, u) for g in range(NSA_KV_HEADS) for u in range(bps)]
    return pl.pallas_call(
        functools.partial(_nsa_sample_attn_body, n_past_blk=n_past_blk),
        grid_spec=pltpu.PrefetchScalarGridSpec(
            num_scalar_prefetch=2,
            grid=(bs, SLC_TOPK // bps),
            in_specs=[blk_spec(g, u, 2) for g, u in gu] + [blk_spec(g, u, 3) for g, u in gu]
                     + [row(GROUP_WIDTH), row(6 * NSA_KV_WIDTH), row(LANE), row(GROUP_WIDTH)],
            out_specs=row(GROUP_WIDTH),
            scratch_shapes=[pltpu.VMEM((NSA_KV_HEADS, 8, HEAD_DIM), F32)] * 3),
        out_shape=jax.ShapeDtypeStruct((bs, 1, GROUP_WIDTH), F32),
        compiler_params=_cparams(("parallel", "arbitrary")),
        name="nsa_sample_attn",
    )(page_table, sel_idx, *([nsa_cache] * (2 * len(gu))), q, kv_new, gates, part)


def _recur_sample_body(c_ref, shift_ref, mu_ref, w0_ref, w2_ref, a0_ref, a2_ref, g2_ref, kk_ref, ka_ref, rk_ref,
                       lng_ref, lnb_ref, s_rw_ref, ret_ref, cos_ref, sin_ref, gam_ref, gn_g_ref, gn_b_ref, s_rt_ref,
                       o_rw_ref, s_rw_out, o_rt_ref, s_rt_out, o_scr):
    gw = GROUP_WIDTH
    hd = HEAD_DIM
    p = (mu_ref[...], w0_ref[...], w2_ref[...], a0_ref[...], a2_ref[...], g2_ref[...], kk_ref[...], ka_ref[...])
    r, w, k, v, kk, kka, gate = _rwkv_prep(c_ref[...], shift_ref[...], p)

    def put_state(h, j, s):
        s_rw_out[j, h] = s

    ys = _rwkv_steps((r, w, k, v, kk, kka), lambda h, j: s_rw_ref[j, h], put_state, 1)
    o_rw_ref[...] = _rwkv_post(ys, r, k, v, gate, rk_ref[...], lng_ref[...], lnb_ref[...]).astype(o_rw_ref.dtype)

    cos = cos_ref[...]
    sin = sin_ref[...]
    q = _rope(ret_ref[:, 0:gw], cos, sin)
    kr = _rope(ret_ref[:, gw:2 * gw], cos, sin) * ATTN_SCALE
    vr = ret_ref[:, 2 * gw:3 * gw]
    gam = gam_ref[...]
    qk = _dot(q * kr, _head_blockdiag(gw), HI)
    pad = jnp.zeros((LANE - 8, LANE), F32)
    for hp in range(N_HEADS // 2):
        ps = slice(hp * LANE, (hp + 1) * LANE)
        qt = jnp.concatenate([q[:, ps], pad], axis=0).T
        kt = jnp.concatenate([kr[:, ps], pad], axis=0).T
        for h2 in range(2):
            h = 2 * hp + h2
            sl = slice(h * hd, (h + 1) * hd)
            for j in range(8):
                s = s_rt_ref[j, h]
                qcol = qt[h2 * hd:(h2 + 1) * hd, j:j + 1]
                kcol = kt[h2 * hd:(h2 + 1) * hd, j:j + 1]
                g_h = gam[:, sl]
                o_scr[j:j + 1, sl] = (qk[j:j + 1, sl] * vr[j:j + 1, sl]
                                      + g_h * jnp.sum(qcol * s, axis=0, keepdims=True))
                s_rt_out[j, h] = s * g_h + kcol * vr[j:j + 1, sl]
    gt = ret_ref[:, 3 * gw:4 * gw]
    y = _head_norm(o_scr[...], gn_g_ref[...], gn_b_ref[...], GN_EPS)
    o_rt_ref[...] = (gt * _sigmoid(gt) * y).astype(o_rt_ref.dtype)


def _recur_sample(c_rwkv, shift, rwkv_params, s_rwkv, c_ret, rope, gamma, gn_g, gn_b, s_ret):
    bs = c_rwkv.shape[0]
    cos, sin = rope
    st = jax.ShapeDtypeStruct((bs, N_HEADS, HEAD_DIM, HEAD_DIM), F32)
    ob = jax.ShapeDtypeStruct((bs, GROUP_WIDTH), BF16)
    return pl.pallas_call(
        _recur_sample_body,
        out_shape=[ob, st, ob, st],
        scratch_shapes=[pltpu.VMEM((bs, GROUP_WIDTH), F32)],
        compiler_params=pltpu.CompilerParams(vmem_limit_bytes=VMEM_LIMIT),
        name="recur_sample",
    )(c_rwkv, shift, *rwkv_params, s_rwkv, c_ret, cos, sin, gamma, gn_g, gn_b, s_ret)


def _prep_rwkv_params(mu, w0, w2, a0, a2, g2, k_k, k_a, r_k, ln_g, ln_b):
    row = lambda z: z.reshape(1, -1)
    return (row(mu), row(w0), w2, row(a0), a2, g2, row(k_k), row(k_a), row(r_k), row(ln_g), row(ln_b))


def _prep_cmp_weights(w1, b1, w2, pe):
    span = CMP_LEN // CMP_STRIDE
    w1r = w1.reshape(2, span, CMP_STRIDE, HEAD_DIM, CMP_HIDDEN)
    w1r = jnp.transpose(w1r, (0, 2, 3, 1, 4)).reshape(2, CMP_STRIDE, HEAD_DIM, span * CMP_HIDDEN).astype(BF16)
    return (w1r, w1, b1.reshape(2, 1, CMP_HIDDEN), w2, pe.reshape(2, 1, CMP_LEN * HEAD_DIM))


def _pad_w_in(w):
    o_nsa = MOBA_COLS
    o_gate = o_nsa + GROUP_WIDTH + 6 * NSA_KV_WIDTH
    o_rwkv = o_nsa + NSA_COLS
    o_ret = o_rwkv + RWKV_COLS
    zeros = jnp.zeros((w.shape[0], N_PAD - OFF_GATE - 3 * N_HEADS), w.dtype)
    return jnp.concatenate([w[:, o_rwkv:o_ret], w[:, :o_gate], w[:, o_ret:], w[:, o_gate:o_rwkv], zeros], axis=1)


def kernel(x_prompt, x_sample, cache_moba_kv, cache_nsa_kv, state_nsa_win, state_rwkv, state_rwkv_shift, state_ret,
           page_table, norm_g, w_in, w_out, w_up, w_down, nsa_cmp_pe, nsa_cmp_w1, nsa_cmp_b1, nsa_cmp_w2, rwkv_mu,
           rwkv_w0, rwkv_w2, rwkv_a0, rwkv_a2, rwkv_g2, rwkv_k_k, rwkv_k_a, rwkv_r_k, rwkv_ln_g, rwkv_ln_b, ret_gn_g,
           ret_gn_b):
    bp, t, d = x_prompt.shape
    bs = x_sample.shape[0]
    assert x_sample.shape[1] == 1 and d == D_MODEL
    depth = w_in.shape[0]
    past = page_table.shape[1] * PAGE_SIZE
    gw = GROUP_WIDTH
    hd = HEAD_DIM

    log_gamma = jnp.log(1.0 - jnp.exp2(-5.0 - jnp.arange(N_HEADS, dtype=F32)))
    ret_tables = _ret_tables(log_gamma, RET_CHUNK)
    rope_p = _rope_tables(jnp.arange(t, dtype=I32))
    rope_s = _rope_tables(jnp.full((1,), past, I32))
    gamma_row = jnp.repeat(jnp.exp(log_gamma), hd)[None, :]
    moba_t = jnp.transpose(cache_moba_kv, (0, 1, 3, 4, 5, 2))
    nsa_t = jnp.transpose(cache_nsa_kv, (0, 1, 3, 4, 5, 2))
    win_t = jnp.transpose(state_nsa_win, (0, 1, 3, 4, 5, 2))

    xp = x_prompt.reshape(bp * t, d)
    xs = x_sample.reshape(bs, d)
    zero_state = jnp.zeros((bp, N_HEADS, hd, hd), F32)
    zero_shift = jnp.zeros((bp, RWKV_COLS), F32)
    st_p, st_s = [], []
    for l in range(depth):
        g = norm_g[l].reshape(4, 1, d)
        w_in_l = _pad_w_in(w_in[l]).astype(BF16)
        w_out_l = w_out[l].astype(BF16)
        w_up_l = w_up[l].astype(BF16)
        w_down_l = w_down[l].astype(BF16)
        cmp_w = _prep_cmp_weights(nsa_cmp_w1[l], nsa_cmp_b1[l], nsa_cmp_w2[l], nsa_cmp_pe[l])
        rwkv_p = _prep_rwkv_params(rwkv_mu[l], rwkv_w0[l], rwkv_w2[l], rwkv_a0[l], rwkv_a2[l], rwkv_g2[l],
                                   rwkv_k_k[l], rwkv_k_a[l], rwkv_r_k[l], rwkv_ln_g[l], rwkv_ln_b[l])
        gn_g = ret_gn_g[l].reshape(1, gw)
        gn_b = ret_gn_b[l].reshape(1, gw)

        cols = _inproj(xp, g[0], w_in_l, min(1024, bp * t), 1024)
        o_moba = _moba_prompt(cols, bp, t)
        o_nsa = _nsa_prompt(cols, cmp_w, bp, t)
        o_rwkv, rwkv_s = _rwkv_prompt(cols, zero_shift, zero_state, rwkv_p, bp, t)
        o_ret, ret_s = _ret_prompt(cols, zero_state, ret_tables, rope_p, gn_g, gn_b, bp, t)
        xp = _outproj((o_moba, o_nsa, o_rwkv, o_ret), xp, g[1], w_out_l, 256)
        xp = _ffn(xp, g[2], g[3], w_up_l, w_down_l, min(512, bp * t), 1024)
        c3 = cols.reshape(bp, t, N_PAD)
        win_keep = min(WINDOW, t)
        st_p.append((c3[:, :, OFF_MOBA + gw:OFF_MOBA + 3 * gw].reshape(bp, t, 2, N_HEADS, hd),
                     c3[:, :, OFF_NSAKV:OFF_NSAKV + 4 * NSA_KV_WIDTH].reshape(bp, t, 4, NSA_KV_HEADS, hd),
                     c3[:, t - win_keep:, OFF_NSAKV + 4 * NSA_KV_WIDTH:OFF_NSAKV + 6 * NSA_KV_WIDTH
                        ].reshape(bp, win_keep, 2, NSA_KV_HEADS, hd),
                     rwkv_s, c3[:, t - 1, OFF_RWKV:OFF_RWKV + RWKV_COLS], ret_s))

        cs = _inproj(xs, g[0], w_in_l, bs, 512)
        heads = lambda z: z.reshape(bs, N_HEADS, 1, hd)
        q_m = cs[:, OFF_MOBA:OFF_MOBA + gw]
        k_m = cs[:, OFF_MOBA + gw:OFF_MOBA + 2 * gw]
        v_m = cs[:, OFF_MOBA + 2 * gw:OFF_MOBA + 3 * gw]
        top, cmp_tok = _past_stats(page_table, moba_t, nsa_t, q_m.reshape(bs, N_HEADS, hd, 1), cmp_w, l)
        o_moba_s = _moba_sample(page_table, top[:, :MOBA_TOPK, :], moba_t, heads(q_m), heads(k_m), heads(v_m), l)
        q_n = cs[:, OFF_NSAQ:OFF_NSAQ + gw].reshape(bs, 1, gw)
        kv_new = cs[:, OFF_NSAKV:OFF_NSAKV + 6 * NSA_KV_WIDTH].reshape(bs, 1, 6 * NSA_KV_WIDTH)
        gates = cs[:, OFF_GATE:OFF_GATE + LANE].reshape(bs, 1, LANE)
        sel_idx, part = _nsa_sample_sel(q_n, cmp_tok, win_t, kv_new, gates, l, past)
        o_nsa_s = _nsa_sample_attn(page_table, sel_idx[:, :NSA_KV_HEADS, :SLC_TOPK], nsa_t, q_n, kv_new, gates, part,
                                   l, past)
        o_rwkv_s, rwkv_s_s, o_ret_s, ret_s_s = _recur_sample(
            cs[:, OFF_RWKV:OFF_RWKV + RWKV_COLS], state_rwkv_shift[l], rwkv_p, state_rwkv[l],
            cs[:, OFF_RET:OFF_RET + RET_COLS], rope_s, gamma_row, gn_g, gn_b, state_ret[l])
        parts_s = (o_moba_s.reshape(bs, gw).astype(BF16), o_nsa_s.reshape(bs, gw).astype(BF16), o_rwkv_s, o_ret_s)
        xs = _outproj(parts_s, xs, g[1], w_out_l, bs)
        xs = _ffn(xs, g[2], g[3], w_up_l, w_down_l, bs, 512)
        win_new = cs[:, OFF_NSAKV + 4 * NSA_KV_WIDTH:OFF_NSAKV + 6 * NSA_KV_WIDTH].reshape(bs, 1, 2, NSA_KV_HEADS, hd)
        win_all = jnp.concatenate([state_nsa_win[l], win_new], axis=1)
        keep_s = min(WINDOW, win_all.shape[1])
        st_s.append((cs[:, OFF_MOBA + gw:OFF_MOBA + 3 * gw].reshape(bs, 1, 2, N_HEADS, hd),
                     cs[:, OFF_NSAKV:OFF_NSAKV + 4 * NSA_KV_WIDTH].reshape(bs, 1, 4, NSA_KV_HEADS, hd),
                     win_all[:, win_all.shape[1] - keep_s:],
                     rwkv_s_s, cs[:, OFF_RWKV:OFF_RWKV + RWKV_COLS], ret_s_s))

    stk = lambda sts, i: jnp.stack([s[i] for s in sts], axis=0)
    outs = [xp.reshape(bp, t, d), xs.reshape(bs, 1, d)]
    for i in range(6):
        outs += [stk(st_p, i), stk(st_s, i)]
    return tuple(outs)
```

```python
import functools

import jax
import jax.numpy as jnp
import numpy as np
from jax import lax
from jax.experimental import pallas as pl
from jax.experimental.pallas import tpu as pltpu

F32 = jnp.float32
BF16 = jnp.bfloat16
I32 = jnp.int32
HI = lax.Precision.HIGHEST

D_MODEL = 2048
HEAD_DIM = 64
GROUP_WIDTH = D_MODEL // 4
N_HEADS = GROUP_WIDTH // HEAD_DIM
D_FF = 4 * D_MODEL
RMS_EPS = 1e-6
GN_EPS = 1e-5
NEG_INF = -1e30
ATTN_SCALE = HEAD_DIM ** -0.5
PAGE_SIZE = 128
MOBA_BLOCK = 256
MOBA_TOPK = 3
NSA_KV_HEADS = 2
NSA_GROUP = N_HEADS // NSA_KV_HEADS
NSA_KV_WIDTH = NSA_KV_HEADS * HEAD_DIM
CMP_LEN = 32
CMP_STRIDE = 16
CMP_HIDDEN = 2 * HEAD_DIM
SLC_BLOCK = 64
SLC_TOPK = 16
WINDOW = 512
RWKV_DECAY_RANK = 64
RWKV_AAA_RANK = 64
RWKV_GATE_RANK = 128
RWKV_LN_EPS = 64e-5
RET_CHUNK = 128
ROPE_BASE = 10000.0

MOBA_COLS = 3 * GROUP_WIDTH
NSA_COLS = GROUP_WIDTH + 6 * NSA_KV_WIDTH + 3 * N_HEADS
RWKV_COLS = 3 * GROUP_WIDTH + RWKV_DECAY_RANK + RWKV_AAA_RANK + RWKV_GATE_RANK
RET_COLS = 4 * GROUP_WIDTH
IN_COLS = MOBA_COLS + NSA_COLS + RWKV_COLS + RET_COLS

OFF_RWKV = 0
OFF_MOBA = OFF_RWKV + RWKV_COLS
OFF_NSAQ = OFF_MOBA + MOBA_COLS
OFF_NSAKV = OFF_NSAQ + GROUP_WIDTH
OFF_RET = OFF_NSAKV + 6 * NSA_KV_WIDTH
OFF_GATE = OFF_RET + RET_COLS
N_PAD = OFF_GATE + 512

LANE = 128
VMEM_LIMIT = 56 * 1024 * 1024


def _cparams(sem):
    return pltpu.CompilerParams(dimension_semantics=sem, vmem_limit_bytes=VMEM_LIMIT)


def _dot(a, b, precision=None):
    return jnp.dot(a, b, preferred_element_type=F32, precision=precision)


def _dot_nt(a, b, precision=None):
    return lax.dot_general(a, b, (((1,), (1,)), ((), ())), preferred_element_type=F32, precision=precision)


def _dot_tn(a, b, precision=None):
    return lax.dot_general(a, b, (((0,), (0,)), ((), ())), preferred_element_type=F32, precision=precision)


def _head_blockdiag(n, scale=1.0):
    r = lax.broadcasted_iota(I32, (n, n), 0) // HEAD_DIM
    c = lax.broadcasted_iota(I32, (n, n), 1) // HEAD_DIM
    return jnp.where(r == c, scale, 0.0).astype(F32)


def _sigmoid(x):
    return 1.0 / (1.0 + jnp.exp(-x))


def _inproj_body(x_ref, g_ref, w_ref, o_ref, h_scr):
    @pl.when(pl.program_id(1) == 0)
    def _():
        x = x_ref[...]
        ms = jnp.mean(x * x, axis=-1, keepdims=True)
        h_scr[...] = (x * lax.rsqrt(ms + RMS_EPS) * g_ref[...]).astype(BF16)

    o_ref[...] = _dot(h_scr[...], w_ref[...])


def _inproj(x, g, w, tm, tn):
    m, d = x.shape
    n = w.shape[1]
    return pl.pallas_call(
        _inproj_body,
        grid=(m // tm, n // tn),
        in_specs=[pl.BlockSpec((tm, d), lambda i, j: (i, 0)),
                  pl.BlockSpec((1, d), lambda i, j: (0, 0)),
                  pl.BlockSpec((d, tn), lambda i, j: (0, j))],
        out_specs=pl.BlockSpec((tm, tn), lambda i, j: (i, j)),
        out_shape=jax.ShapeDtypeStruct((m, n), F32),
        scratch_shapes=[pltpu.VMEM((tm, d), BF16)],
        compiler_params=_cparams(("parallel", "arbitrary")),
        name="inproj",
    )(x, g, w)


def _outproj_body(a_ref, b_ref, c_ref, d_ref, x_ref, g_ref, w_ref, o_ref):
    gw = GROUP_WIDTH
    y = _dot(a_ref[...], w_ref[0:gw, :])
    y += _dot(b_ref[...], w_ref[gw:2 * gw, :])
    y += _dot(c_ref[...], w_ref[2 * gw:3 * gw, :])
    y += _dot(d_ref[...], w_ref[3 * gw:4 * gw, :])
    ms = jnp.mean(y * y, axis=-1, keepdims=True)
    o_ref[...] = x_ref[...] + y * lax.rsqrt(ms + RMS_EPS) * g_ref[...]


def _outproj(parts, x, g, w, tm):
    m, d = x.shape
    gw = GROUP_WIDTH
    part_spec = pl.BlockSpec((tm, gw), lambda i: (i, 0))
    return pl.pallas_call(
        _outproj_body,
        grid=(m // tm,),
        in_specs=[part_spec, part_spec, part_spec, part_spec,
                  pl.BlockSpec((tm, d), lambda i: (i, 0)),
                  pl.BlockSpec((1, d), lambda i: (0, 0)),
                  pl.BlockSpec((4 * gw, d), lambda i: (0, 0))],
        out_specs=pl.BlockSpec((tm, d), lambda i: (i, 0)),
        out_shape=jax.ShapeDtypeStruct((m, d), F32),
        compiler_params=_cparams(("parallel",)),
        name="outproj",
    )(*parts, x, g, w)


def _ffn_body(x_ref, g2_ref, g3_ref, wu_ref, wd_ref, o_ref, h_scr, acc_scr):
    f = pl.program_id(1)

    @pl.when(f == 0)
    def _():
        x = x_ref[...]
        ms = jnp.mean(x * x, axis=-1, keepdims=True)
        h_scr[...] = (x * lax.rsqrt(ms + RMS_EPS) * g2_ref[...]).astype(BF16)
        acc_scr[...] = jnp.zeros_like(acc_scr)

    u = jnp.maximum(_dot(h_scr[...], wu_ref[...]), 0.0)
    acc_scr[...] += _dot((u * u).astype(BF16), wd_ref[...])

    @pl.when(f == pl.num_programs(1) - 1)
    def _():
        y = acc_scr[...]
        ms = jnp.mean(y * y, axis=-1, keepdims=True)
        o_ref[...] = x_ref[...] + y * lax.rsqrt(ms + RMS_EPS) * g3_ref[...]


def _ffn(x, g2, g3, wu, wd, tm, tf):
    m, d = x.shape
    f = wu.shape[1]
    return pl.pallas_call(
        _ffn_body,
        grid=(m // tm, f // tf),
        in_specs=[pl.BlockSpec((tm, d), lambda i, j: (i, 0)),
                  pl.BlockSpec((1, d), lambda i, j: (0, 0)),
                  pl.BlockSpec((1, d), lambda i, j: (0, 0)),
                  pl.BlockSpec((d, tf), lambda i, j: (0, j)),
                  pl.BlockSpec((tf, d), lambda i, j: (j, 0))],
        out_specs=pl.BlockSpec((tm, d), lambda i, j: (i, 0)),
        out_shape=jax.ShapeDtypeStruct((m, d), F32),
        scratch_shapes=[pltpu.VMEM((tm, d), BF16), pltpu.VMEM((tm, d), F32)],
        compiler_params=_cparams(("parallel", "arbitrary")),
        name="ffn",
    )(x, g2, g3, wu, wd)


MOBA_HEAD_GROUP = 4


def _moba_prompt_body(q_ref, k_ref, v_ref, o_ref, kmean_scr, kext, vext, *, nb):
    blk = MOBA_BLOCK
    hg = MOBA_HEAD_GROUP
    qt = pl.program_id(2)
    pair_of = lambda j: slice(LANE * (j // 2), LANE * (j // 2 + 1))
    dst_of = lambda j: slice(LANE * j, LANE * (j + 1))

    @pl.when(qt == 0)
    def _():
        kmean_scr[...] = jnp.zeros_like(kmean_scr)
        mine_t = lax.broadcasted_iota(I32, (k_ref.shape[0], LANE), 1) // HEAD_DIM
        for j in range(hg):
            kp = jnp.where(mine_t == j % 2, k_ref[:, pair_of(j)], 0.0)
            kext[:, dst_of(j)] = kp.astype(BF16)
            vext[:, dst_of(j)] = jnp.where(mine_t == j % 2, v_ref[:, pair_of(j)], 1.0).astype(BF16)
            for n in range(nb):
                kmean_scr[n:n + 1, dst_of(j)] = jnp.sum(kp[n * blk:(n + 1) * blk], axis=0, keepdims=True) * (1.0 / blk)

    row = lax.broadcasted_iota(I32, (blk, blk), 0)
    col = lax.broadcasted_iota(I32, (blk, blk), 1)
    bias_own = jnp.where(col <= row, 0.0, NEG_INF)
    nb_r = kmean_scr.shape[0]
    blk_id = lax.broadcasted_iota(I32, (nb_r, blk), 0)
    eligible = blk_id < qt
    not_sel, qb = [], []
    for j in range(hg):
        q = q_ref[:, pair_of(j)]
        gate = jnp.where(eligible, _dot_nt(kmean_scr[:, dst_of(j)], q, HI), -jnp.inf)
        rank = jnp.zeros((nb_r, blk), I32)
        for m in range(nb):
            gm = gate[m:m + 1, :]
            rank += ((gm > gate) | ((gm == gate) & (m < blk_id))).astype(I32)
        ns_t = jnp.where(eligible & (rank < MOBA_TOPK), 0.0, 1.0)
        not_sel.append(jnp.concatenate([ns_t, jnp.ones((LANE - nb_r, blk), F32)], axis=0).T.astype(BF16))
        qb.append((q * ATTN_SCALE).astype(BF16))

    def attend(carry, start, biases):
        out = []
        for j in range(hg):
            m_i, acc = carry[j]
            s = _dot_nt(qb[j], kext[pl.ds(start, blk), dst_of(j)]) + biases[j]
            m_new = jnp.maximum(m_i, jnp.max(s, axis=-1, keepdims=True))
            p = jnp.exp(s - m_new)
            out.append((m_new, jnp.exp(m_i - m_new) * acc + _dot(p.astype(BF16), vext[pl.ds(start, blk), dst_of(j)])))
        return tuple(out)

    def body(n, carry):
        pick = (lax.broadcasted_iota(I32, (LANE, blk), 0) == n).astype(BF16)
        return attend(carry, pl.multiple_of(n * blk, blk), [_dot(ns, pick) * NEG_INF for ns in not_sel])

    init = tuple((jnp.full((blk, 1), NEG_INF, F32), jnp.zeros((blk, LANE), F32)) for _ in range(hg))
    carry = lax.fori_loop(0, qt, body, init)
    fin = attend(carry, pl.multiple_of(qt * blk, blk), [bias_own] * hg)
    half = lax.broadcasted_iota(I32, (blk, LANE), 1) // HEAD_DIM
    for jp in range(hg // 2):
        outs = [fin[2 * jp + e][1] / pltpu.roll(fin[2 * jp + e][1], HEAD_DIM, 1) for e in range(2)]
        o_ref[:, dst_of(jp)] = jnp.where(half == 0, outs[0], outs[1]).astype(o_ref.dtype)


def _moba_prompt(cols, b, t):
    blk = MOBA_BLOCK
    nb = t // blk
    w = MOBA_HEAD_GROUP * HEAD_DIM
    qoff = OFF_MOBA // w
    koff = (OFF_MOBA + GROUP_WIDTH) // w
    voff = (OFF_MOBA + 2 * GROUP_WIDTH) // w
    return pl.pallas_call(
        functools.partial(_moba_prompt_body, nb=nb),
        grid=(b, GROUP_WIDTH // w, nb),
        in_specs=[pl.BlockSpec((blk, w), lambda i, h, q: (i * nb + q, qoff + h)),
                  pl.BlockSpec((t, w), lambda i, h, q: (i, koff + h)),
                  pl.BlockSpec((t, w), lambda i, h, q: (i, voff + h))],
        out_specs=pl.BlockSpec((blk, w), lambda i, h, q: (i * nb + q, h)),
        out_shape=jax.ShapeDtypeStruct((b * t, GROUP_WIDTH), BF16),
        scratch_shapes=[pltpu.VMEM((-(-nb // 8) * 8, MOBA_HEAD_GROUP * LANE), F32),
                        pltpu.VMEM((t, MOBA_HEAD_GROUP * LANE), BF16), pltpu.VMEM((t, MOBA_HEAD_GROUP * LANE), BF16)],
        compiler_params=_cparams(("parallel", "parallel", "arbitrary")),
        name="moba_prompt",
    )(cols, cols, cols)


def _gelu_tanh(x):
    return x * (0.5 * (1.0 + jnp.tanh(np.sqrt(2.0 / np.pi).astype(np.float32) * (x + 0.044715 * (x * x * x)))))


def _cmp_bias(pe_ref, w1_ref, b1_ref, kv):
    pe8 = jnp.broadcast_to(pe_ref[kv], (8, CMP_LEN * HEAD_DIM))
    return _dot(pe8, w1_ref[kv], HI)[0:1, :] + b1_ref[kv]


def _compress(load_rows, n, w1r_ref, kv, bias, w2):
    acc = [jnp.zeros((n, 2 * CMP_HIDDEN), F32) for _ in range(NSA_KV_HEADS)]
    for r in range(CMP_STRIDE):
        x = load_rows(r).astype(BF16)
        for g in range(NSA_KV_HEADS):
            acc[g] += _dot(x[:, g * HEAD_DIM:(g + 1) * HEAD_DIM], w1r_ref[kv, r])
    out = []
    for g in range(NSA_KV_HEADS):
        hid = acc[g][:, :CMP_HIDDEN] + pltpu.roll(acc[g][:, CMP_HIDDEN:], n - 1, 0) + bias
        out.append(_dot(_gelu_tanh(hid).astype(BF16), w2.astype(BF16)))
    return out


def _softmax_rows(l, mask):
    m = jnp.max(jnp.where(mask, l, NEG_INF), axis=-1, keepdims=True)
    e = jnp.where(mask, jnp.exp(l - m), 0.0)
    s = jnp.sum(e, axis=-1, keepdims=True)
    return jnp.where(s > 0.0, e / jnp.where(s > 0.0, s, 1.0), 0.0)


def _slc_matrix(n_cmp_pad, n_cmp, n_slc_pad):
    n = lax.broadcasted_iota(I32, (n_cmp_pad, n_slc_pad), 0)
    j = lax.broadcasted_iota(I32, (n_cmp_pad, n_slc_pad), 1)
    ratio = SLC_BLOCK // CMP_STRIDE
    return ((n >= ratio * j - 1) & (n <= ratio * j + ratio - 1) & (n < n_cmp)).astype(F32)


def _topk_rows(score, ids, n_cand, k):
    rank = jnp.zeros(score.shape, I32)
    for m in range(n_cand):
        sm = score[:, m:m + 1]
        rank += ((sm > score) | ((sm == score) & (m < ids))).astype(I32)
    return rank < k


def _nsa_prompt_body(q0_ref, q1_ref, kc_ref, vc_ref, ksvs_ref, kwvw_ref, gate_ref, w1r_ref, w1_ref, b1_ref, w2_ref,
                     pe_ref, o_ref, ck_scr, cv_scr, ks_ext, vs_ext, kw_ext, vw_ext, *, t):
    tq = 256
    n_chunk = t // CMP_STRIDE
    n_cmp = n_chunk - CMP_LEN // CMP_STRIDE + 1
    qt = pl.program_id(1)
    hd = HEAD_DIM

    @pl.when(qt == 0)
    def _():
        for kv, (src, dst) in enumerate(((kc_ref, ck_scr), (vc_ref, cv_scr))):
            bias = _cmp_bias(pe_ref, w1_ref, b1_ref, kv)
            out = _compress(lambda r: src[pl.ds(r, n_chunk, stride=CMP_STRIDE), :],
                            n_chunk, w1r_ref, kv, bias, w2_ref[kv])
            for g in range(NSA_KV_HEADS):
                dst[g] = out[g]
        half_t = lax.broadcasted_iota(I32, (t, LANE), 1) // hd
        for src, k_ext, v_ext in ((ksvs_ref, ks_ext, vs_ext), (kwvw_ref, kw_ext, vw_ext)):
            for g in range(NSA_KV_HEADS):
                k_ext[g] = jnp.where(half_t == g, src[:, 0:LANE], 0.0).astype(BF16)
                v_ext[g] = jnp.where(half_t == g, src[:, LANE:2 * LANE], 1.0).astype(BF16)

    tpos = qt * tq + lax.broadcasted_iota(I32, (tq, 1), 0)
    tpos4 = jnp.concatenate([tpos] * NSA_GROUP, axis=0)
    gates = _sigmoid(gate_ref[...])
    slc_ids = lax.broadcasted_iota(I32, (tq, LANE), 1)
    n_slc = t // SLC_BLOCK
    kcol = lax.broadcasted_iota(I32, (tq, tq), 1)
    jrow = lax.broadcasted_iota(I32, (LANE, tq), 0)
    kcol_e = lax.broadcasted_iota(I32, (LANE, tq), 1)
    for g in range(NSA_KV_HEADS):
        q_ref = q0_ref if g == 0 else q1_ref
        q4f = jnp.concatenate([q_ref[:, j * hd:(j + 1) * hd] for j in range(NSA_GROUP)], axis=0)
        q4 = q4f.astype(BF16)
        cmp_end = lax.broadcasted_iota(I32, (NSA_GROUP * tq, n_chunk), 1) * CMP_STRIDE + (CMP_LEN - 1)
        l_cmp = _dot_nt(q4, ck_scr[g].astype(BF16)) * ATTN_SCALE
        p_cmp = _softmax_rows(l_cmp, cmp_end <= tpos4)
        o_cmp = _dot(p_cmp.astype(BF16), cv_scr[g].astype(BF16))
        imp = p_cmp[0:tq]
        for j in range(1, NSA_GROUP):
            imp = imp + p_cmp[j * tq:(j + 1) * tq]
        n_slc_r = -(-n_slc // 8) * 8
        jn = lax.broadcasted_iota(I32, (n_slc_r, n_chunk), 0)
        nn = lax.broadcasted_iota(I32, (n_slc_r, n_chunk), 1)
        ratio = SLC_BLOCK // CMP_STRIDE
        slc_t = ((nn >= ratio * jn - 1) & (nn <= ratio * jn + ratio - 1) & (nn < n_cmp)).astype(F32)
        imp_t = _dot_nt(slc_t, imp, HI)
        jt = lax.broadcasted_iota(I32, (n_slc_r, tq), 0)
        cur_t = (qt * tq + lax.broadcasted_iota(I32, (n_slc_r, tq), 1)) // SLC_BLOCK
        eligible = (jt <= cur_t) & (jt < n_slc)
        forced = (jt == 0) | (jt == cur_t) | (jt == cur_t - 1)
        score = jnp.where(eligible, jnp.where(forced, jnp.inf, imp_t), -jnp.inf)
        rank = jnp.zeros((n_slc_r, tq), I32)
        for m in range(n_slc):
            sm = score[m:m + 1, :]
            rank += ((sm > score) | ((sm == score) & (m < jt))).astype(I32)
        sel_t = jnp.where(eligible & (rank < SLC_TOPK), 1.0, 0.0)
        sel = jnp.concatenate([sel_t, jnp.zeros((LANE - n_slc_r, tq), F32)], axis=0).T.astype(BF16)

        half_q = lax.broadcasted_iota(I32, (tq, LANE), 1) // hd
        q4e = []
        for j in range(NSA_GROUP):
            qp = q_ref[:, (j // 2) * LANE:(j // 2 + 1) * LANE]
            qp = qp if j % 2 == g else pltpu.roll(qp, hd, 1)
            q4e.append(jnp.where(half_q == g, qp * ATTN_SCALE, 0.0).astype(BF16))
        q4e = jnp.concatenate(q4e, axis=0)

        def attend(carry, start, bias, k_ext, v_ext):
            m_i, acc = carry
            s = (_dot_nt(q4e, k_ext[g, pl.ds(start, tq), :]).reshape(NSA_GROUP, tq, tq) + bias[None]
                 ).reshape(NSA_GROUP * tq, tq)
            m_new = jnp.maximum(m_i, jnp.max(s, axis=-1, keepdims=True))
            p = jnp.exp(s - m_new)
            return m_new, jnp.exp(m_i - m_new) * acc + _dot(p.astype(BF16), v_ext[g, pl.ds(start, tq), :])

        def sel_body(n, carry):
            expand = (jrow == (tq // SLC_BLOCK) * n + kcol_e // SLC_BLOCK).astype(BF16)
            mask = (_dot(sel, expand) > 0.5) & (n * tq + kcol <= tpos)
            return attend(carry, pl.multiple_of(n * tq, tq), jnp.where(mask, 0.0, NEG_INF), ks_ext, vs_ext)

        def win_body(n, carry):
            dist = tpos - (n * tq + kcol)
            bias = jnp.where((dist >= 0) & (dist < WINDOW), 0.0, NEG_INF)
            return attend(carry, pl.multiple_of(n * tq, tq), bias, kw_ext, vw_ext)

        rows = NSA_GROUP * tq
        init = (jnp.full((rows, 1), NEG_INF, F32), jnp.zeros((rows, LANE), F32))
        _, acc_s = lax.fori_loop(0, qt + 1, sel_body, init)
        _, acc_w = lax.fori_loop(jnp.maximum(qt - (WINDOW // tq), 0), qt + 1, win_body, init)
        o_sel = (acc_s / pltpu.roll(acc_s, hd, 1))[:, g * hd:(g + 1) * hd]
        o_win = (acc_w / pltpu.roll(acc_w, hd, 1))[:, g * hd:(g + 1) * hd]
        for j in range(NSA_GROUP):
            h = g * NSA_GROUP + j
            rs = slice(j * tq, (j + 1) * tq)
            o = (gates[:, h:h + 1] * o_cmp[rs] + gates[:, N_HEADS + h:N_HEADS + h + 1] * o_sel[rs]
                 + gates[:, 2 * N_HEADS + h:2 * N_HEADS + h + 1] * o_win[rs])
            o_ref[:, h * hd:(h + 1) * hd] = o.astype(o_ref.dtype)


def _nsa_prompt(cols, cmp_w, b, t):
    w1r, w1, b1, w2, pe = cmp_w
    tq = 256
    nq = t // tq
    n_chunk = t // CMP_STRIDE
    qoff = OFF_NSAQ // 256
    kvoff = OFF_NSAKV // 256
    full = lambda a: pl.BlockSpec(a.shape, lambda i, q: (0,) * a.ndim)
    return pl.pallas_call(
        functools.partial(_nsa_prompt_body, t=t),
        grid=(b, nq),
        in_specs=[pl.BlockSpec((tq, 256), lambda i, q: (i * nq + q, qoff)),
                  pl.BlockSpec((tq, 256), lambda i, q: (i * nq + q, qoff + 1)),
                  pl.BlockSpec((t, LANE), lambda i, q: (i, 2 * kvoff)),
                  pl.BlockSpec((t, LANE), lambda i, q: (i, 2 * kvoff + 1)),
                  pl.BlockSpec((t, 256), lambda i, q: (i, kvoff + 1)),
                  pl.BlockSpec((t, 256), lambda i, q: (i, kvoff + 2)),
                  pl.BlockSpec((tq, LANE), lambda i, q: (i * nq + q, OFF_GATE // LANE)),
                  full(w1r), full(w1), full(b1), full(w2), full(pe)],
        out_specs=pl.BlockSpec((tq, GROUP_WIDTH), lambda i, q: (i * nq + q, 0)),
        out_shape=jax.ShapeDtypeStruct((b * t, GROUP_WIDTH), BF16),
        scratch_shapes=[pltpu.VMEM((NSA_KV_HEADS, n_chunk, HEAD_DIM), F32),
                        pltpu.VMEM((NSA_KV_HEADS, n_chunk, HEAD_DIM), F32)]
                       + [pltpu.VMEM((NSA_KV_HEADS, t, LANE), BF16)] * 4,
        compiler_params=_cparams(("parallel", "arbitrary")),
        name="nsa_prompt",
    )(cols, cols, cols, cols, cols, cols, cols, w1r, w1, b1, w2, pe)


def _rope(x, cos, sin):
    half = HEAD_DIM // 2
    lane = lax.broadcasted_iota(I32, x.shape, 1) % HEAD_DIM
    nxt = pltpu.roll(x, x.shape[1] - half, 1)
    prv = pltpu.roll(x, half, 1)
    return x * cos + jnp.where(lane < half, -nxt, prv) * sin


def _head_norm(y, g, b, eps):
    avg = _head_blockdiag(y.shape[1], 1.0 / HEAD_DIM)
    mu = _dot(y, avg, HI)
    d = y - mu
    var = _dot(d * d, avg, HI)
    return d * lax.rsqrt(var + eps) * g + b


def _ret_prompt_body(q_ref, k_ref, v_ref, gate_ref, cos_ref, sin_ref, dmask_ref, xi_ref, zeta_ref, cd_ref,
                     gn_g_ref, gn_b_ref, s0_ref, o_ref, s_ref, o_scr):
    @pl.when(pl.program_id(1) == 0)
    def _():
        s_ref[...] = s0_ref[...]

    cos = cos_ref[...]
    sin = sin_ref[...]
    q = _rope(q_ref[...], cos, sin)
    k = _rope(k_ref[...], cos, sin) * ATTN_SCALE
    kz = (k * zeta_ref[...]).astype(BF16)
    qb = q.astype(BF16)
    kb = k.astype(BF16)
    for h in range(N_HEADS):
        sl = slice(h * HEAD_DIM, (h + 1) * HEAD_DIM)
        vb = v_ref[:, sl].astype(BF16)
        s = s_ref[0, h]
        att = _dot_nt(qb[:, sl], kb[:, sl]) * dmask_ref[h]
        o_scr[:, sl] = _dot(att.astype(BF16), vb) + _dot(qb[:, sl], s.astype(BF16)) * xi_ref[:, sl]
        s_ref[0, h] = s * cd_ref[:, sl] + _dot_tn(kz[:, sl], vb)
    gate = gate_ref[...]
    y = _head_norm(o_scr[...], gn_g_ref[...], gn_b_ref[...], GN_EPS)
    o_ref[...] = (gate * _sigmoid(gate) * y).astype(o_ref.dtype)


def _ret_tables(log_gamma, c):
    idx = jnp.arange(c, dtype=F32)
    diff = idx[:, None] - idx[None, :]
    dmask = jnp.where(diff >= 0, jnp.exp(jnp.maximum(diff, 0.0)[None] * log_gamma[:, None, None]), 0.0)
    rep = lambda z: jnp.repeat(z, HEAD_DIM, axis=-1)
    xi = rep(jnp.exp((idx + 1.0)[:, None] * log_gamma[None, :]))
    zeta = rep(jnp.exp((c - 1.0 - idx)[:, None] * log_gamma[None, :]))
    cd = rep(jnp.exp(c * log_gamma)[None, :])
    return dmask, xi, zeta, cd


def _rope_tables(pos):
    half = HEAD_DIM // 2
    inv = ROPE_BASE ** (-jnp.arange(half, dtype=F32) / half)
    ang = pos.astype(F32)[:, None] * inv[None, :]
    tile = lambda z: jnp.tile(z, (1, 2 * N_HEADS))
    return tile(jnp.cos(ang)), tile(jnp.sin(ang))


def _ret_prompt(cols, s0, tables, rope, gn_g, gn_b, b, t):
    c = RET_CHUNK
    nc = t // c
    gw = GROUP_WIDTH
    off = OFF_RET // gw
    dmask, xi, zeta, cd = tables
    cos, sin = rope
    col_spec = lambda j: pl.BlockSpec((c, gw), lambda i, n: (i * nc + n, off + j))
    full = lambda a: pl.BlockSpec(a.shape, lambda i, n: (0,) * a.ndim)
    st_spec = pl.BlockSpec((1, N_HEADS, HEAD_DIM, HEAD_DIM), lambda i, n: (i, 0, 0, 0))
    return pl.pallas_call(
        _ret_prompt_body,
        grid=(b, nc),
        in_specs=[col_spec(0), col_spec(1), col_spec(2), col_spec(3),
                  pl.BlockSpec((c, gw), lambda i, n: (n, 0)), pl.BlockSpec((c, gw), lambda i, n: (n, 0)),
                  full(dmask), full(xi), full(zeta), full(cd), full(gn_g), full(gn_b), st_spec],
        out_specs=[pl.BlockSpec((c, gw), lambda i, n: (i * nc + n, 0)), st_spec],
        out_shape=[jax.ShapeDtypeStruct((b * t, gw), BF16),
                   jax.ShapeDtypeStruct((b, N_HEADS, HEAD_DIM, HEAD_DIM), F32)],
        scratch_shapes=[pltpu.VMEM((c, gw), F32)],
        compiler_params=_cparams(("parallel", "arbitrary")),
        name="ret_prompt",
    )(cols, cols, cols, cols, cos, sin, dmask, xi, zeta, cd, gn_g, gn_b, s0)


def _softplus(x):
    return jnp.maximum(x, 0.0) + jnp.log(1.0 + jnp.exp(-jnp.abs(x)))


def _rwkv_prep(c, prev, p):
    mu, w0, w2, a0, a2, g2, k_k, k_a = p
    gw = GROUP_WIDTH
    mixed = c + (prev - c) * mu
    r, k, v = mixed[:, :gw], mixed[:, gw:2 * gw], mixed[:, 2 * gw:3 * gw]
    o1 = 3 * gw
    o2 = o1 + RWKV_DECAY_RANK
    o3 = o2 + RWKV_AAA_RANK
    xw, xa, xg = mixed[:, o1:o2], mixed[:, o2:o3], mixed[:, o3:]
    w_log = -_softplus(-(w0 + _dot(jnp.tanh(xw), w2, HI))) - 0.5
    decay = jnp.exp(-jnp.exp(w_log))
    a = _sigmoid(a0 + _dot(xa, a2, HI))
    gate = _dot(_sigmoid(xg).astype(BF16), g2.astype(BF16))
    kk = k * k_k
    norm = jnp.sqrt(_dot(kk * kk, _head_blockdiag(gw), HI))
    kk = kk / jnp.maximum(norm, 1e-12)
    k = k * (1.0 + (a - 1.0) * k_a)
    return r, decay, k, v, kk, kk * a, gate


def _rwkv_steps(vecs, get_state, put_state, n_steps):
    r8, w8, k8, v8, kk8, ka8 = vecs
    lane = lax.broadcasted_iota(I32, (HEAD_DIM, LANE), 1)
    pad = jnp.zeros((LANE - 8, LANE), F32)
    ys = []
    for hp in range(N_HEADS // 2):
        vt = jnp.concatenate([v8[:, hp * LANE:(hp + 1) * LANE], pad], axis=0).T
        yts = []
        for h2 in range(2):
            h = 2 * hp + h2
            sl = slice(h * HEAD_DIM, (h + 1) * HEAD_DIM)
            yt = jnp.zeros((HEAD_DIM, LANE), F32)
            s = get_state(h, 0)
            for j in range(8):
                if n_steps == 1:
                    s = get_state(h, j)
                vcol = vt[h2 * HEAD_DIM:(h2 + 1) * HEAD_DIM, j:j + 1]
                sa = jnp.sum(s * kk8[j:j + 1, sl], axis=-1, keepdims=True)
                s = s * w8[j:j + 1, sl] - sa * ka8[j:j + 1, sl] + vcol * k8[j:j + 1, sl]
                ycol = jnp.sum(s * r8[j:j + 1, sl], axis=-1, keepdims=True)
                yt = jnp.where(lane == j, ycol, yt)
                if n_steps == 1:
                    put_state(h, j, s)
            if n_steps != 1:
                put_state(h, 0, s)
            yts.append(yt)
        ys.append(jnp.concatenate(yts, axis=0).T[0:8, :])
    return jnp.concatenate(ys, axis=1)


def _rwkv_post(ys, r, k, v, gate, r_k, ln_g, ln_b):
    y = _head_norm(ys, ln_g, ln_b, RWKV_LN_EPS)
    y = y + _dot(r * k * r_k, _head_blockdiag(GROUP_WIDTH), HI) * v
    return y * gate


def _rwkv_prep_body(c_ref, shift_ref, mu_ref, w0_ref, w2_ref, a0_ref, a2_ref, g2_ref, kk_ref, ka_ref, rk_ref,
                    keys_o, v_o, bonus_o, gate_o, carry):
    tc = c_ref.shape[0]

    @pl.when(pl.program_id(1) == 0)
    def _():
        carry[...] = shift_ref[...]

    c = c_ref[...]
    row = lax.broadcasted_iota(I32, c.shape, 0)
    prev = jnp.where(row == 0, carry[...], pltpu.roll(c, 1, 0))
    carry[...] = c[tc - 1:tc, :]
    p = (mu_ref[...], w0_ref[...], w2_ref[...], a0_ref[...], a2_ref[...], g2_ref[...], kk_ref[...], ka_ref[...])
    r, w, k, v, kk, kka, gate = _rwkv_prep(c, prev, p)
    for i, z in enumerate((w, kk, kka, k, r)):
        keys_o[i] = z.T
    v_o[...] = v.T
    bonus_o[...] = _dot(r * k * rk_ref[...], _head_blockdiag(GROUP_WIDTH), HI) * v
    gate_o[...] = gate


def _rwkv_relayout_body(*refs):
    x_refs, o_ref, scr = refs[:-2], refs[-2], refs[-1]
    tt = x_refs[0].shape[-1]
    n_rep = LANE // (len(x_refs) * N_HEADS)
    for k in range(HEAD_DIM):
        rows = [x[pl.ds(k, N_HEADS, stride=HEAD_DIM), :] for x in x_refs]
        scr[k * tt:(k + 1) * tt, :] = jnp.concatenate(rows * n_rep, axis=0).T

    def regroup(g, _):
        for u in range(8):
            tok = g * 8 + u
            for kb in range(HEAD_DIM // 8):
                o_ref[pl.ds(pl.multiple_of(tok * HEAD_DIM + kb * 8, 8), 8), :] = (
                    scr[pl.ds(kb * 8 * tt + tok, 8, stride=tt), :])
        return 0

    lax.fori_loop(0, tt // 8, regroup, 0)


def _rwkv_scan_body(w_ref, kk_ref, kka_ref, k_ref, r_ref, v_ref, s0_ref, y_ref, s_ref):
    @pl.when(pl.program_id(0) == 0)
    def _():
        s_ref[...] = s0_ref[...]

    n_vq = s_ref.shape[0]

    def step(t, _):
        vrows = v_ref[t]
        for vq in range(n_vq):
            s = s_ref[vq]
            sa = jnp.sum(s * kk_ref[t], axis=0, keepdims=True)
            s = s * w_ref[t] - sa * kka_ref[t] + vrows[vq:vq + 1, :] * k_ref[t]
            s_ref[vq] = s
            y_ref[t, vq:vq + 1, :] = jnp.sum(s * r_ref[t], axis=0, keepdims=True)
        return 0

    lax.fori_loop(0, v_ref.shape[0], step, 0)


def _rwkv_post_body(y_ref, bonus_ref, gate_ref, lng_ref, lnb_ref, o_ref):
    y = _head_norm(y_ref[...], lng_ref[...], lnb_ref[...], RWKV_LN_EPS)
    o_ref[...] = ((y + bonus_ref[...]) * gate_ref[...]).astype(o_ref.dtype)


def _rwkv_prompt(cols, shift_prev, s0, params, b, t, tc=256, tscan=64):
    mu, w0, w2, a0, a2, g2, k_k, k_a, r_k, ln_g, ln_b = params
    nt = t // tc
    gw = GROUP_WIDTH
    hd = HEAD_DIM
    chains = b * N_HEADS
    rep = LANE // chains
    assert rep * chains == LANE and hd % rep == 0
    n_vq = hd // rep
    full = lambda a: pl.BlockSpec(a.shape, lambda i, n: (0,) * a.ndim)
    tok = pl.BlockSpec((tc, gw), lambda i, n: (i * nt + n, 0))
    vec = jax.ShapeDtypeStruct((b * t, gw), F32)
    prep_in = (mu, w0, w2, a0, a2, g2, k_k, k_a, r_k)
    keys, v, bonus, gate = pl.pallas_call(
        _rwkv_prep_body,
        grid=(b, nt),
        in_specs=[pl.BlockSpec((tc, RWKV_COLS), lambda i, n: (i * nt + n, OFF_RWKV // RWKV_COLS)),
                  pl.BlockSpec((None, 1, RWKV_COLS), lambda i, n: (i, 0, 0))] + [full(a) for a in prep_in],
        out_specs=[pl.BlockSpec((5, gw, tc), lambda i, n: (0, 0, i * nt + n)),
                   pl.BlockSpec((gw, tc), lambda i, n: (0, i * nt + n)), tok, tok],
        out_shape=[jax.ShapeDtypeStruct((5, gw, b * t), F32), jax.ShapeDtypeStruct((gw, b * t), F32), vec, vec],
        scratch_shapes=[pltpu.VMEM((1, RWKV_COLS), F32)],
        compiler_params=_cparams(("parallel", "arbitrary")),
        name="rwkv_prep",
    )(cols, shift_prev.reshape(b, 1, RWKV_COLS), *prep_in)

    tt = LANE
    ntt = t // tt
    keys_t = pl.pallas_call(
        _rwkv_relayout_body,
        grid=(5, ntt),
        in_specs=[pl.BlockSpec((None, gw, tt), lambda j, n, i=i: (j, 0, i * ntt + n)) for i in range(b)],
        out_specs=pl.BlockSpec((None, tt * hd, LANE), lambda j, n: (j, n, 0)),
        out_shape=jax.ShapeDtypeStruct((5, t * hd, LANE), F32),
        scratch_shapes=[pltpu.VMEM((hd * tt, LANE), F32)],
        compiler_params=_cparams(("parallel", "parallel")),
        name="rwkv_relayout",
    )(*([keys] * b)).reshape(5, t, hd, LANE)
    vt = jnp.transpose(v.reshape(N_HEADS, rep, n_vq, b, t), (4, 2, 1, 3, 0)).reshape(t, n_vq, LANE)
    s0t = jnp.transpose(s0.reshape(b, N_HEADS, rep, n_vq, hd), (3, 4, 2, 0, 1)).reshape(n_vq, hd, LANE)

    ns = t // tscan
    st_spec = pl.BlockSpec((n_vq, hd, LANE), lambda n: (0, 0, 0))
    yt, st = pl.pallas_call(
        _rwkv_scan_body,
        grid=(ns,),
        in_specs=[pl.BlockSpec((None, tscan, hd, LANE), lambda n, j=j: (j, n, 0, 0)) for j in range(5)]
                 + [pl.BlockSpec((tscan, n_vq, LANE), lambda n: (n, 0, 0)), st_spec],
        out_specs=[pl.BlockSpec((tscan, n_vq, LANE), lambda n: (n, 0, 0)), st_spec],
        out_shape=[jax.ShapeDtypeStruct((t, n_vq, LANE), F32), jax.ShapeDtypeStruct((n_vq, hd, LANE), F32)],
        compiler_params=_cparams(("arbitrary",)),
        name="rwkv_scan",
    )(*([keys_t] * 5), vt, s0t)
    ys = jnp.transpose(yt.reshape(t, n_vq, rep, b, N_HEADS), (3, 0, 4, 2, 1)).reshape(b * t, gw)
    s_fin = jnp.transpose(st.reshape(n_vq, hd, rep, b, N_HEADS), (3, 4, 2, 0, 1)).reshape(b, N_HEADS, hd, hd)

    tm = min(512, b * t)
    tokm = pl.BlockSpec((tm, gw), lambda i: (i, 0))
    o = pl.pallas_call(
        _rwkv_post_body,
        grid=(b * t // tm,),
        in_specs=[tokm, tokm, tokm, pl.BlockSpec((1, gw), lambda i: (0, 0)), pl.BlockSpec((1, gw), lambda i: (0, 0))],
        out_specs=tokm,
        out_shape=jax.ShapeDtypeStruct((b * t, gw), BF16),
        compiler_params=_cparams(("parallel",)),
        name="rwkv_post",
    )(ys, bonus, gate, ln_g, ln_b)
    return o, s_fin


PAGES_PER_STEP = 8


def _past_stats_body(pt_ref, *refs, n_pages):
    pps = PAGES_PER_STEP
    kt_refs, kct_refs, vct_refs = refs[:pps], refs[pps:2 * pps], refs[2 * pps:3 * pps]
    (q_ref, w1r_ref, w1_ref, b1_ref, w2_ref, pe_ref, top_ref, cmp_ref, qb, gsum, kc_rows, vc_rows) = refs[3 * pps:]
    step = pl.program_id(1)
    pages_per_blk = MOBA_BLOCK // PAGE_SIZE
    nb = n_pages // pages_per_blk
    n_chunk = n_pages * PAGE_SIZE // CMP_STRIDE

    @pl.when(step == 0)
    def _():
        gsum[...] = jnp.zeros_like(gsum)
        qb[...] = jnp.broadcast_to(q_ref[...], qb.shape)

    for i in range(pps):
        p = step * pps + i
        prod = (kt_refs[i][...] * qb[...]).reshape(N_HEADS, HEAD_DIM // 8, 8, PAGE_SIZE)
        gsum[p // pages_per_blk] += jnp.sum(prod, axis=1)
        row0 = pl.multiple_of(p * PAGE_SIZE, PAGE_SIZE)
        kc_rows[pl.ds(row0, PAGE_SIZE), :] = kct_refs[i][...].reshape(LANE, PAGE_SIZE).T
        vc_rows[pl.ds(row0, PAGE_SIZE), :] = vct_refs[i][...].reshape(LANE, PAGE_SIZE).T

    @pl.when(step == n_pages // pps - 1)
    def _():
        gate = jnp.sum(jnp.sum(gsum[...], axis=2), axis=-1) * (1.0 / MOBA_BLOCK)
        ids = lax.broadcasted_iota(I32, (nb, N_HEADS), 0)
        rows8 = lax.broadcasted_iota(I32, (8, N_HEADS), 0)
        top = jnp.zeros((8, N_HEADS), I32)
        for j in range(MOBA_TOPK):
            best = jnp.max(gate, axis=0, keepdims=True)
            arg = jnp.min(jnp.where(gate == best, ids, nb), axis=0, keepdims=True)
            top = jnp.where(rows8 == j, arg, top)
            gate = jnp.where(ids == arg, -jnp.inf, gate)
        top_ref[...] = top
        for kv, src in enumerate((kc_rows, vc_rows)):
            bias = _cmp_bias(pe_ref, w1_ref, b1_ref, kv)
            out = _compress(lambda r: src[pl.ds(r, n_chunk, stride=CMP_STRIDE), :], n_chunk, w1r_ref, kv, bias,
                            w2_ref[kv])
            for g in range(NSA_KV_HEADS):
                cmp_ref[kv * NSA_KV_HEADS + g] = out[g]


def _past_stats(page_table, moba_t, nsa_t, q_col, cmp_w, layer):
    w1r, w1, b1, w2, pe = cmp_w
    bs, n_pages = page_table.shape
    nb = n_pages * PAGE_SIZE // MOBA_BLOCK
    rows = n_pages * PAGE_SIZE
    n_chunk = rows // CMP_STRIDE
    pps = PAGES_PER_STEP
    assert n_pages % pps == 0
    full = lambda a: pl.BlockSpec(a.shape, lambda i, p, pt: (0,) * a.ndim)
    k_page = lambda j: pl.BlockSpec((None, None, None, N_HEADS, HEAD_DIM, PAGE_SIZE),
                                    lambda i, p, pt: (layer, pt[i, p * pps + j], 0, 0, 0, 0))
    nsa_page = lambda t, j: pl.BlockSpec((None, None, None, NSA_KV_HEADS, HEAD_DIM, PAGE_SIZE),
                                         lambda i, p, pt: (layer, pt[i, p * pps + j], t, 0, 0, 0))
    return pl.pallas_call(
        functools.partial(_past_stats_body, n_pages=n_pages),
        grid_spec=pltpu.PrefetchScalarGridSpec(
            num_scalar_prefetch=1,
            grid=(bs, n_pages // pps),
            in_specs=[k_page(j) for j in range(pps)] + [nsa_page(0, j) for j in range(pps)]
                     + [nsa_page(1, j) for j in range(pps)]
                     + [pl.BlockSpec((None, N_HEADS, HEAD_DIM, 1), lambda i, p, pt: (i, 0, 0, 0)),
                        full(w1r), full(w1), full(b1), full(w2), full(pe)],
            out_specs=[pl.BlockSpec((None, 8, N_HEADS), lambda i, p, pt: (i, 0, 0)),
                       pl.BlockSpec((None, 2 * NSA_KV_HEADS, n_chunk, HEAD_DIM), lambda i, p, pt: (i, 0, 0, 0))],
            scratch_shapes=[pltpu.VMEM((N_HEADS, HEAD_DIM, PAGE_SIZE), F32), pltpu.VMEM((nb, N_HEADS, 8, PAGE_SIZE), F32),
                            pltpu.VMEM((rows, LANE), F32), pltpu.VMEM((rows, LANE), F32)]),
        out_shape=[jax.ShapeDtypeStruct((bs, 8, N_HEADS), I32),
                   jax.ShapeDtypeStruct((bs, 2 * NSA_KV_HEADS, n_chunk, HEAD_DIM), F32)],
        compiler_params=_cparams(("parallel", "arbitrary")),
        name="past_stats",
    )(page_table, *([moba_t] * pps), *([nsa_t] * (2 * pps)), q_col, w1r, w1, b1, w2, pe)


def _moba_sample_body(pt_ref, top_ref, *refs):
    n_pg = (len(refs) - 4) // 2
    kt_refs, vt_refs = refs[:n_pg], refs[n_pg:2 * n_pg]
    q_ref, kn_ref, vn_ref, o_ref = refs[2 * n_pg:]
    q = q_ref[...] * ATTN_SCALE
    q8 = jnp.broadcast_to(q, (8, HEAD_DIM)).astype(BF16)
    s_all = [_dot(q8, kt[...].astype(BF16)) for kt in kt_refs]
    s_self = jnp.sum(q * kn_ref[...], axis=-1, keepdims=True)
    m = s_self
    for s in s_all:
        m = jnp.maximum(m, jnp.max(s, axis=-1, keepdims=True))
    e_self = jnp.exp(s_self - m)
    den = e_self
    acc = e_self * vn_ref[...]
    for s, vt in zip(s_all, vt_refs):
        pr = jnp.exp(s - m)
        den = den + jnp.sum(pr, axis=-1, keepdims=True)
        acc = acc + _dot_nt(pr.astype(BF16), vt[...].astype(BF16))
    o_ref[...] = (acc / den)[0:1, :]


def _moba_sample(page_table, top, moba_t, q, k_new, v_new, layer):
    bs = page_table.shape[0]
    ppb = MOBA_BLOCK // PAGE_SIZE

    def page_spec(kv, j, r):
        return pl.BlockSpec((None, None, None, None, HEAD_DIM, PAGE_SIZE),
                            lambda i, h, pt, tp: (layer, pt[i, tp[i, j, h] * ppb + r], kv, h, 0, 0))

    pages = [(j, r) for j in range(MOBA_TOPK) for r in range(ppb)]
    head = pl.BlockSpec((None, None, 1, HEAD_DIM), lambda i, h, pt, tp: (i, h, 0, 0))
    return pl.pallas_call(
        _moba_sample_body,
        grid_spec=pltpu.PrefetchScalarGridSpec(
            num_scalar_prefetch=2,
            grid=(bs, N_HEADS),
            in_specs=[page_spec(0, j, r) for j, r in pages] + [page_spec(1, j, r) for j, r in pages]
                     + [head, head, head],
            out_specs=head),
        out_shape=jax.ShapeDtypeStruct((bs, N_HEADS, 1, HEAD_DIM), F32),
        compiler_params=_cparams(("parallel", "parallel")),
        name="moba_sample",
    )(page_table, top, *([moba_t] * (2 * len(pages))), q, k_new, v_new)


def _heads_to_rows(row, g):
    parts = [row[:, (g * NSA_GROUP + j) * HEAD_DIM:(g * NSA_GROUP + j + 1) * HEAD_DIM] for j in range(NSA_GROUP)]
    return jnp.concatenate(parts + [jnp.zeros((8 - NSA_GROUP, HEAD_DIM), F32)], axis=0)


def _nsa_sample_sel_body(q_ref, cmp_ref, win_ref, kvn_ref, gate_ref, idx_ref, part_ref, *, past):
    hd = HEAD_DIM
    n_chunk = past // CMP_STRIDE
    n_slc = past // SLC_BLOCK + 1
    n_slc_pad = -(-n_slc // LANE) * LANE
    cur = past // SLC_BLOCK
    qrow = q_ref[...]
    gates = _sigmoid(gate_ref[...])
    kvn = kvn_ref[...]
    rows8 = lax.broadcasted_iota(I32, (8, 1), 0)
    ids = lax.broadcasted_iota(I32, (1, n_slc_pad), 1)
    m_iota = lax.broadcasted_iota(I32, (n_slc_pad, n_slc_pad), 0)
    j_iota = lax.broadcasted_iota(I32, (n_slc_pad, n_slc_pad), 1)
    lane = lax.broadcasted_iota(I32, (1, LANE), 1)
    idx_out = jnp.zeros((8, LANE), I32)
    for g in range(NSA_KV_HEADS):
        q8 = _heads_to_rows(qrow, g)
        q8b = q8.astype(BF16)
        n_ids = lax.broadcasted_iota(I32, (8, n_chunk), 1)
        l_cmp = _dot_nt(q8b, cmp_ref[g].astype(BF16)) * ATTN_SCALE
        p_cmp = _softmax_rows(l_cmp, n_ids * CMP_STRIDE + (CMP_LEN - 1) <= past)
        o_cmp = _dot(p_cmp.astype(BF16), cmp_ref[NSA_KV_HEADS + g].astype(BF16))
        imp = jnp.sum(jnp.where(rows8 < NSA_GROUP, p_cmp, 0.0), axis=0, keepdims=True)
        imp_slc = _dot(jnp.broadcast_to(imp, (8, n_chunk)), _slc_matrix(n_chunk, n_chunk - 1, n_slc_pad), HI)
        eligible = (ids <= cur) & (ids < n_slc)
        forced = (ids == 0) | (ids == cur) | (ids == cur - 1)
        score = jnp.where(eligible, jnp.where(forced, jnp.inf, imp_slc[0:1, :]), -jnp.inf)
        s_col = jnp.broadcast_to(score, (LANE, n_slc_pad)).T[:, 0:1]
        beats = (s_col > score) | ((s_col == score) & (m_iota < j_iota))
        rank = jnp.sum(beats.astype(F32), axis=0, keepdims=True)
        sel = jnp.where(eligible & (rank < SLC_TOPK), 1.0, 0.0)
        sel_col = jnp.broadcast_to(sel, (LANE, n_slc_pad)).T[:, 0:1]
        before = jnp.sum(jnp.where(m_iota < j_iota, sel_col, 0.0), axis=0, keepdims=True)
        idx_row = jnp.zeros((1, LANE), I32)
        for i in range(SLC_TOPK):
            hit = (sel > 0.5) & (before == i)
            idx_i = jnp.sum(jnp.where(hit, ids, 0), axis=-1, keepdims=True)
            idx_row = jnp.where(lane == i, idx_i, idx_row)
        idx_out = jnp.where(lax.broadcasted_iota(I32, (8, LANE), 0) == g, idx_row, idx_out)
        kw = win_ref[0, g].astype(BF16)
        vw = win_ref[1, g].astype(BF16)
        n_buf = win_ref.shape[-1]
        w_ids = lax.broadcasted_iota(I32, (8, n_buf), 1)
        l_win = _dot(q8b, kw) * ATTN_SCALE
        w_mask = w_ids >= n_buf - (WINDOW - 1)
        kw_new = kvn[:, 4 * NSA_KV_WIDTH + g * hd:4 * NSA_KV_WIDTH + (g + 1) * hd]
        vw_new = kvn[:, 5 * NSA_KV_WIDTH + g * hd:5 * NSA_KV_WIDTH + (g + 1) * hd]
        s_self = jnp.sum(q8 * kw_new, axis=-1, keepdims=True) * ATTN_SCALE
        m = jnp.maximum(jnp.max(jnp.where(w_mask, l_win, NEG_INF), axis=-1, keepdims=True), s_self)
        e = jnp.where(w_mask, jnp.exp(l_win - m), 0.0)
        e_self = jnp.exp(s_self - m)
        o_win = (_dot_nt(e.astype(BF16), vw) + e_self * vw_new) / (jnp.sum(e, axis=-1, keepdims=True) + e_self)
        for j in range(NSA_GROUP):
            h = g * NSA_GROUP + j
            part_ref[:, h * hd:(h + 1) * hd] = (gates[:, h:h + 1] * o_cmp[j:j + 1]
                                                + gates[:, 2 * N_HEADS + h:2 * N_HEADS + h + 1] * o_win[j:j + 1])
    idx_ref[...] = idx_out


def _nsa_sample_sel(q, cmp_tok, win_state, kv_new, gates, layer, past):
    bs = q.shape[0]
    n_buf = win_state.shape[-1]
    n_chunk = cmp_tok.shape[2]
    return pl.pallas_call(
        functools.partial(_nsa_sample_sel_body, past=past),
        grid=(bs,),
        in_specs=[pl.BlockSpec((None, 1, GROUP_WIDTH), lambda i: (i, 0, 0)),
                  pl.BlockSpec((None, 2 * NSA_KV_HEADS, n_chunk, HEAD_DIM), lambda i: (i, 0, 0, 0)),
                  pl.BlockSpec((None, None, 2, NSA_KV_HEADS, HEAD_DIM, n_buf), lambda i: (layer, i, 0, 0, 0, 0)),
                  pl.BlockSpec((None, 1, 6 * NSA_KV_WIDTH), lambda i: (i, 0, 0)),
                  pl.BlockSpec((None, 1, LANE), lambda i: (i, 0, 0))],
        out_specs=[pl.BlockSpec((None, 8, LANE), lambda i: (i, 0, 0)),
                   pl.BlockSpec((None, 1, GROUP_WIDTH), lambda i: (i, 0, 0))],
        out_shape=[jax.ShapeDtypeStruct((bs, 8, LANE), I32), jax.ShapeDtypeStruct((bs, 1, GROUP_WIDTH), F32)],
        compiler_params=_cparams(("parallel",)),
        name="nsa_sample_sel",
    )(q, cmp_tok, win_state, kv_new, gates)


NSA_BLOCKS_PER_STEP = 4


def _nsa_sample_attn_body(pt_ref, idx_ref, *refs, n_past_blk):
    bps = NSA_BLOCKS_PER_STEP
    n_pg = NSA_KV_HEADS * bps
    ks_refs, vs_refs = refs[:n_pg], refs[n_pg:2 * n_pg]
    q_ref, kvn_ref, gate_ref, part_ref, o_ref, m_s, l_s, acc_s = refs[2 * n_pg:]
    b, i = pl.program_id(0), pl.program_id(1)
    hd = HEAD_DIM
    qrow = q_ref[...]
    kvn = kvn_ref[...]
    half = lax.broadcasted_iota(I32, (8, PAGE_SIZE), 1) // SLC_BLOCK
    for g in range(NSA_KV_HEADS):
        q8 = _heads_to_rows(qrow, g)
        ks_new = kvn[:, 2 * NSA_KV_WIDTH + g * hd:2 * NSA_KV_WIDTH + (g + 1) * hd]
        vs_new = kvn[:, 3 * NSA_KV_WIDTH + g * hd:3 * NSA_KV_WIDTH + (g + 1) * hd]

        @pl.when(i == 0)
        def _():
            m_s[g] = jnp.broadcast_to(jnp.sum(q8 * ks_new, axis=-1, keepdims=True) * ATTN_SCALE, (8, hd))
            l_s[g] = jnp.ones((8, hd), F32)
            acc_s[g] = jnp.broadcast_to(vs_new, (8, hd))

        q8b = (q8 * ATTN_SCALE).astype(BF16)
        m_i = m_s[g][:, 0:1]
        l_i = l_s[g][:, 0:1]
        acc = acc_s[g]
        for u in range(bps):
            blk = idx_ref[b, g, i * bps + u]
            mask = half == jnp.where(blk < n_past_blk, blk % (PAGE_SIZE // SLC_BLOCK), -1)
            s = jnp.where(mask, _dot(q8b, ks_refs[g * bps + u][...].astype(BF16)), NEG_INF)
            m_new = jnp.maximum(m_i, jnp.max(s, axis=-1, keepdims=True))
            pr = jnp.where(mask, jnp.exp(s - m_new), 0.0)
            alpha = jnp.exp(m_i - m_new)
            l_i = alpha * l_i + jnp.sum(pr, axis=-1, keepdims=True)
            acc = alpha * acc + _dot_nt(pr.astype(BF16), vs_refs[g * bps + u][...].astype(BF16))
            m_i = m_new
        l_s[g] = jnp.broadcast_to(l_i, (8, hd))
        acc_s[g] = acc
        m_s[g] = jnp.broadcast_to(m_i, (8, hd))

    @pl.when(i == pl.num_programs(1) - 1)
    def _():
        gates = _sigmoid(gate_ref[...])
        for g in range(NSA_KV_HEADS):
            o_sel = acc_s[g] / l_s[g]
            for j in range(NSA_GROUP):
                h = g * NSA_GROUP + j
                sl = slice(h * hd, (h + 1) * hd)
                o_ref[:, sl] = (part_ref[:, sl] + gates[:, N_HEADS + h:N_HEADS + h + 1] * o_sel[j:j + 1]
                                ).astype(o_ref.dtype)


def _nsa_sample_attn(page_table, sel_idx, nsa_cache, q, kv_new, gates, part, layer, past):
    bs = page_table.shape[0]
    n_past_blk = past // SLC_BLOCK
    per_page = PAGE_SIZE // SLC_BLOCK

    bps = NSA_BLOCKS_PER_STEP

    def blk_spec(g, u, t):
        def imap(b, i, pt, ix):
            blk = jnp.minimum(ix[b, g, i * bps + u], n_past_blk - 1)
            return (layer, pt[b, blk // per_page], t, g, 0, 0)
        return pl.BlockSpec((None, None, None, None, HEAD_DIM, PAGE_SIZE), imap)

    row = lambda w: pl.BlockSpec((None, 1, w), lambda b, i, pt, ix: (b, 0, 0))
    gu = [(g, u) for g in range(NSA_KV_HEADS) for u in range(bps)]
    return pl.pallas_call(
        functools.partial(_nsa_sample_attn_body, n_past_blk=n_past_blk),
        grid_spec=pltpu.PrefetchScalarGridSpec(
            num_scalar_prefetch=2,
            grid=(bs, SLC_TOPK // bps),
            in_specs=[blk_spec(g, u, 2) for g, u in gu] + [blk_spec(g, u, 3) for g, u in gu]
                     + [row(GROUP_WIDTH), row(6 * NSA_KV_WIDTH), row(LANE), row(GROUP_WIDTH)],
            out_specs=row(GROUP_WIDTH),
            scratch_shapes=[pltpu.VMEM((NSA_KV_HEADS, 8, HEAD_DIM), F32)] * 3),
        out_shape=jax.ShapeDtypeStruct((bs, 1, GROUP_WIDTH), F32),
        compiler_params=_cparams(("parallel", "arbitrary")),
        name="nsa_sample_attn",
    )(page_table, sel_idx, *([nsa_cache] * (2 * len(gu))), q, kv_new, gates, part)


def _recur_sample_body(c_ref, shift_ref, mu_ref, w0_ref, w2_ref, a0_ref, a2_ref, g2_ref, kk_ref, ka_ref, rk_ref,
                       lng_ref, lnb_ref, s_rw_ref, ret_ref, cos_ref, sin_ref, gam_ref, gn_g_ref, gn_b_ref, s_rt_ref,
                       o_rw_ref, s_rw_out, o_rt_ref, s_rt_out, o_scr):
    gw = GROUP_WIDTH
    hd = HEAD_DIM
    p = (mu_ref[...], w0_ref[...], w2_ref[...], a0_ref[...], a2_ref[...], g2_ref[...], kk_ref[...], ka_ref[...])
    r, w, k, v, kk, kka, gate = _rwkv_prep(c_ref[...], shift_ref[...], p)

    def put_state(h, j, s):
        s_rw_out[j, h] = s

    ys = _rwkv_steps((r, w, k, v, kk, kka), lambda h, j: s_rw_ref[j, h], put_state, 1)
    o_rw_ref[...] = _rwkv_post(ys, r, k, v, gate, rk_ref[...], lng_ref[...], lnb_ref[...]).astype(o_rw_ref.dtype)

    cos = cos_ref[...]
    sin = sin_ref[...]
    q = _rope(ret_ref[:, 0:gw], cos, sin)
    kr = _rope(ret_ref[:, gw:2 * gw], cos, sin) * ATTN_SCALE
    vr = ret_ref[:, 2 * gw:3 * gw]
    gam = gam_ref[...]
    qk = _dot(q * kr, _head_blockdiag(gw), HI)
    pad = jnp.zeros((LANE - 8, LANE), F32)
    for hp in range(N_HEADS // 2):
        ps = slice(hp * LANE, (hp + 1) * LANE)
        qt = jnp.concatenate([q[:, ps], pad], axis=0).T
        kt = jnp.concatenate([kr[:, ps], pad], axis=0).T
        for h2 in range(2):
            h = 2 * hp + h2
            sl = slice(h * hd, (h + 1) * hd)
            for j in range(8):
                s = s_rt_ref[j, h]
                qcol = qt[h2 * hd:(h2 + 1) * hd, j:j + 1]
                kcol = kt[h2 * hd:(h2 + 1) * hd, j:j + 1]
                g_h = gam[:, sl]
                o_scr[j:j + 1, sl] = (qk[j:j + 1, sl] * vr[j:j + 1, sl]
                                      + g_h * jnp.sum(qcol * s, axis=0, keepdims=True))
                s_rt_out[j, h] = s * g_h + kcol * vr[j:j + 1, sl]
    gt = ret_ref[:, 3 * gw:4 * gw]
    y = _head_norm(o_scr[...], gn_g_ref[...], gn_b_ref[...], GN_EPS)
    o_rt_ref[...] = (gt * _sigmoid(gt) * y).astype(o_rt_ref.dtype)


def _recur_sample(c_rwkv, shift, rwkv_params, s_rwkv, c_ret, rope, gamma, gn_g, gn_b, s_ret):
    bs = c_rwkv.shape[0]
    cos, sin = rope
    st = jax.ShapeDtypeStruct((bs, N_HEADS, HEAD_DIM, HEAD_DIM), F32)
    ob = jax.ShapeDtypeStruct((bs, GROUP_WIDTH), BF16)
    return pl.pallas_call(
        _recur_sample_body,
        out_shape=[ob, st, ob, st],
        scratch_shapes=[pltpu.VMEM((bs, GROUP_WIDTH), F32)],
        compiler_params=pltpu.CompilerParams(vmem_limit_bytes=VMEM_LIMIT),
        name="recur_sample",
    )(c_rwkv, shift, *rwkv_params, s_rwkv, c_ret, cos, sin, gamma, gn_g, gn_b, s_ret)


def _prep_rwkv_params(mu, w0, w2, a0, a2, g2, k_k, k_a, r_k, ln_g, ln_b):
    row = lambda z: z.reshape(1, -1)
    return (row(mu), row(w0), w2, row(a0), a2, g2, row(k_k), row(k_a), row(r_k), row(ln_g), row(ln_b))


def _prep_cmp_weights(w1, b1, w2, pe):
    span = CMP_LEN // CMP_STRIDE
    w1r = w1.reshape(2, span, CMP_STRIDE, HEAD_DIM, CMP_HIDDEN)
    w1r = jnp.transpose(w1r, (0, 2, 3, 1, 4)).reshape(2, CMP_STRIDE, HEAD_DIM, span * CMP_HIDDEN).astype(BF16)
    return (w1r, w1, b1.reshape(2, 1, CMP_HIDDEN), w2, pe.reshape(2, 1, CMP_LEN * HEAD_DIM))


def _pad_w_in(w):
    o_nsa = MOBA_COLS
    o_gate = o_nsa + GROUP_WIDTH + 6 * NSA_KV_WIDTH
    o_rwkv = o_nsa + NSA_COLS
    o_ret = o_rwkv + RWKV_COLS
    zeros = jnp.zeros((w.shape[0], N_PAD - OFF_GATE - 3 * N_HEADS), w.dtype)
    return jnp.concatenate([w[:, o_rwkv:o_ret], w[:, :o_gate], w[:, o_ret:], w[:, o_gate:o_rwkv], zeros], axis=1)


def kernel(x_prompt, x_sample, cache_moba_kv, cache_nsa_kv, state_nsa_win, state_rwkv, state_rwkv_shift, state_ret,
           page_table, norm_g, w_in, w_out, w_up, w_down, nsa_cmp_pe, nsa_cmp_w1, nsa_cmp_b1, nsa_cmp_w2, rwkv_mu,
           rwkv_w0, rwkv_w2, rwkv_a0, rwkv_a2, rwkv_g2, rwkv_k_k, rwkv_k_a, rwkv_r_k, rwkv_ln_g, rwkv_ln_b, ret_gn_g,
           ret_gn_b):
    bp, t, d = x_prompt.shape
    bs = x_sample.shape[0]
    assert x_sample.shape[1] == 1 and d == D_MODEL
    depth = w_in.shape[0]
    past = page_table.shape[1] * PAGE_SIZE
    gw = GROUP_WIDTH
    hd = HEAD_DIM

    log_gamma = jnp.log(1.0 - jnp.exp2(-5.0 - jnp.arange(N_HEADS, dtype=F32)))
    ret_tables = _ret_tables(log_gamma, RET_CHUNK)
    rope_p = _rope_tables(jnp.arange(t, dtype=I32))
    rope_s = _rope_tables(jnp.full((1,), past, I32))
    gamma_row = jnp.repeat(jnp.exp(log_gamma), hd)[None, :]
    moba_t = jnp.transpose(cache_moba_kv, (0, 1, 3, 4, 5, 2))
    nsa_t = jnp.transpose(cache_nsa_kv, (0, 1, 3, 4, 5, 2))
    win_t = jnp.transpose(state_nsa_win, (0, 1, 3, 4, 5, 2))

    xp = x_prompt.reshape(bp * t, d)
    xs = x_sample.reshape(bs, d)
    zero_state = jnp.zeros((bp, N_HEADS, hd, hd), F32)
    zero_shift = jnp.zeros((bp, RWKV_COLS), F32)
    st_p, st_s = [], []
    for l in range(depth):
        g = norm_g[l].reshape(4, 1, d)
        w_in_l = _pad_w_in(w_in[l]).astype(BF16)
        w_out_l = w_out[l].astype(BF16)
        w_up_l = w_up[l].astype(BF16)
        w_down_l = w_down[l].astype(BF16)
        cmp_w = _prep_cmp_weights(nsa_cmp_w1[l], nsa_cmp_b1[l], nsa_cmp_w2[l], nsa_cmp_pe[l])
        rwkv_p = _prep_rwkv_params(rwkv_mu[l], rwkv_w0[l], rwkv_w2[l], rwkv_a0[l], rwkv_a2[l], rwkv_g2[l],
                                   rwkv_k_k[l], rwkv_k_a[l], rwkv_r_k[l], rwkv_ln_g[l], rwkv_ln_b[l])
        gn_g = ret_gn_g[l].reshape(1, gw)
        gn_b = ret_gn_b[l].reshape(1, gw)

        cols = _inproj(xp, g[0], w_in_l, min(1024, bp * t), 1024)
        o_moba = _moba_prompt(cols, bp, t)
        o_nsa = _nsa_prompt(cols, cmp_w, bp, t)
        o_rwkv, rwkv_s = _rwkv_prompt(cols, zero_shift, zero_state, rwkv_p, bp, t)
        o_ret, ret_s = _ret_prompt(cols, zero_state, ret_tables, rope_p, gn_g, gn_b, bp, t)
        xp = _outproj((o_moba, o_nsa, o_rwkv, o_ret), xp, g[1], w_out_l, 256)
        xp = _ffn(xp, g[2], g[3], w_up_l, w_down_l, min(512, bp * t), 1024)
        c3 = cols.reshape(bp, t, N_PAD)
        win_keep = min(WINDOW, t)
        st_p.append((c3[:, :, OFF_MOBA + gw:OFF_MOBA + 3 * gw].reshape(bp, t, 2, N_HEADS, hd),
                     c3[:, :, OFF_NSAKV:OFF_NSAKV + 4 * NSA_KV_WIDTH].reshape(bp, t, 4, NSA_KV_HEADS, hd),
                     c3[:, t - win_keep:, OFF_NSAKV + 4 * NSA_KV_WIDTH:OFF_NSAKV + 6 * NSA_KV_WIDTH
                        ].reshape(bp, win_keep, 2, NSA_KV_HEADS, hd),
                     rwkv_s, c3[:, t - 1, OFF_RWKV:OFF_RWKV + RWKV_COLS], ret_s))

        cs = _inproj(xs, g[0], w_in_l, bs, 512)
        heads = lambda z: z.reshape(bs, N_HEADS, 1, hd)
        q_m = cs[:, OFF_MOBA:OFF_MOBA + gw]
        k_m = cs[:, OFF_MOBA + gw:OFF_MOBA + 2 * gw]
        v_m = cs[:, OFF_MOBA + 2 * gw:OFF_MOBA + 3 * gw]
        top, cmp_tok = _past_stats(page_table, moba_t, nsa_t, q_m.reshape(bs, N_HEADS, hd, 1), cmp_w, l)
        o_moba_s = _moba_sample(page_table, top[:, :MOBA_TOPK, :], moba_t, heads(q_m), heads(k_m), heads(v_m), l)
        q_n = cs[:, OFF_NSAQ:OFF_NSAQ + gw].reshape(bs, 1, gw)
        kv_new = cs[:, OFF_NSAKV:OFF_NSAKV + 6 * NSA_KV_WIDTH].reshape(bs, 1, 6 * NSA_KV_WIDTH)
        gates = cs[:, OFF_GATE:OFF_GATE + LANE].reshape(bs, 1, LANE)
        sel_idx, part = _nsa_sample_sel(q_n, cmp_tok, win_t, kv_new, gates, l, past)
        o_nsa_s = _nsa_sample_attn(page_table, sel_idx[:, :NSA_KV_HEADS, :SLC_TOPK], nsa_t, q_n, kv_new, gates, part,
                                   l, past)
        o_rwkv_s, rwkv_s_s, o_ret_s, ret_s_s = _recur_sample(
            cs[:, OFF_RWKV:OFF_RWKV + RWKV_COLS], state_rwkv_shift[l], rwkv_p, state_rwkv[l],
            cs[:, OFF_RET:OFF_RET + RET_COLS], rope_s, gamma_row, gn_g, gn_b, state_ret[l])
        parts_s = (o_moba_s.reshape(bs, gw).astype(BF16), o_nsa_s.reshape(bs, gw).astype(BF16), o_rwkv_s, o_ret_s)
        xs = _outproj(parts_s, xs, g[1], w_out_l, bs)
        xs = _ffn(xs, g[2], g[3], w_up_l, w_down_l, bs, 512)
        win_new = cs[:, OFF_NSAKV + 4 * NSA_KV_WIDTH:OFF_NSAKV + 6 * NSA_KV_WIDTH].reshape(bs, 1, 2, NSA_KV_HEADS, hd)
        win_all = jnp.concatenate([state_nsa_win[l], win_new], axis=1)
        keep_s = min(WINDOW, win_all.shape[1])
        st_s.append((cs[:, OFF_MOBA + gw:OFF_MOBA + 3 * gw].reshape(bs, 1, 2, N_HEADS, hd),
                     cs[:, OFF_NSAKV:OFF_NSAKV + 4 * NSA_KV_WIDTH].reshape(bs, 1, 4, NSA_KV_HEADS, hd),
                     win_all[:, win_all.shape[1] - keep_s:],
                     rwkv_s_s, cs[:, OFF_RWKV:OFF_RWKV + RWKV_COLS], ret_s_s))

    stk = lambda sts, i: jnp.stack([s[i] for s in sts], axis=0)
    outs = [xp.reshape(bp, t, d), xs.reshape(bs, 1, d)]
    for i in range(6):
        outs += [stk(st_p, i), stk(st_s, i)]
    return tuple(outs)
```

```python
import functools

import jax
import jax.numpy as jnp
import numpy as np
from jax import lax
from jax.experimental import pallas as pl
from jax.experimental.pallas import tpu as pltpu

F32 = jnp.float32
BF16 = jnp.bfloat16
I32 = jnp.int32
HI = lax.Precision.HIGHEST

D_MODEL = 2048
HEAD_DIM = 64
GROUP_WIDTH = D_MODEL // 4
N_HEADS = GROUP_WIDTH // HEAD_DIM
D_FF = 4 * D_MODEL
RMS_EPS = 1e-6
GN_EPS = 1e-5
NEG_INF = -1e30
ATTN_SCALE = HEAD_DIM ** -0.5
PAGE_SIZE = 128
MOBA_BLOCK = 256
MOBA_TOPK = 3
NSA_KV_HEADS = 2
NSA_GROUP = N_HEADS // NSA_KV_HEADS
NSA_KV_WIDTH = NSA_KV_HEADS * HEAD_DIM
CMP_LEN = 32
CMP_STRIDE = 16
CMP_HIDDEN = 2 * HEAD_DIM
SLC_BLOCK = 64
SLC_TOPK = 16
WINDOW = 512
RWKV_DECAY_RANK = 64
RWKV_AAA_RANK = 64
RWKV_GATE_RANK = 128
RWKV_LN_EPS = 64e-5
RET_CHUNK = 128
ROPE_BASE = 10000.0

MOBA_COLS = 3 * GROUP_WIDTH
NSA_COLS = GROUP_WIDTH + 6 * NSA_KV_WIDTH + 3 * N_HEADS
RWKV_COLS = 3 * GROUP_WIDTH + RWKV_DECAY_RANK + RWKV_AAA_RANK + RWKV_GATE_RANK
RET_COLS = 4 * GROUP_WIDTH
IN_COLS = MOBA_COLS + NSA_COLS + RWKV_COLS + RET_COLS

OFF_RWKV = 0
OFF_MOBA = OFF_RWKV + RWKV_COLS
OFF_NSAQ = OFF_MOBA + MOBA_COLS
OFF_NSAKV = OFF_NSAQ + GROUP_WIDTH
OFF_RET = OFF_NSAKV + 6 * NSA_KV_WIDTH
OFF_GATE = OFF_RET + RET_COLS
N_PAD = OFF_GATE + 512

LANE = 128
VMEM_LIMIT = 56 * 1024 * 1024


def _cparams(sem):
    return pltpu.CompilerParams(dimension_semantics=sem, vmem_limit_bytes=VMEM_LIMIT)


def _dot(a, b, precision=None):
    return jnp.dot(a, b, preferred_element_type=F32, precision=precision)


def _dot_nt(a, b, precision=None):
    return lax.dot_general(a, b, (((1,), (1,)), ((), ())), preferred_element_type=F32, precision=precision)


def _dot_tn(a, b, precision=None):
    return lax.dot_general(a, b, (((0,), (0,)), ((), ())), preferred_element_type=F32, precision=precision)


def _head_blockdiag(n, scale=1.0):
    r = lax.broadcasted_iota(I32, (n, n), 0) // HEAD_DIM
    c = lax.broadcasted_iota(I32, (n, n), 1) // HEAD_DIM
    return jnp.where(r == c, scale, 0.0).astype(F32)


def _dot_split(x, m):
    hi = x.astype(BF16)
    lo = (x - hi.astype(F32)).astype(BF16)
    mb = m.astype(BF16)
    return _dot(hi, mb) + _dot(lo, mb)


def _sigmoid(x):
    return 1.0 / (1.0 + jnp.exp(-x))


def _inproj_body(x_ref, g_ref, w_ref, o_ref, h_scr):
    @pl.when(pl.program_id(1) == 0)
    def _():
        x = x_ref[...]
        ms = jnp.mean(x * x, axis=-1, keepdims=True)
        h_scr[...] = (x * lax.rsqrt(ms + RMS_EPS) * g_ref[...]).astype(BF16)

    o_ref[...] = _dot(h_scr[...], w_ref[...])


def _inproj(x, g, w, tm, tn):
    m, d = x.shape
    n = w.shape[1]
    return pl.pallas_call(
        _inproj_body,
        grid=(m // tm, n // tn),
        in_specs=[pl.BlockSpec((tm, d), lambda i, j: (i, 0)),
                  pl.BlockSpec((1, d), lambda i, j: (0, 0)),
                  pl.BlockSpec((d, tn), lambda i, j: (0, j))],
        out_specs=pl.BlockSpec((tm, tn), lambda i, j: (i, j)),
        out_shape=jax.ShapeDtypeStruct((m, n), F32),
        scratch_shapes=[pltpu.VMEM((tm, d), BF16)],
        compiler_params=_cparams(("parallel", "arbitrary")),
        name="inproj",
    )(x, g, w)


def _outproj_body(a_ref, b_ref, c_ref, d_ref, x_ref, g_ref, w_ref, o_ref):
    gw = GROUP_WIDTH
    y = _dot(a_ref[...], w_ref[0:gw, :])
    y += _dot(b_ref[...], w_ref[gw:2 * gw, :])
    y += _dot(c_ref[...], w_ref[2 * gw:3 * gw, :])
    y += _dot(d_ref[...], w_ref[3 * gw:4 * gw, :])
    ms = jnp.mean(y * y, axis=-1, keepdims=True)
    o_ref[...] = x_ref[...] + y * lax.rsqrt(ms + RMS_EPS) * g_ref[...]


def _outproj(parts, x, g, w, tm):
    m, d = x.shape
    gw = GROUP_WIDTH
    part_spec = pl.BlockSpec((tm, gw), lambda i: (i, 0))
    return pl.pallas_call(
        _outproj_body,
        grid=(m // tm,),
        in_specs=[part_spec, part_spec, part_spec, part_spec,
                  pl.BlockSpec((tm, d), lambda i: (i, 0)),
                  pl.BlockSpec((1, d), lambda i: (0, 0)),
                  pl.BlockSpec((4 * gw, d), lambda i: (0, 0))],
        out_specs=pl.BlockSpec((tm, d), lambda i: (i, 0)),
        out_shape=jax.ShapeDtypeStruct((m, d), F32),
        compiler_params=_cparams(("parallel",)),
        name="outproj",
    )(*parts, x, g, w)


def _ffn_body(x_ref, g2_ref, g3_ref, wu_ref, wd_ref, o_ref, h_scr, acc_scr):
    f = pl.program_id(1)

    @pl.when(f == 0)
    def _():
        x = x_ref[...]
        ms = jnp.mean(x * x, axis=-1, keepdims=True)
        h_scr[...] = (x * lax.rsqrt(ms + RMS_EPS) * g2_ref[...]).astype(BF16)
        acc_scr[...] = jnp.zeros_like(acc_scr)

    u = jnp.maximum(_dot(h_scr[...], wu_ref[...]), 0.0)
    acc_scr[...] += _dot((u * u).astype(BF16), wd_ref[...])

    @pl.when(f == pl.num_programs(1) - 1)
    def _():
        y = acc_scr[...]
        ms = jnp.mean(y * y, axis=-1, keepdims=True)
        o_ref[...] = x_ref[...] + y * lax.rsqrt(ms + RMS_EPS) * g3_ref[...]


def _ffn(x, g2, g3, wu, wd, tm, tf):
    m, d = x.shape
    f = wu.shape[1]
    return pl.pallas_call(
        _ffn_body,
        grid=(m // tm, f // tf),
        in_specs=[pl.BlockSpec((tm, d), lambda i, j: (i, 0)),
                  pl.BlockSpec((1, d), lambda i, j: (0, 0)),
                  pl.BlockSpec((1, d), lambda i, j: (0, 0)),
                  pl.BlockSpec((d, tf), lambda i, j: (0, j)),
                  pl.BlockSpec((tf, d), lambda i, j: (j, 0))],
        out_specs=pl.BlockSpec((tm, d), lambda i, j: (i, 0)),
        out_shape=jax.ShapeDtypeStruct((m, d), F32),
        scratch_shapes=[pltpu.VMEM((tm, d), BF16), pltpu.VMEM((tm, d), F32)],
        compiler_params=_cparams(("parallel", "arbitrary")),
        name="ffn",
    )(x, g2, g3, wu, wd)


MOBA_HEAD_GROUP = 4


def _moba_prompt_body(q_ref, k_ref, v_ref, o_ref, kmean_scr, kext, vext, *, nb):
    blk = MOBA_BLOCK
    hg = MOBA_HEAD_GROUP
    qt = pl.program_id(2)
    pair_of = lambda j: slice(LANE * (j // 2), LANE * (j // 2 + 1))
    dst_of = lambda j: slice(LANE * j, LANE * (j + 1))

    @pl.when(qt == 0)
    def _():
        kmean_scr[...] = jnp.zeros_like(kmean_scr)
        mine_t = lax.broadcasted_iota(I32, (k_ref.shape[0], LANE), 1) // HEAD_DIM
        for j in range(hg):
            kp = jnp.where(mine_t == j % 2, k_ref[:, pair_of(j)], 0.0)
            kext[:, dst_of(j)] = kp.astype(BF16)
            vext[:, dst_of(j)] = jnp.where(mine_t == j % 2, v_ref[:, pair_of(j)], 1.0).astype(BF16)
            for n in range(nb):
                kmean_scr[n:n + 1, dst_of(j)] = jnp.sum(kp[n * blk:(n + 1) * blk], axis=0, keepdims=True) * (1.0 / blk)

    row = lax.broadcasted_iota(I32, (blk, blk), 0)
    col = lax.broadcasted_iota(I32, (blk, blk), 1)
    bias_own = jnp.where(col <= row, 0.0, NEG_INF)
    nb_r = kmean_scr.shape[0]
    blk_id = lax.broadcasted_iota(I32, (nb_r, blk), 0)
    eligible = blk_id < qt
    not_sel, qb = [], []
    for j in range(hg):
        q = q_ref[:, pair_of(j)]
        gate = jnp.where(eligible, _dot_nt(kmean_scr[:, dst_of(j)], q, HI), -jnp.inf)
        rank = jnp.zeros((nb_r, blk), I32)
        for m in range(nb):
            gm = gate[m:m + 1, :]
            rank += ((gm > gate) | ((gm == gate) & (m < blk_id))).astype(I32)
        ns_t = jnp.where(eligible & (rank < MOBA_TOPK), 0.0, 1.0)
        not_sel.append(jnp.concatenate([ns_t, jnp.ones((LANE - nb_r, blk), F32)], axis=0).T.astype(BF16))
        qb.append((q * ATTN_SCALE).astype(BF16))

    def attend(carry, start, biases):
        out = []
        for j in range(hg):
            m_i, acc = carry[j]
            s = _dot_nt(qb[j], kext[pl.ds(start, blk), dst_of(j)]) + biases[j]
            m_new = jnp.maximum(m_i, jnp.max(s, axis=-1, keepdims=True))
            p = jnp.exp(s - m_new)
            out.append((m_new, jnp.exp(m_i - m_new) * acc + _dot(p.astype(BF16), vext[pl.ds(start, blk), dst_of(j)])))
        return tuple(out)

    def body(n, carry):
        pick = (lax.broadcasted_iota(I32, (LANE, blk), 0) == n).astype(BF16)
        return attend(carry, pl.multiple_of(n * blk, blk), [_dot(ns, pick) * NEG_INF for ns in not_sel])

    init = tuple((jnp.full((blk, 1), NEG_INF, F32), jnp.zeros((blk, LANE), F32)) for _ in range(hg))
    carry = lax.fori_loop(0, qt, body, init)
    fin = attend(carry, pl.multiple_of(qt * blk, blk), [bias_own] * hg)
    half = lax.broadcasted_iota(I32, (blk, LANE), 1) // HEAD_DIM
    for jp in range(hg // 2):
        outs = [fin[2 * jp + e][1] / pltpu.roll(fin[2 * jp + e][1], HEAD_DIM, 1) for e in range(2)]
        o_ref[:, dst_of(jp)] = jnp.where(half == 0, outs[0], outs[1]).astype(o_ref.dtype)


def _moba_prompt(cols, b, t):
    blk = MOBA_BLOCK
    nb = t // blk
    w = MOBA_HEAD_GROUP * HEAD_DIM
    qoff = OFF_MOBA // w
    koff = (OFF_MOBA + GROUP_WIDTH) // w
    voff = (OFF_MOBA + 2 * GROUP_WIDTH) // w
    return pl.pallas_call(
        functools.partial(_moba_prompt_body, nb=nb),
        grid=(b, GROUP_WIDTH // w, nb),
        in_specs=[pl.BlockSpec((blk, w), lambda i, h, q: (i * nb + q, qoff + h)),
                  pl.BlockSpec((t, w), lambda i, h, q: (i, koff + h)),
                  pl.BlockSpec((t, w), lambda i, h, q: (i, voff + h))],
        out_specs=pl.BlockSpec((blk, w), lambda i, h, q: (i * nb + q, h)),
        out_shape=jax.ShapeDtypeStruct((b * t, GROUP_WIDTH), BF16),
        scratch_shapes=[pltpu.VMEM((-(-nb // 8) * 8, MOBA_HEAD_GROUP * LANE), F32),
                        pltpu.VMEM((t, MOBA_HEAD_GROUP * LANE), BF16), pltpu.VMEM((t, MOBA_HEAD_GROUP * LANE), BF16)],
        compiler_params=_cparams(("parallel", "parallel", "arbitrary")),
        name="moba_prompt",
    )(cols, cols, cols)


def _gelu_tanh(x):
    return x * (0.5 * (1.0 + jnp.tanh(np.sqrt(2.0 / np.pi).astype(np.float32) * (x + 0.044715 * (x * x * x)))))


def _cmp_bias(pe_ref, w1_ref, b1_ref, kv):
    pe8 = jnp.broadcast_to(pe_ref[kv], (8, CMP_LEN * HEAD_DIM))
    return _dot(pe8, w1_ref[kv], HI)[0:1, :] + b1_ref[kv]


def _compress(load_rows, n, w1r_ref, kv, bias, w2):
    acc = [jnp.zeros((n, 2 * CMP_HIDDEN), F32) for _ in range(NSA_KV_HEADS)]
    for r in range(CMP_STRIDE):
        x = load_rows(r).astype(BF16)
        for g in range(NSA_KV_HEADS):
            acc[g] += _dot(x[:, g * HEAD_DIM:(g + 1) * HEAD_DIM], w1r_ref[kv, r])
    out = []
    for g in range(NSA_KV_HEADS):
        hid = acc[g][:, :CMP_HIDDEN] + pltpu.roll(acc[g][:, CMP_HIDDEN:], n - 1, 0) + bias
        out.append(_dot(_gelu_tanh(hid).astype(BF16), w2.astype(BF16)))
    return out


def _softmax_rows(l, mask):
    m = jnp.max(jnp.where(mask, l, NEG_INF), axis=-1, keepdims=True)
    e = jnp.where(mask, jnp.exp(l - m), 0.0)
    s = jnp.sum(e, axis=-1, keepdims=True)
    return jnp.where(s > 0.0, e / jnp.where(s > 0.0, s, 1.0), 0.0)


def _slc_matrix(n_cmp_pad, n_cmp, n_slc_pad):
    n = lax.broadcasted_iota(I32, (n_cmp_pad, n_slc_pad), 0)
    j = lax.broadcasted_iota(I32, (n_cmp_pad, n_slc_pad), 1)
    ratio = SLC_BLOCK // CMP_STRIDE
    return ((n >= ratio * j - 1) & (n <= ratio * j + ratio - 1) & (n < n_cmp)).astype(F32)


def _topk_rows(score, ids, n_cand, k):
    rank = jnp.zeros(score.shape, I32)
    for m in range(n_cand):
        sm = score[:, m:m + 1]
        rank += ((sm > score) | ((sm == score) & (m < ids))).astype(I32)
    return rank < k


def _nsa_prompt_body(q0_ref, q1_ref, kc_ref, vc_ref, ksvs_ref, kwvw_ref, gate_ref, w1r_ref, w1_ref, b1_ref, w2_ref,
                     pe_ref, o_ref, ck_scr, cv_scr, ks_ext, vs_ext, kw_ext, vw_ext, *, t):
    tq = 256
    n_chunk = t // CMP_STRIDE
    n_cmp = n_chunk - CMP_LEN // CMP_STRIDE + 1
    qt = pl.program_id(1)
    hd = HEAD_DIM

    @pl.when(qt == 0)
    def _():
        for kv, (src, dst) in enumerate(((kc_ref, ck_scr), (vc_ref, cv_scr))):
            bias = _cmp_bias(pe_ref, w1_ref, b1_ref, kv)
            out = _compress(lambda r: src[pl.ds(r, n_chunk, stride=CMP_STRIDE), :],
                            n_chunk, w1r_ref, kv, bias, w2_ref[kv])
            for g in range(NSA_KV_HEADS):
                dst[g] = out[g]
        half_t = lax.broadcasted_iota(I32, (t, LANE), 1) // hd
        for src, k_ext, v_ext in ((ksvs_ref, ks_ext, vs_ext), (kwvw_ref, kw_ext, vw_ext)):
            for g in range(NSA_KV_HEADS):
                k_ext[g] = jnp.where(half_t == g, src[:, 0:LANE], 0.0).astype(BF16)
                v_ext[g] = jnp.where(half_t == g, src[:, LANE:2 * LANE], 1.0).astype(BF16)

    tpos = qt * tq + lax.broadcasted_iota(I32, (tq, 1), 0)
    tpos4 = jnp.concatenate([tpos] * NSA_GROUP, axis=0)
    gates = _sigmoid(gate_ref[...])
    slc_ids = lax.broadcasted_iota(I32, (tq, LANE), 1)
    n_slc = t // SLC_BLOCK
    kcol = lax.broadcasted_iota(I32, (tq, tq), 1)
    jrow = lax.broadcasted_iota(I32, (LANE, tq), 0)
    kcol_e = lax.broadcasted_iota(I32, (LANE, tq), 1)
    for g in range(NSA_KV_HEADS):
        q_ref = q0_ref if g == 0 else q1_ref
        q4f = jnp.concatenate([q_ref[:, j * hd:(j + 1) * hd] for j in range(NSA_GROUP)], axis=0)
        q4 = q4f.astype(BF16)
        cmp_end = lax.broadcasted_iota(I32, (NSA_GROUP * tq, n_chunk), 1) * CMP_STRIDE + (CMP_LEN - 1)
        l_cmp = _dot_nt(q4, ck_scr[g].astype(BF16)) * ATTN_SCALE
        p_cmp = _softmax_rows(l_cmp, cmp_end <= tpos4)
        o_cmp = _dot(p_cmp.astype(BF16), cv_scr[g].astype(BF16))
        imp = p_cmp[0:tq]
        for j in range(1, NSA_GROUP):
            imp = imp + p_cmp[j * tq:(j + 1) * tq]
        n_slc_r = -(-n_slc // 8) * 8
        jn = lax.broadcasted_iota(I32, (n_slc_r, n_chunk), 0)
        nn = lax.broadcasted_iota(I32, (n_slc_r, n_chunk), 1)
        ratio = SLC_BLOCK // CMP_STRIDE
        slc_t = ((nn >= ratio * jn - 1) & (nn <= ratio * jn + ratio - 1) & (nn < n_cmp)).astype(F32)
        imp_t = _dot_nt(slc_t, imp, HI)
        jt = lax.broadcasted_iota(I32, (n_slc_r, tq), 0)
        cur_t = (qt * tq + lax.broadcasted_iota(I32, (n_slc_r, tq), 1)) // SLC_BLOCK
        eligible = (jt <= cur_t) & (jt < n_slc)
        forced = (jt == 0) | (jt == cur_t) | (jt == cur_t - 1)
        score = jnp.where(eligible, jnp.where(forced, jnp.inf, imp_t), -jnp.inf)
        rank = jnp.zeros((n_slc_r, tq), I32)
        for m in range(n_slc):
            sm = score[m:m + 1, :]
            rank += ((sm > score) | ((sm == score) & (m < jt))).astype(I32)
        sel_t = jnp.where(eligible & (rank < SLC_TOPK), 1.0, 0.0)
        sel = jnp.concatenate([sel_t, jnp.zeros((LANE - n_slc_r, tq), F32)], axis=0).T.astype(BF16)

        half_q = lax.broadcasted_iota(I32, (tq, LANE), 1) // hd
        q4e = []
        for j in range(NSA_GROUP):
            qp = q_ref[:, (j // 2) * LANE:(j // 2 + 1) * LANE]
            qp = qp if j % 2 == g else pltpu.roll(qp, hd, 1)
            q4e.append(jnp.where(half_q == g, qp * ATTN_SCALE, 0.0).astype(BF16))
        q4e = jnp.concatenate(q4e, axis=0)

        def attend(carry, start, bias, k_ext, v_ext):
            m_i, acc = carry
            s = (_dot_nt(q4e, k_ext[g, pl.ds(start, tq), :]).reshape(NSA_GROUP, tq, tq) + bias[None]
                 ).reshape(NSA_GROUP * tq, tq)
            m_new = jnp.maximum(m_i, jnp.max(s, axis=-1, keepdims=True))
            p = jnp.exp(s - m_new)
            return m_new, jnp.exp(m_i - m_new) * acc + _dot(p.astype(BF16), v_ext[g, pl.ds(start, tq), :])

        def sel_body(n, carry):
            expand = (jrow == (tq // SLC_BLOCK) * n + kcol_e // SLC_BLOCK).astype(BF16)
            mask = (_dot(sel, expand) > 0.5) & (n * tq + kcol <= tpos)
            return attend(carry, pl.multiple_of(n * tq, tq), jnp.where(mask, 0.0, NEG_INF), ks_ext, vs_ext)

        def win_body(n, carry):
            dist = tpos - (n * tq + kcol)
            bias = jnp.where((dist >= 0) & (dist < WINDOW), 0.0, NEG_INF)
            return attend(carry, pl.multiple_of(n * tq, tq), bias, kw_ext, vw_ext)

        rows = NSA_GROUP * tq
        init = (jnp.full((rows, 1), NEG_INF, F32), jnp.zeros((rows, LANE), F32))
        _, acc_s = lax.fori_loop(0, qt + 1, sel_body, init)
        _, acc_w = lax.fori_loop(jnp.maximum(qt - (WINDOW // tq), 0), qt + 1, win_body, init)
        o_sel = (acc_s / pltpu.roll(acc_s, hd, 1))[:, g * hd:(g + 1) * hd]
        o_win = (acc_w / pltpu.roll(acc_w, hd, 1))[:, g * hd:(g + 1) * hd]
        for j in range(NSA_GROUP):
            h = g * NSA_GROUP + j
            rs = slice(j * tq, (j + 1) * tq)
            o = (gates[:, h:h + 1] * o_cmp[rs] + gates[:, N_HEADS + h:N_HEADS + h + 1] * o_sel[rs]
                 + gates[:, 2 * N_HEADS + h:2 * N_HEADS + h + 1] * o_win[rs])
            o_ref[:, h * hd:(h + 1) * hd] = o.astype(o_ref.dtype)


def _nsa_prompt(cols, cmp_w, b, t):
    w1r, w1, b1, w2, pe = cmp_w
    tq = 256
    nq = t // tq
    n_chunk = t // CMP_STRIDE
    qoff = OFF_NSAQ // 256
    kvoff = OFF_NSAKV // 256
    full = lambda a: pl.BlockSpec(a.shape, lambda i, q: (0,) * a.ndim)
    return pl.pallas_call(
        functools.partial(_nsa_prompt_body, t=t),
        grid=(b, nq),
        in_specs=[pl.BlockSpec((tq, 256), lambda i, q: (i * nq + q, qoff)),
                  pl.BlockSpec((tq, 256), lambda i, q: (i * nq + q, qoff + 1)),
                  pl.BlockSpec((t, LANE), lambda i, q: (i, 2 * kvoff)),
                  pl.BlockSpec((t, LANE), lambda i, q: (i, 2 * kvoff + 1)),
                  pl.BlockSpec((t, 256), lambda i, q: (i, kvoff + 1)),
                  pl.BlockSpec((t, 256), lambda i, q: (i, kvoff + 2)),
                  pl.BlockSpec((tq, LANE), lambda i, q: (i * nq + q, OFF_GATE // LANE)),
                  full(w1r), full(w1), full(b1), full(w2), full(pe)],
        out_specs=pl.BlockSpec((tq, GROUP_WIDTH), lambda i, q: (i * nq + q, 0)),
        out_shape=jax.ShapeDtypeStruct((b * t, GROUP_WIDTH), BF16),
        scratch_shapes=[pltpu.VMEM((NSA_KV_HEADS, n_chunk, HEAD_DIM), F32),
                        pltpu.VMEM((NSA_KV_HEADS, n_chunk, HEAD_DIM), F32)]
                       + [pltpu.VMEM((NSA_KV_HEADS, t, LANE), BF16)] * 4,
        compiler_params=_cparams(("parallel", "arbitrary")),
        name="nsa_prompt",
    )(cols, cols, cols, cols, cols, cols, cols, w1r, w1, b1, w2, pe)


def _rope(x, cos, sin):
    half = HEAD_DIM // 2
    lane = lax.broadcasted_iota(I32, x.shape, 1) % HEAD_DIM
    nxt = pltpu.roll(x, x.shape[1] - half, 1)
    prv = pltpu.roll(x, half, 1)
    return x * cos + jnp.where(lane < half, -nxt, prv) * sin


def _head_norm(y, g, b, eps):
    avg = _head_blockdiag(y.shape[1], 1.0 / HEAD_DIM)
    mu = _dot_split(y, avg)
    d = y - mu
    var = _dot_split(d * d, avg)
    return d * lax.rsqrt(var + eps) * g + b


def _ret_prompt_body(q_ref, k_ref, v_ref, gate_ref, cos_ref, sin_ref, dmask_ref, xi_ref, zeta_ref, cd_ref,
                     gn_g_ref, gn_b_ref, s0_ref, o_ref, s_ref, o_scr):
    @pl.when(pl.program_id(1) == 0)
    def _():
        s_ref[...] = s0_ref[...]

    cos = cos_ref[...]
    sin = sin_ref[...]
    q = _rope(q_ref[...], cos, sin)
    k = _rope(k_ref[...], cos, sin) * ATTN_SCALE
    kz = (k * zeta_ref[...]).astype(BF16)
    qb = q.astype(BF16)
    kb = k.astype(BF16)
    for h in range(N_HEADS):
        sl = slice(h * HEAD_DIM, (h + 1) * HEAD_DIM)
        vb = v_ref[:, sl].astype(BF16)
        s = s_ref[0, h]
        att = _dot_nt(qb[:, sl], kb[:, sl]) * dmask_ref[h]
        o_scr[:, sl] = _dot(att.astype(BF16), vb) + _dot(qb[:, sl], s.astype(BF16)) * xi_ref[:, sl]
        s_ref[0, h] = s * cd_ref[:, sl] + _dot_tn(kz[:, sl], vb)
    gate = gate_ref[...]
    y = _head_norm(o_scr[...], gn_g_ref[...], gn_b_ref[...], GN_EPS)
    o_ref[...] = (gate * _sigmoid(gate) * y).astype(o_ref.dtype)


def _ret_tables(log_gamma, c):
    idx = jnp.arange(c, dtype=F32)
    diff = idx[:, None] - idx[None, :]
    dmask = jnp.where(diff >= 0, jnp.exp(jnp.maximum(diff, 0.0)[None] * log_gamma[:, None, None]), 0.0)
    rep = lambda z: jnp.repeat(z, HEAD_DIM, axis=-1)
    xi = rep(jnp.exp((idx + 1.0)[:, None] * log_gamma[None, :]))
    zeta = rep(jnp.exp((c - 1.0 - idx)[:, None] * log_gamma[None, :]))
    cd = rep(jnp.exp(c * log_gamma)[None, :])
    return dmask, xi, zeta, cd


def _rope_tables(pos):
    half = HEAD_DIM // 2
    inv = ROPE_BASE ** (-jnp.arange(half, dtype=F32) / half)
    ang = pos.astype(F32)[:, None] * inv[None, :]
    tile = lambda z: jnp.tile(z, (1, 2 * N_HEADS))
    return tile(jnp.cos(ang)), tile(jnp.sin(ang))


def _ret_prompt(cols, s0, tables, rope, gn_g, gn_b, b, t):
    c = RET_CHUNK
    nc = t // c
    gw = GROUP_WIDTH
    off = OFF_RET // gw
    dmask, xi, zeta, cd = tables
    cos, sin = rope
    col_spec = lambda j: pl.BlockSpec((c, gw), lambda i, n: (i * nc + n, off + j))
    full = lambda a: pl.BlockSpec(a.shape, lambda i, n: (0,) * a.ndim)
    st_spec = pl.BlockSpec((1, N_HEADS, HEAD_DIM, HEAD_DIM), lambda i, n: (i, 0, 0, 0))
    return pl.pallas_call(
        _ret_prompt_body,
        grid=(b, nc),
        in_specs=[col_spec(0), col_spec(1), col_spec(2), col_spec(3),
                  pl.BlockSpec((c, gw), lambda i, n: (n, 0)), pl.BlockSpec((c, gw), lambda i, n: (n, 0)),
                  full(dmask), full(xi), full(zeta), full(cd), full(gn_g), full(gn_b), st_spec],
        out_specs=[pl.BlockSpec((c, gw), lambda i, n: (i * nc + n, 0)), st_spec],
        out_shape=[jax.ShapeDtypeStruct((b * t, gw), BF16),
                   jax.ShapeDtypeStruct((b, N_HEADS, HEAD_DIM, HEAD_DIM), F32)],
        scratch_shapes=[pltpu.VMEM((c, gw), F32)],
        compiler_params=_cparams(("parallel", "arbitrary")),
        name="ret_prompt",
    )(cols, cols, cols, cols, cos, sin, dmask, xi, zeta, cd, gn_g, gn_b, s0)


def _softplus(x):
    return jnp.maximum(x, 0.0) + jnp.log(1.0 + jnp.exp(-jnp.abs(x)))


def _rwkv_prep(c, prev, p):
    mu, w0, w2, a0, a2, g2, k_k, k_a = p
    gw = GROUP_WIDTH
    mixed = c + (prev - c) * mu
    r, k, v = mixed[:, :gw], mixed[:, gw:2 * gw], mixed[:, 2 * gw:3 * gw]
    o1 = 3 * gw
    o2 = o1 + RWKV_DECAY_RANK
    o3 = o2 + RWKV_AAA_RANK
    xw, xa, xg = mixed[:, o1:o2], mixed[:, o2:o3], mixed[:, o3:]
    w_log = -_softplus(-(w0 + _dot(jnp.tanh(xw), w2, HI))) - 0.5
    decay = jnp.exp(-jnp.exp(w_log))
    a = _sigmoid(a0 + _dot(xa, a2, HI))
    gate = _dot(_sigmoid(xg).astype(BF16), g2.astype(BF16))
    kk = k * k_k
    norm = jnp.sqrt(_dot_split(kk * kk, _head_blockdiag(gw)))
    kk = kk / jnp.maximum(norm, 1e-12)
    k = k * (1.0 + (a - 1.0) * k_a)
    return r, decay, k, v, kk, kk * a, gate


def _rwkv_steps(vecs, get_state, put_state, n_steps):
    r8, w8, k8, v8, kk8, ka8 = vecs
    lane = lax.broadcasted_iota(I32, (HEAD_DIM, LANE), 1)
    pad = jnp.zeros((LANE - 8, LANE), F32)
    ys = []
    for hp in range(N_HEADS // 2):
        vt = jnp.concatenate([v8[:, hp * LANE:(hp + 1) * LANE], pad], axis=0).T
        yts = []
        for h2 in range(2):
            h = 2 * hp + h2
            sl = slice(h * HEAD_DIM, (h + 1) * HEAD_DIM)
            yt = jnp.zeros((HEAD_DIM, LANE), F32)
            s = get_state(h, 0)
            for j in range(8):
                if n_steps == 1:
                    s = get_state(h, j)
                vcol = vt[h2 * HEAD_DIM:(h2 + 1) * HEAD_DIM, j:j + 1]
                sa = jnp.sum(s * kk8[j:j + 1, sl], axis=-1, keepdims=True)
                s = s * w8[j:j + 1, sl] - sa * ka8[j:j + 1, sl] + vcol * k8[j:j + 1, sl]
                ycol = jnp.sum(s * r8[j:j + 1, sl], axis=-1, keepdims=True)
                yt = jnp.where(lane == j, ycol, yt)
                if n_steps == 1:
                    put_state(h, j, s)
            if n_steps != 1:
                put_state(h, 0, s)
            yts.append(yt)
        ys.append(jnp.concatenate(yts, axis=0).T[0:8, :])
    return jnp.concatenate(ys, axis=1)


def _rwkv_post(ys, r, k, v, gate, r_k, ln_g, ln_b):
    y = _head_norm(ys, ln_g, ln_b, RWKV_LN_EPS)
    y = y + _dot_split(r * k * r_k, _head_blockdiag(GROUP_WIDTH)) * v
    return y * gate


def _rwkv_prep_body(c_ref, shift_ref, mu_ref, w0_ref, w2_ref, a0_ref, a2_ref, g2_ref, kk_ref, ka_ref, rk_ref,
                    keys_o, v_o, bonus_o, gate_o, carry):
    tc = c_ref.shape[0]

    @pl.when(pl.program_id(1) == 0)
    def _():
        carry[...] = shift_ref[...]

    c = c_ref[...]
    row = lax.broadcasted_iota(I32, c.shape, 0)
    prev = jnp.where(row == 0, carry[...], pltpu.roll(c, 1, 0))
    carry[...] = c[tc - 1:tc, :]
    p = (mu_ref[...], w0_ref[...], w2_ref[...], a0_ref[...], a2_ref[...], g2_ref[...], kk_ref[...], ka_ref[...])
    r, w, k, v, kk, kka, gate = _rwkv_prep(c, prev, p)
    for i, z in enumerate((w, kk, kka, k, r)):
        keys_o[i] = z.T
    v_o[...] = v.T
    bonus_o[...] = _dot_split(r * k * rk_ref[...], _head_blockdiag(GROUP_WIDTH)) * v
    gate_o[...] = gate


def _rwkv_relayout_body(*refs):
    x_refs, o_ref, scr = refs[:-2], refs[-2], refs[-1]
    tt = x_refs[0].shape[-1]
    n_rep = LANE // (len(x_refs) * N_HEADS)
    for k in range(HEAD_DIM):
        rows = [x[pl.ds(k, N_HEADS, stride=HEAD_DIM), :] for x in x_refs]
        scr[k * tt:(k + 1) * tt, :] = jnp.concatenate(rows * n_rep, axis=0).T

    def regroup(g, _):
        for u in range(8):
            tok = g * 8 + u
            for kb in range(HEAD_DIM // 8):
                o_ref[pl.ds(pl.multiple_of(tok * HEAD_DIM + kb * 8, 8), 8), :] = (
                    scr[pl.ds(kb * 8 * tt + tok, 8, stride=tt), :])
        return 0

    lax.fori_loop(0, tt // 8, regroup, 0)


def _rwkv_scan_body(w_ref, kk_ref, kka_ref, k_ref, r_ref, v_ref, s0_ref, y_ref, s_ref):
    @pl.when(pl.program_id(0) == 0)
    def _():
        s_ref[...] = s0_ref[...]

    n_vq = s_ref.shape[0]

    def step(t, _):
        vrows = v_ref[t]
        for vq in range(n_vq):
            s = s_ref[vq]
            sa = jnp.sum(s * kk_ref[t], axis=0, keepdims=True)
            s = s * w_ref[t] - sa * kka_ref[t] + vrows[vq:vq + 1, :] * k_ref[t]
            s_ref[vq] = s
            y_ref[t, vq:vq + 1, :] = jnp.sum(s * r_ref[t], axis=0, keepdims=True)
        return 0

    lax.fori_loop(0, v_ref.shape[0], step, 0)


def _rwkv_post_body(y_ref, bonus_ref, gate_ref, lng_ref, lnb_ref, o_ref):
    y = _head_norm(y_ref[...], lng_ref[...], lnb_ref[...], RWKV_LN_EPS)
    o_ref[...] = ((y + bonus_ref[...]) * gate_ref[...]).astype(o_ref.dtype)


def _rwkv_prompt(cols, shift_prev, s0, params, b, t, tc=256, tscan=64):
    mu, w0, w2, a0, a2, g2, k_k, k_a, r_k, ln_g, ln_b = params
    nt = t // tc
    gw = GROUP_WIDTH
    hd = HEAD_DIM
    chains = b * N_HEADS
    rep = LANE // chains
    assert rep * chains == LANE and hd % rep == 0
    n_vq = hd // rep
    full = lambda a: pl.BlockSpec(a.shape, lambda i, n: (0,) * a.ndim)
    tok = pl.BlockSpec((tc, gw), lambda i, n: (i * nt + n, 0))
    vec = jax.ShapeDtypeStruct((b * t, gw), F32)
    prep_in = (mu, w0, w2, a0, a2, g2, k_k, k_a, r_k)
    keys, v, bonus, gate = pl.pallas_call(
        _rwkv_prep_body,
        grid=(b, nt),
        in_specs=[pl.BlockSpec((tc, RWKV_COLS), lambda i, n: (i * nt + n, OFF_RWKV // RWKV_COLS)),
                  pl.BlockSpec((None, 1, RWKV_COLS), lambda i, n: (i, 0, 0))] + [full(a) for a in prep_in],
        out_specs=[pl.BlockSpec((5, gw, tc), lambda i, n: (0, 0, i * nt + n)),
                   pl.BlockSpec((gw, tc), lambda i, n: (0, i * nt + n)), tok, tok],
        out_shape=[jax.ShapeDtypeStruct((5, gw, b * t), F32), jax.ShapeDtypeStruct((gw, b * t), F32), vec, vec],
        scratch_shapes=[pltpu.VMEM((1, RWKV_COLS), F32)],
        compiler_params=_cparams(("parallel", "arbitrary")),
        name="rwkv_prep",
    )(cols, shift_prev.reshape(b, 1, RWKV_COLS), *prep_in)

    tt = LANE
    ntt = t // tt
    keys_t = pl.pallas_call(
        _rwkv_relayout_body,
        grid=(5, ntt),
        in_specs=[pl.BlockSpec((None, gw, tt), lambda j, n, i=i: (j, 0, i * ntt + n)) for i in range(b)],
        out_specs=pl.BlockSpec((None, tt * hd, LANE), lambda j, n: (j, n, 0)),
        out_shape=jax.ShapeDtypeStruct((5, t * hd, LANE), F32),
        scratch_shapes=[pltpu.VMEM((hd * tt, LANE), F32)],
        compiler_params=_cparams(("parallel", "parallel")),
        name="rwkv_relayout",
    )(*([keys] * b)).reshape(5, t, hd, LANE)
    vt = jnp.transpose(v.reshape(N_HEADS, rep, n_vq, b, t), (4, 2, 1, 3, 0)).reshape(t, n_vq, LANE)
    s0t = jnp.transpose(s0.reshape(b, N_HEADS, rep, n_vq, hd), (3, 4, 2, 0, 1)).reshape(n_vq, hd, LANE)

    ns = t // tscan
    st_spec = pl.BlockSpec((n_vq, hd, LANE), lambda n: (0, 0, 0))
    yt, st = pl.pallas_call(
        _rwkv_scan_body,
        grid=(ns,),
        in_specs=[pl.BlockSpec((None, tscan, hd, LANE), lambda n, j=j: (j, n, 0, 0)) for j in range(5)]
                 + [pl.BlockSpec((tscan, n_vq, LANE), lambda n: (n, 0, 0)), st_spec],
        out_specs=[pl.BlockSpec((tscan, n_vq, LANE), lambda n: (n, 0, 0)), st_spec],
        out_shape=[jax.ShapeDtypeStruct((t, n_vq, LANE), F32), jax.ShapeDtypeStruct((n_vq, hd, LANE), F32)],
        compiler_params=_cparams(("arbitrary",)),
        name="rwkv_scan",
    )(*([keys_t] * 5), vt, s0t)
    ys = jnp.transpose(yt.reshape(t, n_vq, rep, b, N_HEADS), (3, 0, 4, 2, 1)).reshape(b * t, gw)
    s_fin = jnp.transpose(st.reshape(n_vq, hd, rep, b, N_HEADS), (3, 4, 2, 0, 1)).reshape(b, N_HEADS, hd, hd)

    tm = min(512, b * t)
    tokm = pl.BlockSpec((tm, gw), lambda i: (i, 0))
    o = pl.pallas_call(
        _rwkv_post_body,
        grid=(b * t // tm,),
        in_specs=[tokm, tokm, tokm, pl.BlockSpec((1, gw), lambda i: (0, 0)), pl.BlockSpec((1, gw), lambda i: (0, 0))],
        out_specs=tokm,
        out_shape=jax.ShapeDtypeStruct((b * t, gw), BF16),
        compiler_params=_cparams(("parallel",)),
        name="rwkv_post",
    )(ys, bonus, gate, ln_g, ln_b)
    return o, s_fin


PAGES_PER_STEP = 8


def _past_stats_body(pt_ref, *refs, n_pages):
    pps = PAGES_PER_STEP
    kt_refs, kct_refs, vct_refs = refs[:pps], refs[pps:2 * pps], refs[2 * pps:3 * pps]
    (q_ref, w1r_ref, w1_ref, b1_ref, w2_ref, pe_ref, top_ref, cmp_ref, qb, gsum, kc_rows, vc_rows) = refs[3 * pps:]
    step = pl.program_id(1)
    pages_per_blk = MOBA_BLOCK // PAGE_SIZE
    nb = n_pages // pages_per_blk
    n_chunk = n_pages * PAGE_SIZE // CMP_STRIDE

    @pl.when(step == 0)
    def _():
        gsum[...] = jnp.zeros_like(gsum)
        qb[...] = jnp.broadcast_to(q_ref[...], qb.shape)

    for i in range(pps):
        p = step * pps + i
        prod = (kt_refs[i][...] * qb[...]).reshape(N_HEADS, HEAD_DIM // 8, 8, PAGE_SIZE)
        gsum[p // pages_per_blk] += jnp.sum(prod, axis=1)
        row0 = pl.multiple_of(p * PAGE_SIZE, PAGE_SIZE)
        kc_rows[pl.ds(row0, PAGE_SIZE), :] = kct_refs[i][...].reshape(LANE, PAGE_SIZE).T
        vc_rows[pl.ds(row0, PAGE_SIZE), :] = vct_refs[i][...].reshape(LANE, PAGE_SIZE).T

    @pl.when(step == n_pages // pps - 1)
    def _():
        gate = jnp.sum(jnp.sum(gsum[...], axis=2), axis=-1) * (1.0 / MOBA_BLOCK)
        ids = lax.broadcasted_iota(I32, (nb, N_HEADS), 0)
        rows8 = lax.broadcasted_iota(I32, (8, N_HEADS), 0)
        top = jnp.zeros((8, N_HEADS), I32)
        for j in range(MOBA_TOPK):
            best = jnp.max(gate, axis=0, keepdims=True)
            arg = jnp.min(jnp.where(gate == best, ids, nb), axis=0, keepdims=True)
            top = jnp.where(rows8 == j, arg, top)
            gate = jnp.where(ids == arg, -jnp.inf, gate)
        top_ref[...] = top
        for kv, src in enumerate((kc_rows, vc_rows)):
            bias = _cmp_bias(pe_ref, w1_ref, b1_ref, kv)
            out = _compress(lambda r: src[pl.ds(r, n_chunk, stride=CMP_STRIDE), :], n_chunk, w1r_ref, kv, bias,
                            w2_ref[kv])
            for g in range(NSA_KV_HEADS):
                cmp_ref[kv * NSA_KV_HEADS + g] = out[g]


def _past_stats(page_table, moba_t, nsa_t, q_col, cmp_w, layer):
    w1r, w1, b1, w2, pe = cmp_w
    bs, n_pages = page_table.shape
    nb = n_pages * PAGE_SIZE // MOBA_BLOCK
    rows = n_pages * PAGE_SIZE
    n_chunk = rows // CMP_STRIDE
    pps = PAGES_PER_STEP
    assert n_pages % pps == 0
    full = lambda a: pl.BlockSpec(a.shape, lambda i, p, pt: (0,) * a.ndim)
    k_page = lambda j: pl.BlockSpec((None, None, None, N_HEADS, HEAD_DIM, PAGE_SIZE),
                                    lambda i, p, pt: (layer, pt[i, p * pps + j], 0, 0, 0, 0))
    nsa_page = lambda t, j: pl.BlockSpec((None, None, None, NSA_KV_HEADS, HEAD_DIM, PAGE_SIZE),
                                         lambda i, p, pt: (layer, pt[i, p * pps + j], t, 0, 0, 0))
    return pl.pallas_call(
        functools.partial(_past_stats_body, n_pages=n_pages),
        grid_spec=pltpu.PrefetchScalarGridSpec(
            num_scalar_prefetch=1,
            grid=(bs, n_pages // pps),
            in_specs=[k_page(j) for j in range(pps)] + [nsa_page(0, j) for j in range(pps)]
                     + [nsa_page(1, j) for j in range(pps)]
                     + [pl.BlockSpec((None, N_HEADS, HEAD_DIM, 1), lambda i, p, pt: (i, 0, 0, 0)),
                        full(w1r), full(w1), full(b1), full(w2), full(pe)],
            out_specs=[pl.BlockSpec((None, 8, N_HEADS), lambda i, p, pt: (i, 0, 0)),
                       pl.BlockSpec((None, 2 * NSA_KV_HEADS, n_chunk, HEAD_DIM), lambda i, p, pt: (i, 0, 0, 0))],
            scratch_shapes=[pltpu.VMEM((N_HEADS, HEAD_DIM, PAGE_SIZE), F32), pltpu.VMEM((nb, N_HEADS, 8, PAGE_SIZE), F32),
                            pltpu.VMEM((rows, LANE), F32), pltpu.VMEM((rows, LANE), F32)]),
        out_shape=[jax.ShapeDtypeStruct((bs, 8, N_HEADS), I32),
                   jax.ShapeDtypeStruct((bs, 2 * NSA_KV_HEADS, n_chunk, HEAD_DIM), F32)],
        compiler_params=_cparams(("parallel", "arbitrary")),
        name="past_stats",
    )(page_table, *([moba_t] * pps), *([nsa_t] * (2 * pps)), q_col, w1r, w1, b1, w2, pe)


def _moba_sample_body(pt_ref, top_ref, *refs):
    n_pg = (len(refs) - 4) // 2
    kt_refs, vt_refs = refs[:n_pg], refs[n_pg:2 * n_pg]
    q_ref, kn_ref, vn_ref, o_ref = refs[2 * n_pg:]
    q = q_ref[...] * ATTN_SCALE
    q8 = jnp.broadcast_to(q, (8, HEAD_DIM)).astype(BF16)
    s_all = [_dot(q8, kt[...].astype(BF16)) for kt in kt_refs]
    s_self = jnp.sum(q * kn_ref[...], axis=-1, keepdims=True)
    m = s_self
    for s in s_all:
        m = jnp.maximum(m, jnp.max(s, axis=-1, keepdims=True))
    e_self = jnp.exp(s_self - m)
    den = e_self
    acc = e_self * vn_ref[...]
    for s, vt in zip(s_all, vt_refs):
        pr = jnp.exp(s - m)
        den = den + jnp.sum(pr, axis=-1, keepdims=True)
        acc = acc + _dot_nt(pr.astype(BF16), vt[...].astype(BF16))
    o_ref[...] = (acc / den)[0:1, :]


def _moba_sample(page_table, top, moba_t, q, k_new, v_new, layer):
    bs = page_table.shape[0]
    ppb = MOBA_BLOCK // PAGE_SIZE

    def page_spec(kv, j, r):
        return pl.BlockSpec((None, None, None, None, HEAD_DIM, PAGE_SIZE),
                            lambda i, h, pt, tp: (layer, pt[i, tp[i, j, h] * ppb + r], kv, h, 0, 0))

    pages = [(j, r) for j in range(MOBA_TOPK) for r in range(ppb)]
    head = pl.BlockSpec((None, None, 1, HEAD_DIM), lambda i, h, pt, tp: (i, h, 0, 0))
    return pl.pallas_call(
        _moba_sample_body,
        grid_spec=pltpu.PrefetchScalarGridSpec(
            num_scalar_prefetch=2,
            grid=(bs, N_HEADS),
            in_specs=[page_spec(0, j, r) for j, r in pages] + [page_spec(1, j, r) for j, r in pages]
                     + [head, head, head],
            out_specs=head),
        out_shape=jax.ShapeDtypeStruct((bs, N_HEADS, 1, HEAD_DIM), F32),
        compiler_params=_cparams(("parallel", "parallel")),
        name="moba_sample",
    )(page_table, top, *([moba_t] * (2 * len(pages))), q, k_new, v_new)


def _heads_to_rows(row, g):
    parts = [row[:, (g * NSA_GROUP + j) * HEAD_DIM:(g * NSA_GROUP + j + 1) * HEAD_DIM] for j in range(NSA_GROUP)]
    return jnp.concatenate(parts + [jnp.zeros((8 - NSA_GROUP, HEAD_DIM), F32)], axis=0)


def _nsa_sample_sel_body(q_ref, cmp_ref, win_ref, kvn_ref, gate_ref, idx_ref, part_ref, *, past):
    hd = HEAD_DIM
    n_chunk = past // CMP_STRIDE
    n_slc = past // SLC_BLOCK + 1
    n_slc_pad = -(-n_slc // LANE) * LANE
    cur = past // SLC_BLOCK
    qrow = q_ref[...]
    gates = _sigmoid(gate_ref[...])
    kvn = kvn_ref[...]
    rows8 = lax.broadcasted_iota(I32, (8, 1), 0)
    ids = lax.broadcasted_iota(I32, (1, n_slc_pad), 1)
    m_iota = lax.broadcasted_iota(I32, (n_slc_pad, n_slc_pad), 0)
    j_iota = lax.broadcasted_iota(I32, (n_slc_pad, n_slc_pad), 1)
    lane = lax.broadcasted_iota(I32, (1, LANE), 1)
    idx_out = jnp.zeros((8, LANE), I32)
    for g in range(NSA_KV_HEADS):
        q8 = _heads_to_rows(qrow, g)
        q8b = q8.astype(BF16)
        n_ids = lax.broadcasted_iota(I32, (8, n_chunk), 1)
        l_cmp = _dot_nt(q8b, cmp_ref[g].astype(BF16)) * ATTN_SCALE
        p_cmp = _softmax_rows(l_cmp, n_ids * CMP_STRIDE + (CMP_LEN - 1) <= past)
        o_cmp = _dot(p_cmp.astype(BF16), cmp_ref[NSA_KV_HEADS + g].astype(BF16))
        imp = jnp.sum(jnp.where(rows8 < NSA_GROUP, p_cmp, 0.0), axis=0, keepdims=True)
        imp_slc = _dot(jnp.broadcast_to(imp, (8, n_chunk)), _slc_matrix(n_chunk, n_chunk - 1, n_slc_pad), HI)
        eligible = (ids <= cur) & (ids < n_slc)
        forced = (ids == 0) | (ids == cur) | (ids == cur - 1)
        score = jnp.where(eligible, jnp.where(forced, jnp.inf, imp_slc[0:1, :]), -jnp.inf)
        s_col = jnp.broadcast_to(score, (LANE, n_slc_pad)).T[:, 0:1]
        beats = (s_col > score) | ((s_col == score) & (m_iota < j_iota))
        rank = jnp.sum(beats.astype(F32), axis=0, keepdims=True)
        sel = jnp.where(eligible & (rank < SLC_TOPK), 1.0, 0.0)
        sel_col = jnp.broadcast_to(sel, (LANE, n_slc_pad)).T[:, 0:1]
        before = jnp.sum(jnp.where(m_iota < j_iota, sel_col, 0.0), axis=0, keepdims=True)
        idx_row = jnp.zeros((1, LANE), I32)
        for i in range(SLC_TOPK):
            hit = (sel > 0.5) & (before == i)
            idx_i = jnp.sum(jnp.where(hit, ids, 0), axis=-1, keepdims=True)
            idx_row = jnp.where(lane == i, idx_i, idx_row)
        idx_out = jnp.where(lax.broadcasted_iota(I32, (8, LANE), 0) == g, idx_row, idx_out)
        kw = win_ref[0, g].astype(BF16)
        vw = win_ref[1, g].astype(BF16)
        n_buf = win_ref.shape[-1]
        w_ids = lax.broadcasted_iota(I32, (8, n_buf), 1)
        l_win = _dot(q8b, kw) * ATTN_SCALE
        w_mask = w_ids >= n_buf - (WINDOW - 1)
        kw_new = kvn[:, 4 * NSA_KV_WIDTH + g * hd:4 * NSA_KV_WIDTH + (g + 1) * hd]
        vw_new = kvn[:, 5 * NSA_KV_WIDTH + g * hd:5 * NSA_KV_WIDTH + (g + 1) * hd]
        s_self = jnp.sum(q8 * kw_new, axis=-1, keepdims=True) * ATTN_SCALE
        m = jnp.maximum(jnp.max(jnp.where(w_mask, l_win, NEG_INF), axis=-1, keepdims=True), s_self)
        e = jnp.where(w_mask, jnp.exp(l_win - m), 0.0)
        e_self = jnp.exp(s_self - m)
        o_win = (_dot_nt(e.astype(BF16), vw) + e_self * vw_new) / (jnp.sum(e, axis=-1, keepdims=True) + e_self)
        for j in range(NSA_GROUP):
            h = g * NSA_GROUP + j
            part_ref[:, h * hd:(h + 1) * hd] = (gates[:, h:h + 1] * o_cmp[j:j + 1]
                                                + gates[:, 2 * N_HEADS + h:2 * N_HEADS + h + 1] * o_win[j:j + 1])
    idx_ref[...] = idx_out


def _nsa_sample_sel(q, cmp_tok, win_state, kv_new, gates, layer, past):
    bs = q.shape[0]
    n_buf = win_state.shape[-1]
    n_chunk = cmp_tok.shape[2]
    return pl.pallas_call(
        functools.partial(_nsa_sample_sel_body, past=past),
        grid=(bs,),
        in_specs=[pl.BlockSpec((None, 1, GROUP_WIDTH), lambda i: (i, 0, 0)),
                  pl.BlockSpec((None, 2 * NSA_KV_HEADS, n_chunk, HEAD_DIM), lambda i: (i, 0, 0, 0)),
                  pl.BlockSpec((None, None, 2, NSA_KV_HEADS, HEAD_DIM, n_buf), lambda i: (layer, i, 0, 0, 0, 0)),
                  pl.BlockSpec((None, 1, 6 * NSA_KV_WIDTH), lambda i: (i, 0, 0)),
                  pl.BlockSpec((None, 1, LANE), lambda i: (i, 0, 0))],
        out_specs=[pl.BlockSpec((None, 8, LANE), lambda i: (i, 0, 0)),
                   pl.BlockSpec((None, 1, GROUP_WIDTH), lambda i: (i, 0, 0))],
        out_shape=[jax.ShapeDtypeStruct((bs, 8, LANE), I32), jax.ShapeDtypeStruct((bs, 1, GROUP_WIDTH), F32)],
        compiler_params=_cparams(("parallel",)),
        name="nsa_sample_sel",
    )(q, cmp_tok, win_state, kv_new, gates)


NSA_BLOCKS_PER_STEP = 4


def _nsa_sample_attn_body(pt_ref, idx_ref, *refs, n_past_blk):
    bps = NSA_BLOCKS_PER_STEP
    n_pg = NSA_KV_HEADS * bps
    ks_refs, vs_refs = refs[:n_pg], refs[n_pg:2 * n_pg]
    q_ref, kvn_ref, gate_ref, part_ref, o_ref, m_s, l_s, acc_s = refs[2 * n_pg:]
    b, i = pl.program_id(0), pl.program_id(1)
    hd = HEAD_DIM
    qrow = q_ref[...]
    kvn = kvn_ref[...]
    half = lax.broadcasted_iota(I32, (8, PAGE_SIZE), 1) // SLC_BLOCK
    for g in range(NSA_KV_HEADS):
        q8 = _heads_to_rows(qrow, g)
        ks_new = kvn[:, 2 * NSA_KV_WIDTH + g * hd:2 * NSA_KV_WIDTH + (g + 1) * hd]
        vs_new = kvn[:, 3 * NSA_KV_WIDTH + g * hd:3 * NSA_KV_WIDTH + (g + 1) * hd]

        @pl.when(i == 0)
        def _():
            m_s[g] = jnp.broadcast_to(jnp.sum(q8 * ks_new, axis=-1, keepdims=True) * ATTN_SCALE, (8, hd))
            l_s[g] = jnp.ones((8, hd), F32)
            acc_s[g] = jnp.broadcast_to(vs_new, (8, hd))

        q8b = (q8 * ATTN_SCALE).astype(BF16)
        m_i = m_s[g][:, 0:1]
        l_i = l_s[g][:, 0:1]
        acc = acc_s[g]
        for u in range(bps):
            blk = idx_ref[b, g, i * bps + u]
            mask = half == jnp.where(blk < n_past_blk, blk % (PAGE_SIZE // SLC_BLOCK), -1)
            s = jnp.where(mask, _dot(q8b, ks_refs[g * bps + u][...].astype(BF16)), NEG_INF)
            m_new = jnp.maximum(m_i, jnp.max(s, axis=-1, keepdims=True))
            pr = jnp.where(mask, jnp.exp(s - m_new), 0.0)
            alpha = jnp.exp(m_i - m_new)
            l_i = alpha * l_i + jnp.sum(pr, axis=-1, keepdims=True)
            acc = alpha * acc + _dot_nt(pr.astype(BF16), vs_refs[g * bps + u][...].astype(BF16))
            m_i = m_new
        l_s[g] = jnp.broadcast_to(l_i, (8, hd))
        acc_s[g] = acc
        m_s[g] = jnp.broadcast_to(m_i, (8, hd))

    @pl.when(i == pl.num_programs(1) - 1)
    def _():
        gates = _sigmoid(gate_ref[...])
        for g in range(NSA_KV_HEADS):
            o_sel = acc_s[g] / l_s[g]
            for j in range(NSA_GROUP):
                h = g * NSA_GROUP + j
                sl = slice(h * hd, (h + 1) * hd)
                o_ref[:, sl] = (part_ref[:, sl] + gates[:, N_HEADS + h:N_HEADS + h + 1] * o_sel[j:j + 1]
                                ).astype(o_ref.dtype)


def _nsa_sample_attn(page_table, sel_idx, nsa_cache, q, kv_new, gates, part, layer, past):
    bs = page_table.shape[0]
    n_past_blk = past // SLC_BLOCK
    per_page = PAGE_SIZE // SLC_BLOCK

    bps = NSA_BLOCKS_PER_STEP

    def blk_spec(g, u, t):
        def imap(b, i, pt, ix):
            blk = jnp.minimum(ix[b, g, i * bps + u], n_past_blk - 1)
            return (layer, pt[b, blk // per_page], t, g, 0, 0)
        return pl.BlockSpec((None, None, None, None, HEAD_DIM, PAGE_SIZE), imap)

    row = lambda w: pl.BlockSpec((None, 1, w), lambda b, i, pt, ix: (b, 0, 0))
    gu = [(g, u) for g in range(NSA_KV_HEADS) for u in range(bps)]
    return pl.pallas_call(
        functools.partial(_nsa_sample_attn_body, n_past_blk=n_past_blk),
        grid_spec=pltpu.PrefetchScalarGridSpec(
            num_scalar_prefetch=2,
            grid=(bs, SLC_TOPK // bps),
            in_specs=[blk_spec(g, u, 2) for g, u in gu] + [blk_spec(g, u, 3) for g, u in gu]
                     + [row(GROUP_WIDTH), row(6 * NSA_KV_WIDTH), row(LANE), row(GROUP_WIDTH)],
            out_specs=row(GROUP_WIDTH),
            scratch_shapes=[pltpu.VMEM((NSA_KV_HEADS, 8, HEAD_DIM), F32)] * 3),
        out_shape=jax.ShapeDtypeStruct((bs, 1, GROUP_WIDTH), F32),
        compiler_params=_cparams(("parallel", "arbitrary")),
        name="nsa_sample_attn",
    )(page_table, sel_idx, *([nsa_cache] * (2 * len(gu))), q, kv_new, gates, part)


def _recur_sample_body(c_ref, shift_ref, mu_ref, w0_ref, w2_ref, a0_ref, a2_ref, g2_ref, kk_ref, ka_ref, rk_ref,
                       lng_ref, lnb_ref, s_rw_ref, ret_ref, cos_ref, sin_ref, gam_ref, gn_g_ref, gn_b_ref, s_rt_ref,
                       o_rw_ref, s_rw_out, o_rt_ref, s_rt_out, o_scr):
    gw = GROUP_WIDTH
    hd = HEAD_DIM
    p = (mu_ref[...], w0_ref[...], w2_ref[...], a0_ref[...], a2_ref[...], g2_ref[...], kk_ref[...], ka_ref[...])
    r, w, k, v, kk, kka, gate = _rwkv_prep(c_ref[...], shift_ref[...], p)

    def put_state(h, j, s):
        s_rw_out[j, h] = s

    ys = _rwkv_steps((r, w, k, v, kk, kka), lambda h, j: s_rw_ref[j, h], put_state, 1)
    o_rw_ref[...] = _rwkv_post(ys, r, k, v, gate, rk_ref[...], lng_ref[...], lnb_ref[...]).astype(o_rw_ref.dtype)

    cos = cos_ref[...]
    sin = sin_ref[...]
    q = _rope(ret_ref[:, 0:gw], cos, sin)
    kr = _rope(ret_ref[:, gw:2 * gw], cos, sin) * ATTN_SCALE
    vr = ret_ref[:, 2 * gw:3 * gw]
    gam = gam_ref[...]
    qk = _dot_split(q * kr, _head_blockdiag(gw))
    pad = jnp.zeros((LANE - 8, LANE), F32)
    for hp in range(N_HEADS // 2):
        ps = slice(hp * LANE, (hp + 1) * LANE)
        qt = jnp.concatenate([q[:, ps], pad], axis=0).T
        kt = jnp.concatenate([kr[:, ps], pad], axis=0).T
        for h2 in range(2):
            h = 2 * hp + h2
            sl = slice(h * hd, (h + 1) * hd)
            for j in range(8):
                s = s_rt_ref[j, h]
                qcol = qt[h2 * hd:(h2 + 1) * hd, j:j + 1]
                kcol = kt[h2 * hd:(h2 + 1) * hd, j:j + 1]
                g_h = gam[:, sl]
                o_scr[j:j + 1, sl] = (qk[j:j + 1, sl] * vr[j:j + 1, sl]
                                      + g_h * jnp.sum(qcol * s, axis=0, keepdims=True))
                s_rt_out[j, h] = s * g_h + kcol * vr[j:j + 1, sl]
    gt = ret_ref[:, 3 * gw:4 * gw]
    y = _head_norm(o_scr[...], gn_g_ref[...], gn_b_ref[...], GN_EPS)
    o_rt_ref[...] = (gt * _sigmoid(gt) * y).astype(o_rt_ref.dtype)


def _recur_sample(c_rwkv, shift, rwkv_params, s_rwkv, c_ret, rope, gamma, gn_g, gn_b, s_ret):
    bs = c_rwkv.shape[0]
    cos, sin = rope
    st = jax.ShapeDtypeStruct((bs, N_HEADS, HEAD_DIM, HEAD_DIM), F32)
    ob = jax.ShapeDtypeStruct((bs, GROUP_WIDTH), BF16)
    return pl.pallas_call(
        _recur_sample_body,
        out_shape=[ob, st, ob, st],
        scratch_shapes=[pltpu.VMEM((bs, GROUP_WIDTH), F32)],
        compiler_params=pltpu.CompilerParams(vmem_limit_bytes=VMEM_LIMIT),
        name="recur_sample",
    )(c_rwkv, shift, *rwkv_params, s_rwkv, c_ret, cos, sin, gamma, gn_g, gn_b, s_ret)


def _prep_rwkv_params(mu, w0, w2, a0, a2, g2, k_k, k_a, r_k, ln_g, ln_b):
    row = lambda z: z.reshape(1, -1)
    return (row(mu), row(w0), w2, row(a0), a2, g2, row(k_k), row(k_a), row(r_k), row(ln_g), row(ln_b))


def _prep_cmp_weights(w1, b1, w2, pe):
    span = CMP_LEN // CMP_STRIDE
    w1r = w1.reshape(2, span, CMP_STRIDE, HEAD_DIM, CMP_HIDDEN)
    w1r = jnp.transpose(w1r, (0, 2, 3, 1, 4)).reshape(2, CMP_STRIDE, HEAD_DIM, span * CMP_HIDDEN).astype(BF16)
    return (w1r, w1, b1.reshape(2, 1, CMP_HIDDEN), w2, pe.reshape(2, 1, CMP_LEN * HEAD_DIM))


def _pad_w_in(w):
    o_nsa = MOBA_COLS
    o_gate = o_nsa + GROUP_WIDTH + 6 * NSA_KV_WIDTH
    o_rwkv = o_nsa + NSA_COLS
    o_ret = o_rwkv + RWKV_COLS
    zeros = jnp.zeros((w.shape[0], N_PAD - OFF_GATE - 3 * N_HEADS), w.dtype)
    return jnp.concatenate([w[:, o_rwkv:o_ret], w[:, :o_gate], w[:, o_ret:], w[:, o_gate:o_rwkv], zeros], axis=1)


def kernel(x_prompt, x_sample, cache_moba_kv, cache_nsa_kv, state_nsa_win, state_rwkv, state_rwkv_shift, state_ret,
           page_table, norm_g, w_in, w_out, w_up, w_down, nsa_cmp_pe, nsa_cmp_w1, nsa_cmp_b1, nsa_cmp_w2, rwkv_mu,
           rwkv_w0, rwkv_w2, rwkv_a0, rwkv_a2, rwkv_g2, rwkv_k_k, rwkv_k_a, rwkv_r_k, rwkv_ln_g, rwkv_ln_b, ret_gn_g,
           ret_gn_b):
    bp, t, d = x_prompt.shape
    bs = x_sample.shape[0]
    assert x_sample.shape[1] == 1 and d == D_MODEL
    depth = w_in.shape[0]
    past = page_table.shape[1] * PAGE_SIZE
    gw = GROUP_WIDTH
    hd = HEAD_DIM

    log_gamma = jnp.log(1.0 - jnp.exp2(-5.0 - jnp.arange(N_HEADS, dtype=F32)))
    ret_tables = _ret_tables(log_gamma, RET_CHUNK)
    rope_p = _rope_tables(jnp.arange(t, dtype=I32))
    rope_s = _rope_tables(jnp.full((1,), past, I32))
    gamma_row = jnp.repeat(jnp.exp(log_gamma), hd)[None, :]
    moba_t = jnp.transpose(cache_moba_kv, (0, 1, 3, 4, 5, 2))
    nsa_t = jnp.transpose(cache_nsa_kv, (0, 1, 3, 4, 5, 2))
    win_t = jnp.transpose(state_nsa_win, (0, 1, 3, 4, 5, 2))

    xp = x_prompt.reshape(bp * t, d)
    xs = x_sample.reshape(bs, d)
    zero_state = jnp.zeros((bp, N_HEADS, hd, hd), F32)
    zero_shift = jnp.zeros((bp, RWKV_COLS), F32)
    st_p, st_s = [], []
    for l in range(depth):
        g = norm_g[l].reshape(4, 1, d)
        w_in_l = _pad_w_in(w_in[l]).astype(BF16)
        w_out_l = w_out[l].astype(BF16)
        w_up_l = w_up[l].astype(BF16)
        w_down_l = w_down[l].astype(BF16)
        cmp_w = _prep_cmp_weights(nsa_cmp_w1[l], nsa_cmp_b1[l], nsa_cmp_w2[l], nsa_cmp_pe[l])
        rwkv_p = _prep_rwkv_params(rwkv_mu[l], rwkv_w0[l], rwkv_w2[l], rwkv_a0[l], rwkv_a2[l], rwkv_g2[l],
                                   rwkv_k_k[l], rwkv_k_a[l], rwkv_r_k[l], rwkv_ln_g[l], rwkv_ln_b[l])
        gn_g = ret_gn_g[l].reshape(1, gw)
        gn_b = ret_gn_b[l].reshape(1, gw)

        cols = _inproj(xp, g[0], w_in_l, min(1024, bp * t), 1024)
        o_moba = _moba_prompt(cols, bp, t)
        o_nsa = _nsa_prompt(cols, cmp_w, bp, t)
        o_rwkv, rwkv_s = _rwkv_prompt(cols, zero_shift, zero_state, rwkv_p, bp, t)
        o_ret, ret_s = _ret_prompt(cols, zero_state, ret_tables, rope_p, gn_g, gn_b, bp, t)
        xp = _outproj((o_moba, o_nsa, o_rwkv, o_ret), xp, g[1], w_out_l, 256)
        xp = _ffn(xp, g[2], g[3], w_up_l, w_down_l, min(512, bp * t), 1024)
        c3 = cols.reshape(bp, t, N_PAD)
        win_keep = min(WINDOW, t)
        st_p.append((c3[:, :, OFF_MOBA + gw:OFF_MOBA + 3 * gw].reshape(bp, t, 2, N_HEADS, hd),
                     c3[:, :, OFF_NSAKV:OFF_NSAKV + 4 * NSA_KV_WIDTH].reshape(bp, t, 4, NSA_KV_HEADS, hd),
                     c3[:, t - win_keep:, OFF_NSAKV + 4 * NSA_KV_WIDTH:OFF_NSAKV + 6 * NSA_KV_WIDTH
                        ].reshape(bp, win_keep, 2, NSA_KV_HEADS, hd),
                     rwkv_s, c3[:, t - 1, OFF_RWKV:OFF_RWKV + RWKV_COLS], ret_s))

        cs = _inproj(xs, g[0], w_in_l, bs, 512)
        heads = lambda z: z.reshape(bs, N_HEADS, 1, hd)
        q_m = cs[:, OFF_MOBA:OFF_MOBA + gw]
        k_m = cs[:, OFF_MOBA + gw:OFF_MOBA + 2 * gw]
        v_m = cs[:, OFF_MOBA + 2 * gw:OFF_MOBA + 3 * gw]
        top, cmp_tok = _past_stats(page_table, moba_t, nsa_t, q_m.reshape(bs, N_HEADS, hd, 1), cmp_w, l)
        o_moba_s = _moba_sample(page_table, top[:, :MOBA_TOPK, :], moba_t, heads(q_m), heads(k_m), heads(v_m), l)
        q_n = cs[:, OFF_NSAQ:OFF_NSAQ + gw].reshape(bs, 1, gw)
        kv_new = cs[:, OFF_NSAKV:OFF_NSAKV + 6 * NSA_KV_WIDTH].reshape(bs, 1, 6 * NSA_KV_WIDTH)
        gates = cs[:, OFF_GATE:OFF_GATE + LANE].reshape(bs, 1, LANE)
        sel_idx, part = _nsa_sample_sel(q_n, cmp_tok, win_t, kv_new, gates, l, past)
        o_nsa_s = _nsa_sample_attn(page_table, sel_idx[:, :NSA_KV_HEADS, :SLC_TOPK], nsa_t, q_n, kv_new, gates, part,
                                   l, past)
        o_rwkv_s, rwkv_s_s, o_ret_s, ret_s_s = _recur_sample(
            cs[:, OFF_RWKV:OFF_RWKV + RWKV_COLS], state_rwkv_shift[l], rwkv_p, state_rwkv[l],
            cs[:, OFF_RET:OFF_RET + RET_COLS], rope_s, gamma_row, gn_g, gn_b, state_ret[l])
        parts_s = (o_moba_s.reshape(bs, gw).astype(BF16), o_nsa_s.reshape(bs, gw).astype(BF16), o_rwkv_s, o_ret_s)
        xs = _outproj(parts_s, xs, g[1], w_out_l, bs)
        xs = _ffn(xs, g[2], g[3], w_up_l, w_down_l, bs, 512)
        win_new = cs[:, OFF_NSAKV + 4 * NSA_KV_WIDTH:OFF_NSAKV + 6 * NSA_KV_WIDTH].reshape(bs, 1, 2, NSA_KV_HEADS, hd)
        win_all = jnp.concatenate([state_nsa_win[l], win_new], axis=1)
        keep_s = min(WINDOW, win_all.shape[1])
        st_s.append((cs[:, OFF_MOBA + gw:OFF_MOBA + 3 * gw].reshape(bs, 1, 2, N_HEADS, hd),
                     cs[:, OFF_NSAKV:OFF_NSAKV + 4 * NSA_KV_WIDTH].reshape(bs, 1, 4, NSA_KV_HEADS, hd),
                     win_all[:, win_all.shape[1] - keep_s:],
                     rwkv_s_s, cs[:, OFF_RWKV:OFF_RWKV + RWKV_COLS], ret_s_s))

    stk = lambda sts, i: jnp.stack([s[i] for s in sts], axis=0)
    outs = [xp.reshape(bp, t, d), xs.reshape(bs, 1, d)]
    for i in range(6):
        outs += [stk(st_p, i), stk(st_s, i)]
    return tuple(outs)
```

```python
import functools

import jax
import jax.numpy as jnp
import numpy as np
from jax import lax
from jax.experimental import pallas as pl
from jax.experimental.pallas import tpu as pltpu

F32 = jnp.float32
BF16 = jnp.bfloat16
I32 = jnp.int32
HI = lax.Precision.HIGHEST

D_MODEL = 2048
HEAD_DIM = 64
GROUP_WIDTH = D_MODEL // 4
N_HEADS = GROUP_WIDTH // HEAD_DIM
D_FF = 4 * D_MODEL
RMS_EPS = 1e-6
GN_EPS = 1e-5
NEG_INF = -1e30
ATTN_SCALE = HEAD_DIM ** -0.5
PAGE_SIZE = 128
MOBA_BLOCK = 256
MOBA_TOPK = 3
NSA_KV_HEADS = 2
NSA_GROUP = N_HEADS // NSA_KV_HEADS
NSA_KV_WIDTH = NSA_KV_HEADS * HEAD_DIM
CMP_LEN = 32
CMP_STRIDE = 16
CMP_HIDDEN = 2 * HEAD_DIM
SLC_BLOCK = 64
SLC_TOPK = 16
WINDOW = 512
RWKV_DECAY_RANK = 64
RWKV_AAA_RANK = 64
RWKV_GATE_RANK = 128
RWKV_LN_EPS = 64e-5
RET_CHUNK = 128
ROPE_BASE = 10000.0

MOBA_COLS = 3 * GROUP_WIDTH
NSA_COLS = GROUP_WIDTH + 6 * NSA_KV_WIDTH + 3 * N_HEADS
RWKV_COLS = 3 * GROUP_WIDTH + RWKV_DECAY_RANK + RWKV_AAA_RANK + RWKV_GATE_RANK
RET_COLS = 4 * GROUP_WIDTH
IN_COLS = MOBA_COLS + NSA_COLS + RWKV_COLS + RET_COLS

OFF_RWKV = 0
OFF_MOBA = OFF_RWKV + RWKV_COLS
OFF_NSAQ = OFF_MOBA + MOBA_COLS
OFF_NSAKV = OFF_NSAQ + GROUP_WIDTH
OFF_RET = OFF_NSAKV + 6 * NSA_KV_WIDTH
OFF_GATE = OFF_RET + RET_COLS
N_PAD = OFF_GATE + 512

LANE = 128
VMEM_LIMIT = 56 * 1024 * 1024


def _cparams(sem):
    return pltpu.CompilerParams(dimension_semantics=sem, vmem_limit_bytes=VMEM_LIMIT)


def _dot(a, b, precision=None):
    return jnp.dot(a, b, preferred_element_type=F32, precision=precision)


def _dot_nt(a, b, precision=None):
    return lax.dot_general(a, b, (((1,), (1,)), ((), ())), preferred_element_type=F32, precision=precision)


def _dot_tn(a, b, precision=None):
    return lax.dot_general(a, b, (((0,), (0,)), ((), ())), preferred_element_type=F32, precision=precision)


def _head_blockdiag(n, scale=1.0):
    r = lax.broadcasted_iota(I32, (n, n), 0) // HEAD_DIM
    c = lax.broadcasted_iota(I32, (n, n), 1) // HEAD_DIM
    return jnp.where(r == c, scale, 0.0).astype(F32)


def _dot_split(x, m):
    hi = x.astype(BF16)
    lo = (x - hi.astype(F32)).astype(BF16)
    mb = m.astype(BF16)
    return _dot(hi, mb) + _dot(lo, mb)


def _sigmoid(x):
    return 1.0 / (1.0 + jnp.exp(-x))


def _inproj_body(x_ref, g_ref, w_ref, o_ref, h_scr):
    @pl.when(pl.program_id(1) == 0)
    def _():
        x = x_ref[...]
        ms = jnp.mean(x * x, axis=-1, keepdims=True)
        h_scr[...] = (x * lax.rsqrt(ms + RMS_EPS) * g_ref[...]).astype(BF16)

    o_ref[...] = _dot(h_scr[...], w_ref[...])


def _inproj(x, g, w, tm, tn):
    m, d = x.shape
    n = w.shape[1]
    return pl.pallas_call(
        _inproj_body,
        grid=(m // tm, n // tn),
        in_specs=[pl.BlockSpec((tm, d), lambda i, j: (i, 0)),
                  pl.BlockSpec((1, d), lambda i, j: (0, 0)),
                  pl.BlockSpec((d, tn), lambda i, j: (0, j))],
        out_specs=pl.BlockSpec((tm, tn), lambda i, j: (i, j)),
        out_shape=jax.ShapeDtypeStruct((m, n), F32),
        scratch_shapes=[pltpu.VMEM((tm, d), BF16)],
        compiler_params=_cparams(("parallel", "arbitrary")),
        name="inproj",
    )(x, g, w)


def _outproj_body(a_ref, b_ref, c_ref, d_ref, x_ref, g_ref, w_ref, o_ref):
    gw = GROUP_WIDTH
    y = _dot(a_ref[...], w_ref[0:gw, :])
    y += _dot(b_ref[...], w_ref[gw:2 * gw, :])
    y += _dot(c_ref[...], w_ref[2 * gw:3 * gw, :])
    y += _dot(d_ref[...], w_ref[3 * gw:4 * gw, :])
    ms = jnp.mean(y * y, axis=-1, keepdims=True)
    o_ref[...] = x_ref[...] + y * lax.rsqrt(ms + RMS_EPS) * g_ref[...]


def _outproj(parts, x, g, w, tm):
    m, d = x.shape
    gw = GROUP_WIDTH
    part_spec = pl.BlockSpec((tm, gw), lambda i: (i, 0))
    return pl.pallas_call(
        _outproj_body,
        grid=(m // tm,),
        in_specs=[part_spec, part_spec, part_spec, part_spec,
                  pl.BlockSpec((tm, d), lambda i: (i, 0)),
                  pl.BlockSpec((1, d), lambda i: (0, 0)),
                  pl.BlockSpec((4 * gw, d), lambda i: (0, 0))],
        out_specs=pl.BlockSpec((tm, d), lambda i: (i, 0)),
        out_shape=jax.ShapeDtypeStruct((m, d), F32),
        compiler_params=_cparams(("parallel",)),
        name="outproj",
    )(*parts, x, g, w)


def _ffn_body(x_ref, g2_ref, g3_ref, wu_ref, wd_ref, o_ref, h_scr, acc_scr):
    f = pl.program_id(1)

    @pl.when(f == 0)
    def _():
        x = x_ref[...]
        ms = jnp.mean(x * x, axis=-1, keepdims=True)
        h_scr[...] = (x * lax.rsqrt(ms + RMS_EPS) * g2_ref[...]).astype(BF16)
        acc_scr[...] = jnp.zeros_like(acc_scr)

    u = jnp.maximum(_dot(h_scr[...], wu_ref[...]), 0.0)
    acc_scr[...] += _dot((u * u).astype(BF16), wd_ref[...])

    @pl.when(f == pl.num_programs(1) - 1)
    def _():
        y = acc_scr[...]
        ms = jnp.mean(y * y, axis=-1, keepdims=True)
        o_ref[...] = x_ref[...] + y * lax.rsqrt(ms + RMS_EPS) * g3_ref[...]


def _ffn(x, g2, g3, wu, wd, tm, tf):
    m, d = x.shape
    f = wu.shape[1]
    return pl.pallas_call(
        _ffn_body,
        grid=(m // tm, f // tf),
        in_specs=[pl.BlockSpec((tm, d), lambda i, j: (i, 0)),
                  pl.BlockSpec((1, d), lambda i, j: (0, 0)),
                  pl.BlockSpec((1, d), lambda i, j: (0, 0)),
                  pl.BlockSpec((d, tf), lambda i, j: (0, j)),
                  pl.BlockSpec((tf, d), lambda i, j: (j, 0))],
        out_specs=pl.BlockSpec((tm, d), lambda i, j: (i, 0)),
        out_shape=jax.ShapeDtypeStruct((m, d), F32),
        scratch_shapes=[pltpu.VMEM((tm, d), BF16), pltpu.VMEM((tm, d), F32)],
        compiler_params=_cparams(("parallel", "arbitrary")),
        name="ffn",
    )(x, g2, g3, wu, wd)


MOBA_HEAD_GROUP = 4


def _moba_prompt_body(q_ref, k_ref, v_ref, o_ref, kmean_scr, kext, vext, *, nb):
    blk = MOBA_BLOCK
    hg = MOBA_HEAD_GROUP
    qt = pl.program_id(2)
    pair_of = lambda j: slice(LANE * (j // 2), LANE * (j // 2 + 1))
    dst_of = lambda j: slice(LANE * j, LANE * (j + 1))

    @pl.when(qt == 0)
    def _():
        kmean_scr[...] = jnp.zeros_like(kmean_scr)
        mine_t = lax.broadcasted_iota(I32, (k_ref.shape[0], LANE), 1) // HEAD_DIM
        for j in range(hg):
            kp = jnp.where(mine_t == j % 2, k_ref[:, pair_of(j)], 0.0)
            kext[:, dst_of(j)] = kp.astype(BF16)
            vext[:, dst_of(j)] = jnp.where(mine_t == j % 2, v_ref[:, pair_of(j)], 1.0).astype(BF16)
            for n in range(nb):
                kmean_scr[n:n + 1, dst_of(j)] = jnp.sum(kp[n * blk:(n + 1) * blk], axis=0, keepdims=True) * (1.0 / blk)

    row = lax.broadcasted_iota(I32, (blk, blk), 0)
    col = lax.broadcasted_iota(I32, (blk, blk), 1)
    bias_own = jnp.where(col <= row, 0.0, NEG_INF)
    nb_r = kmean_scr.shape[0]
    blk_id = lax.broadcasted_iota(I32, (nb_r, blk), 0)
    eligible = blk_id < qt
    not_sel, qb = [], []
    for j in range(hg):
        q = q_ref[:, pair_of(j)]
        gate = jnp.where(eligible, _dot_nt(kmean_scr[:, dst_of(j)], q, HI), -jnp.inf)
        rank = jnp.zeros((nb_r, blk), I32)
        for m in range(nb):
            gm = gate[m:m + 1, :]
            rank += ((gm > gate) | ((gm == gate) & (m < blk_id))).astype(I32)
        ns_t = jnp.where(eligible & (rank < MOBA_TOPK), 0.0, 1.0)
        not_sel.append(jnp.concatenate([ns_t, jnp.ones((LANE - nb_r, blk), F32)], axis=0).T.astype(BF16))
        qb.append((q * ATTN_SCALE).astype(BF16))

    def attend(carry, start, biases):
        out = []
        for j in range(hg):
            m_i, acc = carry[j]
            s = _dot_nt(qb[j], kext[pl.ds(start, blk), dst_of(j)]) + biases[j]
            m_new = jnp.maximum(m_i, jnp.max(s, axis=-1, keepdims=True))
            p = jnp.exp(s - m_new)
            out.append((m_new, jnp.exp(m_i - m_new) * acc + _dot(p.astype(BF16), vext[pl.ds(start, blk), dst_of(j)])))
        return tuple(out)

    def body(n, carry):
        pick = (lax.broadcasted_iota(I32, (LANE, blk), 0) == n).astype(BF16)
        return attend(carry, pl.multiple_of(n * blk, blk), [_dot(ns, pick) * NEG_INF for ns in not_sel])

    init = tuple((jnp.full((blk, 1), NEG_INF, F32), jnp.zeros((blk, LANE), F32)) for _ in range(hg))
    carry = lax.fori_loop(0, qt, body, init)
    fin = attend(carry, pl.multiple_of(qt * blk, blk), [bias_own] * hg)
    half = lax.broadcasted_iota(I32, (blk, LANE), 1) // HEAD_DIM
    for jp in range(hg // 2):
        outs = [fin[2 * jp + e][1] / pltpu.roll(fin[2 * jp + e][1], HEAD_DIM, 1) for e in range(2)]
        o_ref[:, dst_of(jp)] = jnp.where(half == 0, outs[0], outs[1]).astype(o_ref.dtype)


def _moba_prompt(cols, b, t):
    blk = MOBA_BLOCK
    nb = t // blk
    w = MOBA_HEAD_GROUP * HEAD_DIM
    qoff = OFF_MOBA // w
    koff = (OFF_MOBA + GROUP_WIDTH) // w
    voff = (OFF_MOBA + 2 * GROUP_WIDTH) // w
    return pl.pallas_call(
        functools.partial(_moba_prompt_body, nb=nb),
        grid=(b, GROUP_WIDTH // w, nb),
        in_specs=[pl.BlockSpec((blk, w), lambda i, h, q: (i * nb + q, qoff + h)),
                  pl.BlockSpec((t, w), lambda i, h, q: (i, koff + h)),
                  pl.BlockSpec((t, w), lambda i, h, q: (i, voff + h))],
        out_specs=pl.BlockSpec((blk, w), lambda i, h, q: (i * nb + q, h)),
        out_shape=jax.ShapeDtypeStruct((b * t, GROUP_WIDTH), BF16),
        scratch_shapes=[pltpu.VMEM((-(-nb // 8) * 8, MOBA_HEAD_GROUP * LANE), F32),
                        pltpu.VMEM((t, MOBA_HEAD_GROUP * LANE), BF16), pltpu.VMEM((t, MOBA_HEAD_GROUP * LANE), BF16)],
        compiler_params=_cparams(("parallel", "parallel", "arbitrary")),
        name="moba_prompt",
    )(cols, cols, cols)


def _gelu_tanh(x):
    return x * (0.5 * (1.0 + jnp.tanh(np.sqrt(2.0 / np.pi).astype(np.float32) * (x + 0.044715 * (x * x * x)))))


def _cmp_bias(pe_ref, w1_ref, b1_ref, kv):
    pe8 = jnp.broadcast_to(pe_ref[kv], (8, CMP_LEN * HEAD_DIM))
    return _dot(pe8, w1_ref[kv], HI)[0:1, :] + b1_ref[kv]


def _compress(load_rows, n, w1r_ref, kv, bias, w2):
    acc = [jnp.zeros((n, 2 * CMP_HIDDEN), F32) for _ in range(NSA_KV_HEADS)]
    for r in range(CMP_STRIDE):
        x = load_rows(r).astype(BF16)
        for g in range(NSA_KV_HEADS):
            acc[g] += _dot(x[:, g * HEAD_DIM:(g + 1) * HEAD_DIM], w1r_ref[kv, r])
    out = []
    for g in range(NSA_KV_HEADS):
        hid = acc[g][:, :CMP_HIDDEN] + pltpu.roll(acc[g][:, CMP_HIDDEN:], n - 1, 0) + bias
        out.append(_dot(_gelu_tanh(hid).astype(BF16), w2.astype(BF16)))
    return out


def _softmax_rows(l, mask):
    m = jnp.max(jnp.where(mask, l, NEG_INF), axis=-1, keepdims=True)
    e = jnp.where(mask, jnp.exp(l - m), 0.0)
    s = jnp.sum(e, axis=-1, keepdims=True)
    return jnp.where(s > 0.0, e / jnp.where(s > 0.0, s, 1.0), 0.0)


def _slc_matrix(n_cmp_pad, n_cmp, n_slc_pad):
    n = lax.broadcasted_iota(I32, (n_cmp_pad, n_slc_pad), 0)
    j = lax.broadcasted_iota(I32, (n_cmp_pad, n_slc_pad), 1)
    ratio = SLC_BLOCK // CMP_STRIDE
    return ((n >= ratio * j - 1) & (n <= ratio * j + ratio - 1) & (n < n_cmp)).astype(F32)


def _topk_rows(score, ids, n_cand, k):
    rank = jnp.zeros(score.shape, I32)
    for m in range(n_cand):
        sm = score[:, m:m + 1]
        rank += ((sm > score) | ((sm == score) & (m < ids))).astype(I32)
    return rank < k


def _nsa_prompt_body(q0_ref, q1_ref, kc_ref, vc_ref, ksvs_ref, kwvw_ref, gate_ref, w1r_ref, w1_ref, b1_ref, w2_ref,
                     pe_ref, o_ref, ck_scr, cv_scr, ks_ext, vs_ext, kw_ext, vw_ext, *, t):
    tq = 256
    n_chunk = t // CMP_STRIDE
    n_cmp = n_chunk - CMP_LEN // CMP_STRIDE + 1
    qt = pl.program_id(1)
    hd = HEAD_DIM

    @pl.when(qt == 0)
    def _():
        for kv, (src, dst) in enumerate(((kc_ref, ck_scr), (vc_ref, cv_scr))):
            bias = _cmp_bias(pe_ref, w1_ref, b1_ref, kv)
            out = _compress(lambda r: src[pl.ds(r, n_chunk, stride=CMP_STRIDE), :],
                            n_chunk, w1r_ref, kv, bias, w2_ref[kv])
            for g in range(NSA_KV_HEADS):
                dst[g] = out[g]
        half_t = lax.broadcasted_iota(I32, (t, LANE), 1) // hd
        for src, k_ext, v_ext in ((ksvs_ref, ks_ext, vs_ext), (kwvw_ref, kw_ext, vw_ext)):
            for g in range(NSA_KV_HEADS):
                k_ext[g] = jnp.where(half_t == g, src[:, 0:LANE], 0.0).astype(BF16)
                v_ext[g] = jnp.where(half_t == g, src[:, LANE:2 * LANE], 1.0).astype(BF16)

    tpos = qt * tq + lax.broadcasted_iota(I32, (tq, 1), 0)
    tpos4 = jnp.concatenate([tpos] * NSA_GROUP, axis=0)
    gates = _sigmoid(gate_ref[...])
    slc_ids = lax.broadcasted_iota(I32, (tq, LANE), 1)
    n_slc = t // SLC_BLOCK
    kcol = lax.broadcasted_iota(I32, (tq, tq), 1)
    jrow = lax.broadcasted_iota(I32, (LANE, tq), 0)
    kcol_e = lax.broadcasted_iota(I32, (LANE, tq), 1)
    for g in range(NSA_KV_HEADS):
        q_ref = q0_ref if g == 0 else q1_ref
        q4f = jnp.concatenate([q_ref[:, j * hd:(j + 1) * hd] for j in range(NSA_GROUP)], axis=0)
        q4 = q4f.astype(BF16)
        cmp_end = lax.broadcasted_iota(I32, (NSA_GROUP * tq, n_chunk), 1) * CMP_STRIDE + (CMP_LEN - 1)
        l_cmp = _dot_nt(q4, ck_scr[g].astype(BF16)) * ATTN_SCALE
        p_cmp = _softmax_rows(l_cmp, cmp_end <= tpos4)
        o_cmp = _dot(p_cmp.astype(BF16), cv_scr[g].astype(BF16))
        imp = p_cmp[0:tq]
        for j in range(1, NSA_GROUP):
            imp = imp + p_cmp[j * tq:(j + 1) * tq]
        n_slc_r = -(-n_slc // 8) * 8
        jn = lax.broadcasted_iota(I32, (n_slc_r, n_chunk), 0)
        nn = lax.broadcasted_iota(I32, (n_slc_r, n_chunk), 1)
        ratio = SLC_BLOCK // CMP_STRIDE
        slc_t = ((nn >= ratio * jn - 1) & (nn <= ratio * jn + ratio - 1) & (nn < n_cmp)).astype(F32)
        imp_t = _dot_nt(slc_t, imp, HI)
        jt = lax.broadcasted_iota(I32, (n_slc_r, tq), 0)
        cur_t = (qt * tq + lax.broadcasted_iota(I32, (n_slc_r, tq), 1)) // SLC_BLOCK
        eligible = (jt <= cur_t) & (jt < n_slc)
        forced = (jt == 0) | (jt == cur_t) | (jt == cur_t - 1)
        score = jnp.where(eligible, jnp.where(forced, jnp.inf, imp_t), -jnp.inf)
        rank = jnp.zeros((n_slc_r, tq), I32)
        for m in range(n_slc):
            sm = score[m:m + 1, :]
            rank += ((sm > score) | ((sm == score) & (m < jt))).astype(I32)
        sel_t = jnp.where(eligible & (rank < SLC_TOPK), 1.0, 0.0)
        sel = jnp.concatenate([sel_t, jnp.zeros((LANE - n_slc_r, tq), F32)], axis=0).T.astype(BF16)

        half_q = lax.broadcasted_iota(I32, (tq, LANE), 1) // hd
        q4e = []
        for j in range(NSA_GROUP):
            qp = q_ref[:, (j // 2) * LANE:(j // 2 + 1) * LANE]
            qp = qp if j % 2 == g else pltpu.roll(qp, hd, 1)
            q4e.append(jnp.where(half_q == g, qp * ATTN_SCALE, 0.0).astype(BF16))
        q4e = jnp.concatenate(q4e, axis=0)

        def attend(carry, start, bias, k_ext, v_ext):
            m_i, acc = carry
            s = (_dot_nt(q4e, k_ext[g, pl.ds(start, tq), :]).reshape(NSA_GROUP, tq, tq) + bias[None]
                 ).reshape(NSA_GROUP * tq, tq)
            m_new = jnp.maximum(m_i, jnp.max(s, axis=-1, keepdims=True))
            p = jnp.exp(s - m_new)
            return m_new, jnp.exp(m_i - m_new) * acc + _dot(p.astype(BF16), v_ext[g, pl.ds(start, tq), :])

        def sel_body(n, carry):
            expand = (jrow == (tq // SLC_BLOCK) * n + kcol_e // SLC_BLOCK).astype(BF16)
            mask = (_dot(sel, expand) > 0.5) & (n * tq + kcol <= tpos)
            return attend(carry, pl.multiple_of(n * tq, tq), jnp.where(mask, 0.0, NEG_INF), ks_ext, vs_ext)

        def win_body(n, carry):
            dist = tpos - (n * tq + kcol)
            bias = jnp.where((dist >= 0) & (dist < WINDOW), 0.0, NEG_INF)
            return attend(carry, pl.multiple_of(n * tq, tq), bias, kw_ext, vw_ext)

        rows = NSA_GROUP * tq
        init = (jnp.full((rows, 1), NEG_INF, F32), jnp.zeros((rows, LANE), F32))
        _, acc_s = lax.fori_loop(0, qt + 1, sel_body, init)
        _, acc_w = lax.fori_loop(jnp.maximum(qt - (WINDOW // tq), 0), qt + 1, win_body, init)
        o_sel = (acc_s / pltpu.roll(acc_s, hd, 1))[:, g * hd:(g + 1) * hd]
        o_win = (acc_w / pltpu.roll(acc_w, hd, 1))[:, g * hd:(g + 1) * hd]
        for j in range(NSA_GROUP):
            h = g * NSA_GROUP + j
            rs = slice(j * tq, (j + 1) * tq)
            o = (gates[:, h:h + 1] * o_cmp[rs] + gates[:, N_HEADS + h:N_HEADS + h + 1] * o_sel[rs]
                 + gates[:, 2 * N_HEADS + h:2 * N_HEADS + h + 1] * o_win[rs])
            o_ref[:, h * hd:(h + 1) * hd] = o.astype(o_ref.dtype)


def _nsa_prompt(cols, cmp_w, b, t):
    w1r, w1, b1, w2, pe = cmp_w
    tq = 256
    nq = t // tq
    n_chunk = t // CMP_STRIDE
    qoff = OFF_NSAQ // 256
    kvoff = OFF_NSAKV // 256
    full = lambda a: pl.BlockSpec(a.shape, lambda i, q: (0,) * a.ndim)
    return pl.pallas_call(
        functools.partial(_nsa_prompt_body, t=t),
        grid=(b, nq),
        in_specs=[pl.BlockSpec((tq, 256), lambda i, q: (i * nq + q, qoff)),
                  pl.BlockSpec((tq, 256), lambda i, q: (i * nq + q, qoff + 1)),
                  pl.BlockSpec((t, LANE), lambda i, q: (i, 2 * kvoff)),
                  pl.BlockSpec((t, LANE), lambda i, q: (i, 2 * kvoff + 1)),
                  pl.BlockSpec((t, 256), lambda i, q: (i, kvoff + 1)),
                  pl.BlockSpec((t, 256), lambda i, q: (i, kvoff + 2)),
                  pl.BlockSpec((tq, LANE), lambda i, q: (i * nq + q, OFF_GATE // LANE)),
                  full(w1r), full(w1), full(b1), full(w2), full(pe)],
        out_specs=pl.BlockSpec((tq, GROUP_WIDTH), lambda i, q: (i * nq + q, 0)),
        out_shape=jax.ShapeDtypeStruct((b * t, GROUP_WIDTH), BF16),
        scratch_shapes=[pltpu.VMEM((NSA_KV_HEADS, n_chunk, HEAD_DIM), F32),
                        pltpu.VMEM((NSA_KV_HEADS, n_chunk, HEAD_DIM), F32)]
                       + [pltpu.VMEM((NSA_KV_HEADS, t, LANE), BF16)] * 4,
        compiler_params=_cparams(("parallel", "arbitrary")),
        name="nsa_prompt",
    )(cols, cols, cols, cols, cols, cols, cols, w1r, w1, b1, w2, pe)


def _rope(x, cos, sin):
    half = HEAD_DIM // 2
    lane = lax.broadcasted_iota(I32, x.shape, 1) % HEAD_DIM
    nxt = pltpu.roll(x, x.shape[1] - half, 1)
    prv = pltpu.roll(x, half, 1)
    return x * cos + jnp.where(lane < half, -nxt, prv) * sin


def _head_norm(y, g, b, eps):
    avg = _head_blockdiag(y.shape[1], 1.0 / HEAD_DIM)
    mu = _dot_split(y, avg)
    d = y - mu
    var = _dot_split(d * d, avg)
    return d * lax.rsqrt(var + eps) * g + b


def _ret_prompt_body(q_ref, k_ref, v_ref, gate_ref, cos_ref, sin_ref, dmask_ref, xi_ref, zeta_ref, cd_ref,
                     gn_g_ref, gn_b_ref, s0_ref, o_ref, s_ref, o_scr):
    @pl.when(pl.program_id(1) == 0)
    def _():
        s_ref[...] = s0_ref[...]

    cos = cos_ref[...]
    sin = sin_ref[...]
    q = _rope(q_ref[...], cos, sin)
    k = _rope(k_ref[...], cos, sin) * ATTN_SCALE
    kz = (k * zeta_ref[...]).astype(BF16)
    qb = q.astype(BF16)
    kb = k.astype(BF16)
    for h in range(N_HEADS):
        sl = slice(h * HEAD_DIM, (h + 1) * HEAD_DIM)
        vb = v_ref[:, sl].astype(BF16)
        s = s_ref[0, h]
        att = _dot_nt(qb[:, sl], kb[:, sl]) * dmask_ref[h]
        o_scr[:, sl] = _dot(att.astype(BF16), vb) + _dot(qb[:, sl], s.astype(BF16)) * xi_ref[:, sl]
        s_ref[0, h] = s * cd_ref[:, sl] + _dot_tn(kz[:, sl], vb)
    gate = gate_ref[...]
    y = _head_norm(o_scr[...], gn_g_ref[...], gn_b_ref[...], GN_EPS)
    o_ref[...] = (gate * _sigmoid(gate) * y).astype(o_ref.dtype)


def _ret_tables(log_gamma, c):
    idx = jnp.arange(c, dtype=F32)
    diff = idx[:, None] - idx[None, :]
    dmask = jnp.where(diff >= 0, jnp.exp(jnp.maximum(diff, 0.0)[None] * log_gamma[:, None, None]), 0.0)
    rep = lambda z: jnp.repeat(z, HEAD_DIM, axis=-1)
    xi = rep(jnp.exp((idx + 1.0)[:, None] * log_gamma[None, :]))
    zeta = rep(jnp.exp((c - 1.0 - idx)[:, None] * log_gamma[None, :]))
    cd = rep(jnp.exp(c * log_gamma)[None, :])
    return dmask, xi, zeta, cd


def _rope_tables(pos):
    half = HEAD_DIM // 2
    inv = ROPE_BASE ** (-jnp.arange(half, dtype=F32) / half)
    ang = pos.astype(F32)[:, None] * inv[None, :]
    tile = lambda z: jnp.tile(z, (1, 2 * N_HEADS))
    return tile(jnp.cos(ang)), tile(jnp.sin(ang))


def _ret_prompt(cols, s0, tables, rope, gn_g, gn_b, b, t):
    c = RET_CHUNK
    nc = t // c
    gw = GROUP_WIDTH
    off = OFF_RET // gw
    dmask, xi, zeta, cd = tables
    cos, sin = rope
    col_spec = lambda j: pl.BlockSpec((c, gw), lambda i, n: (i * nc + n, off + j))
    full = lambda a: pl.BlockSpec(a.shape, lambda i, n: (0,) * a.ndim)
    st_spec = pl.BlockSpec((1, N_HEADS, HEAD_DIM, HEAD_DIM), lambda i, n: (i, 0, 0, 0))
    return pl.pallas_call(
        _ret_prompt_body,
        grid=(b, nc),
        in_specs=[col_spec(0), col_spec(1), col_spec(2), col_spec(3),
                  pl.BlockSpec((c, gw), lambda i, n: (n, 0)), pl.BlockSpec((c, gw), lambda i, n: (n, 0)),
                  full(dmask), full(xi), full(zeta), full(cd), full(gn_g), full(gn_b), st_spec],
        out_specs=[pl.BlockSpec((c, gw), lambda i, n: (i * nc + n, 0)), st_spec],
        out_shape=[jax.ShapeDtypeStruct((b * t, gw), BF16),
                   jax.ShapeDtypeStruct((b, N_HEADS, HEAD_DIM, HEAD_DIM), F32)],
        scratch_shapes=[pltpu.VMEM((c, gw), F32)],
        compiler_params=_cparams(("parallel", "arbitrary")),
        name="ret_prompt",
    )(cols, cols, cols, cols, cos, sin, dmask, xi, zeta, cd, gn_g, gn_b, s0)


def _softplus(x):
    return jnp.maximum(x, 0.0) + jnp.log(1.0 + jnp.exp(-jnp.abs(x)))


def _rwkv_prep(c, prev, p):
    mu, w0, w2, a0, a2, g2, k_k, k_a = p
    gw = GROUP_WIDTH
    mixed = c + (prev - c) * mu
    r, k, v = mixed[:, :gw], mixed[:, gw:2 * gw], mixed[:, 2 * gw:3 * gw]
    o1 = 3 * gw
    o2 = o1 + RWKV_DECAY_RANK
    o3 = o2 + RWKV_AAA_RANK
    xw, xa, xg = mixed[:, o1:o2], mixed[:, o2:o3], mixed[:, o3:]
    w_log = -_softplus(-(w0 + _dot(jnp.tanh(xw), w2, HI))) - 0.5
    decay = jnp.exp(-jnp.exp(w_log))
    a = _sigmoid(a0 + _dot(xa, a2, HI))
    gate = _dot(_sigmoid(xg).astype(BF16), g2.astype(BF16))
    kk = k * k_k
    norm = jnp.sqrt(_dot_split(kk * kk, _head_blockdiag(gw)))
    kk = kk / jnp.maximum(norm, 1e-12)
    k = k * (1.0 + (a - 1.0) * k_a)
    return r, decay, k, v, kk, kk * a, gate


def _rwkv_steps(vecs, get_state, put_state, n_steps):
    r8, w8, k8, v8, kk8, ka8 = vecs
    lane = lax.broadcasted_iota(I32, (HEAD_DIM, LANE), 1)
    pad = jnp.zeros((LANE - 8, LANE), F32)
    ys = []
    for hp in range(N_HEADS // 2):
        vt = jnp.concatenate([v8[:, hp * LANE:(hp + 1) * LANE], pad], axis=0).T
        yts = []
        for h2 in range(2):
            h = 2 * hp + h2
            sl = slice(h * HEAD_DIM, (h + 1) * HEAD_DIM)
            yt = jnp.zeros((HEAD_DIM, LANE), F32)
            s = get_state(h, 0)
            for j in range(8):
                if n_steps == 1:
                    s = get_state(h, j)
                vcol = vt[h2 * HEAD_DIM:(h2 + 1) * HEAD_DIM, j:j + 1]
                sa = jnp.sum(s * kk8[j:j + 1, sl], axis=-1, keepdims=True)
                s = s * w8[j:j + 1, sl] - sa * ka8[j:j + 1, sl] + vcol * k8[j:j + 1, sl]
                ycol = jnp.sum(s * r8[j:j + 1, sl], axis=-1, keepdims=True)
                yt = jnp.where(lane == j, ycol, yt)
                if n_steps == 1:
                    put_state(h, j, s)
            if n_steps != 1:
                put_state(h, 0, s)
            yts.append(yt)
        ys.append(jnp.concatenate(yts, axis=0).T[0:8, :])
    return jnp.concatenate(ys, axis=1)


def _rwkv_post(ys, r, k, v, gate, r_k, ln_g, ln_b):
    y = _head_norm(ys, ln_g, ln_b, RWKV_LN_EPS)
    y = y + _dot_split(r * k * r_k, _head_blockdiag(GROUP_WIDTH)) * v
    return y * gate


def _rwkv_prep_body(c_ref, shift_ref, mu_ref, w0_ref, w2_ref, a0_ref, a2_ref, g2_ref, kk_ref, ka_ref, rk_ref,
                    keys_o, v_o, bonus_o, gate_o, carry):
    tc = c_ref.shape[0]

    @pl.when(pl.program_id(1) == 0)
    def _():
        carry[...] = shift_ref[...]

    c = c_ref[...]
    row = lax.broadcasted_iota(I32, c.shape, 0)
    prev = jnp.where(row == 0, carry[...], pltpu.roll(c, 1, 0))
    carry[...] = c[tc - 1:tc, :]
    p = (mu_ref[...], w0_ref[...], w2_ref[...], a0_ref[...], a2_ref[...], g2_ref[...], kk_ref[...], ka_ref[...])
    r, w, k, v, kk, kka, gate = _rwkv_prep(c, prev, p)
    for i, z in enumerate((w, kk, kka, k, r)):
        keys_o[i] = z.T
    v_o[...] = v.T
    bonus_o[...] = _dot_split(r * k * rk_ref[...], _head_blockdiag(GROUP_WIDTH)) * v
    gate_o[...] = gate


def _rwkv_relayout_body(*refs):
    x_refs, o_ref, scr = refs[:-2], refs[-2], refs[-1]
    tt = x_refs[0].shape[-1]
    n_rep = LANE // (len(x_refs) * N_HEADS)
    for k in range(HEAD_DIM):
        rows = [x[pl.ds(k, N_HEADS, stride=HEAD_DIM), :] for x in x_refs]
        scr[k * tt:(k + 1) * tt, :] = jnp.concatenate(rows * n_rep, axis=0).T

    def regroup(g, _):
        for u in range(8):
            tok = g * 8 + u
            for kb in range(HEAD_DIM // 8):
                o_ref[pl.ds(pl.multiple_of(tok * HEAD_DIM + kb * 8, 8), 8), :] = (
                    scr[pl.ds(kb * 8 * tt + tok, 8, stride=tt), :])
        return 0

    lax.fori_loop(0, tt // 8, regroup, 0)


def _rwkv_scan_body(w_ref, kk_ref, kka_ref, k_ref, r_ref, v_ref, s0_ref, y_ref, s_ref):
    @pl.when(pl.program_id(0) == 0)
    def _():
        s_ref[...] = s0_ref[...]

    n_vq = s_ref.shape[0]

    def step(t, _):
        vrows = v_ref[t]
        for vq in range(n_vq):
            s = s_ref[vq]
            sa = jnp.sum(s * kk_ref[t], axis=0, keepdims=True)
            s = s * w_ref[t] - sa * kka_ref[t] + vrows[vq:vq + 1, :] * k_ref[t]
            s_ref[vq] = s
            y_ref[t, vq:vq + 1, :] = jnp.sum(s * r_ref[t], axis=0, keepdims=True)
        return 0

    lax.fori_loop(0, v_ref.shape[0], step, 0)


def _rwkv_post_body(y_ref, bonus_ref, gate_ref, lng_ref, lnb_ref, o_ref):
    y = _head_norm(y_ref[...], lng_ref[...], lnb_ref[...], RWKV_LN_EPS)
    o_ref[...] = ((y + bonus_ref[...]) * gate_ref[...]).astype(o_ref.dtype)


def _rwkv_prompt(cols, shift_prev, s0, params, b, t, tc=256, tscan=64):
    mu, w0, w2, a0, a2, g2, k_k, k_a, r_k, ln_g, ln_b = params
    nt = t // tc
    gw = GROUP_WIDTH
    hd = HEAD_DIM
    chains = b * N_HEADS
    rep = LANE // chains
    assert rep * chains == LANE and hd % rep == 0
    n_vq = hd // rep
    full = lambda a: pl.BlockSpec(a.shape, lambda i, n: (0,) * a.ndim)
    tok = pl.BlockSpec((tc, gw), lambda i, n: (i * nt + n, 0))
    vec = jax.ShapeDtypeStruct((b * t, gw), F32)
    prep_in = (mu, w0, w2, a0, a2, g2, k_k, k_a, r_k)
    keys, v, bonus, gate = pl.pallas_call(
        _rwkv_prep_body,
        grid=(b, nt),
        in_specs=[pl.BlockSpec((tc, RWKV_COLS), lambda i, n: (i * nt + n, OFF_RWKV // RWKV_COLS)),
                  pl.BlockSpec((None, 1, RWKV_COLS), lambda i, n: (i, 0, 0))] + [full(a) for a in prep_in],
        out_specs=[pl.BlockSpec((5, gw, tc), lambda i, n: (0, 0, i * nt + n)),
                   pl.BlockSpec((gw, tc), lambda i, n: (0, i * nt + n)), tok, tok],
        out_shape=[jax.ShapeDtypeStruct((5, gw, b * t), F32), jax.ShapeDtypeStruct((gw, b * t), F32), vec, vec],
        scratch_shapes=[pltpu.VMEM((1, RWKV_COLS), F32)],
        compiler_params=_cparams(("parallel", "arbitrary")),
        name="rwkv_prep",
    )(cols, shift_prev.reshape(b, 1, RWKV_COLS), *prep_in)

    tt = LANE
    ntt = t // tt
    keys_t = pl.pallas_call(
        _rwkv_relayout_body,
        grid=(5, ntt),
        in_specs=[pl.BlockSpec((None, gw, tt), lambda j, n, i=i: (j, 0, i * ntt + n)) for i in range(b)],
        out_specs=pl.BlockSpec((None, tt * hd, LANE), lambda j, n: (j, n, 0)),
        out_shape=jax.ShapeDtypeStruct((5, t * hd, LANE), F32),
        scratch_shapes=[pltpu.VMEM((hd * tt, LANE), F32)],
        compiler_params=_cparams(("parallel", "parallel")),
        name="rwkv_relayout",
    )(*([keys] * b)).reshape(5, t, hd, LANE)
    vt = jnp.transpose(v.reshape(N_HEADS, rep, n_vq, b, t), (4, 2, 1, 3, 0)).reshape(t, n_vq, LANE)
    s0t = jnp.transpose(s0.reshape(b, N_HEADS, rep, n_vq, hd), (3, 4, 2, 0, 1)).reshape(n_vq, hd, LANE)

    ns = t // tscan
    st_spec = pl.BlockSpec((n_vq, hd, LANE), lambda n: (0, 0, 0))
    yt, st = pl.pallas_call(
        _rwkv_scan_body,
        grid=(ns,),
        in_specs=[pl.BlockSpec((None, tscan, hd, LANE), lambda n, j=j: (j, n, 0, 0)) for j in range(5)]
                 + [pl.BlockSpec((tscan, n_vq, LANE), lambda n: (n, 0, 0)), st_spec],
        out_specs=[pl.BlockSpec((tscan, n_vq, LANE), lambda n: (n, 0, 0)), st_spec],
        out_shape=[jax.ShapeDtypeStruct((t, n_vq, LANE), F32), jax.ShapeDtypeStruct((n_vq, hd, LANE), F32)],
        compiler_params=_cparams(("arbitrary",)),
        name="rwkv_scan",
    )(*([keys_t] * 5), vt, s0t)
    ys = jnp.transpose(yt.reshape(t, n_vq, rep, b, N_HEADS), (3, 0, 4, 2, 1)).reshape(b * t, gw)
    s_fin = jnp.transpose(st.reshape(n_vq, hd, rep, b, N_HEADS), (3, 4, 2, 0, 1)).reshape(b, N_HEADS, hd, hd)

    tm = min(512, b * t)
    tokm = pl.BlockSpec((tm, gw), lambda i: (i, 0))
    o = pl.pallas_call(
        _rwkv_post_body,
        grid=(b * t // tm,),
        in_specs=[tokm, tokm, tokm, pl.BlockSpec((1, gw), lambda i: (0, 0)), pl.BlockSpec((1, gw), lambda i: (0, 0))],
        out_specs=tokm,
        out_shape=jax.ShapeDtypeStruct((b * t, gw), BF16),
        compiler_params=_cparams(("parallel",)),
        name="rwkv_post",
    )(ys, bonus, gate, ln_g, ln_b)
    return o, s_fin


PAGES_PER_STEP = 16


def _past_stats_body(pt_ref, *refs, n_pages):
    pps = PAGES_PER_STEP
    kt_refs, kct_refs, vct_refs = refs[:pps], refs[pps:2 * pps], refs[2 * pps:3 * pps]
    (q_ref, w1r_ref, w1_ref, b1_ref, w2_ref, pe_ref, top_ref, cmp_ref, qb, gsum, kc_rows, vc_rows) = refs[3 * pps:]
    step = pl.program_id(1)
    pages_per_blk = MOBA_BLOCK // PAGE_SIZE
    nb = n_pages // pages_per_blk
    n_chunk = n_pages * PAGE_SIZE // CMP_STRIDE

    @pl.when(step == 0)
    def _():
        gsum[...] = jnp.zeros_like(gsum)
        qb[...] = jnp.broadcast_to(q_ref[...], qb.shape)

    for i in range(pps):
        p = step * pps + i
        prod = (kt_refs[i][...] * qb[...]).reshape(N_HEADS, HEAD_DIM // 8, 8, PAGE_SIZE)
        gsum[p // pages_per_blk] += jnp.sum(prod, axis=1)
        row0 = pl.multiple_of(p * PAGE_SIZE, PAGE_SIZE)
        kc_rows[pl.ds(row0, PAGE_SIZE), :] = kct_refs[i][...].reshape(LANE, PAGE_SIZE).T
        vc_rows[pl.ds(row0, PAGE_SIZE), :] = vct_refs[i][...].reshape(LANE, PAGE_SIZE).T

    @pl.when(step == n_pages // pps - 1)
    def _():
        gate = jnp.sum(jnp.sum(gsum[...], axis=2), axis=-1) * (1.0 / MOBA_BLOCK)
        ids = lax.broadcasted_iota(I32, (nb, N_HEADS), 0)
        rows8 = lax.broadcasted_iota(I32, (8, N_HEADS), 0)
        top = jnp.zeros((8, N_HEADS), I32)
        for j in range(MOBA_TOPK):
            best = jnp.max(gate, axis=0, keepdims=True)
            arg = jnp.min(jnp.where(gate == best, ids, nb), axis=0, keepdims=True)
            top = jnp.where(rows8 == j, arg, top)
            gate = jnp.where(ids == arg, -jnp.inf, gate)
        top_ref[...] = top
        for kv, src in enumerate((kc_rows, vc_rows)):
            bias = _cmp_bias(pe_ref, w1_ref, b1_ref, kv)
            out = _compress(lambda r: src[pl.ds(r, n_chunk, stride=CMP_STRIDE), :], n_chunk, w1r_ref, kv, bias,
                            w2_ref[kv])
            for g in range(NSA_KV_HEADS):
                cmp_ref[kv * NSA_KV_HEADS + g] = out[g]


def _past_stats(page_table, moba_t, nsa_t, q_col, cmp_w, layer):
    w1r, w1, b1, w2, pe = cmp_w
    bs, n_pages = page_table.shape
    nb = n_pages * PAGE_SIZE // MOBA_BLOCK
    rows = n_pages * PAGE_SIZE
    n_chunk = rows // CMP_STRIDE
    pps = PAGES_PER_STEP
    assert n_pages % pps == 0
    full = lambda a: pl.BlockSpec(a.shape, lambda i, p, pt: (0,) * a.ndim)
    k_page = lambda j: pl.BlockSpec((None, None, None, N_HEADS, HEAD_DIM, PAGE_SIZE),
                                    lambda i, p, pt: (layer, pt[i, p * pps + j], 0, 0, 0, 0))
    nsa_page = lambda t, j: pl.BlockSpec((None, None, None, NSA_KV_HEADS, HEAD_DIM, PAGE_SIZE),
                                         lambda i, p, pt: (layer, pt[i, p * pps + j], t, 0, 0, 0))
    return pl.pallas_call(
        functools.partial(_past_stats_body, n_pages=n_pages),
        grid_spec=pltpu.PrefetchScalarGridSpec(
            num_scalar_prefetch=1,
            grid=(bs, n_pages // pps),
            in_specs=[k_page(j) for j in range(pps)] + [nsa_page(0, j) for j in range(pps)]
                     + [nsa_page(1, j) for j in range(pps)]
                     + [pl.BlockSpec((None, N_HEADS, HEAD_DIM, 1), lambda i, p, pt: (i, 0, 0, 0)),
                        full(w1r), full(w1), full(b1), full(w2), full(pe)],
            out_specs=[pl.BlockSpec((None, 8, N_HEADS), lambda i, p, pt: (i, 0, 0)),
                       pl.BlockSpec((None, 2 * NSA_KV_HEADS, n_chunk, HEAD_DIM), lambda i, p, pt: (i, 0, 0, 0))],
            scratch_shapes=[pltpu.VMEM((N_HEADS, HEAD_DIM, PAGE_SIZE), F32), pltpu.VMEM((nb, N_HEADS, 8, PAGE_SIZE), F32),
                            pltpu.VMEM((rows, LANE), F32), pltpu.VMEM((rows, LANE), F32)]),
        out_shape=[jax.ShapeDtypeStruct((bs, 8, N_HEADS), I32),
                   jax.ShapeDtypeStruct((bs, 2 * NSA_KV_HEADS, n_chunk, HEAD_DIM), F32)],
        compiler_params=_cparams(("parallel", "arbitrary")),
        name="past_stats",
    )(page_table, *([moba_t] * pps), *([nsa_t] * (2 * pps)), q_col, w1r, w1, b1, w2, pe)


def _moba_sample_body(pt_ref, top_ref, *refs):
    n_pg = (len(refs) - 4) // 2
    kt_refs, vt_refs = refs[:n_pg], refs[n_pg:2 * n_pg]
    q_ref, kn_ref, vn_ref, o_ref = refs[2 * n_pg:]
    q = q_ref[...] * ATTN_SCALE
    q8 = jnp.broadcast_to(q, (8, HEAD_DIM)).astype(BF16)
    s_all = [_dot(q8, kt[...].astype(BF16)) for kt in kt_refs]
    s_self = jnp.sum(q * kn_ref[...], axis=-1, keepdims=True)
    m = s_self
    for s in s_all:
        m = jnp.maximum(m, jnp.max(s, axis=-1, keepdims=True))
    e_self = jnp.exp(s_self - m)
    den = e_self
    acc = e_self * vn_ref[...]
    for s, vt in zip(s_all, vt_refs):
        pr = jnp.exp(s - m)
        den = den + jnp.sum(pr, axis=-1, keepdims=True)
        acc = acc + _dot_nt(pr.astype(BF16), vt[...].astype(BF16))
    o_ref[...] = (acc / den)[0:1, :]


def _moba_sample(page_table, top, moba_t, q, k_new, v_new, layer):
    bs = page_table.shape[0]
    ppb = MOBA_BLOCK // PAGE_SIZE

    def page_spec(kv, j, r):
        return pl.BlockSpec((None, None, None, None, HEAD_DIM, PAGE_SIZE),
                            lambda i, h, pt, tp: (layer, pt[i, tp[i, j, h] * ppb + r], kv, h, 0, 0))

    pages = [(j, r) for j in range(MOBA_TOPK) for r in range(ppb)]
    head = pl.BlockSpec((None, None, 1, HEAD_DIM), lambda i, h, pt, tp: (i, h, 0, 0))
    return pl.pallas_call(
        _moba_sample_body,
        grid_spec=pltpu.PrefetchScalarGridSpec(
            num_scalar_prefetch=2,
            grid=(bs, N_HEADS),
            in_specs=[page_spec(0, j, r) for j, r in pages] + [page_spec(1, j, r) for j, r in pages]
                     + [head, head, head],
            out_specs=head),
        out_shape=jax.ShapeDtypeStruct((bs, N_HEADS, 1, HEAD_DIM), F32),
        compiler_params=_cparams(("parallel", "parallel")),
        name="moba_sample",
    )(page_table, top, *([moba_t] * (2 * len(pages))), q, k_new, v_new)


def _heads_to_rows(row, g):
    parts = [row[:, (g * NSA_GROUP + j) * HEAD_DIM:(g * NSA_GROUP + j + 1) * HEAD_DIM] for j in range(NSA_GROUP)]
    return jnp.concatenate(parts + [jnp.zeros((8 - NSA_GROUP, HEAD_DIM), F32)], axis=0)


def _nsa_sample_sel_body(q_ref, cmp_ref, win_ref, kvn_ref, gate_ref, idx_ref, part_ref, *, past):
    hd = HEAD_DIM
    n_chunk = past // CMP_STRIDE
    n_slc = past // SLC_BLOCK + 1
    n_slc_pad = -(-n_slc // LANE) * LANE
    cur = past // SLC_BLOCK
    qrow = q_ref[...]
    gates = _sigmoid(gate_ref[...])
    kvn = kvn_ref[...]
    rows8 = lax.broadcasted_iota(I32, (8, 1), 0)
    ids = lax.broadcasted_iota(I32, (1, n_slc_pad), 1)
    m_iota = lax.broadcasted_iota(I32, (n_slc_pad, n_slc_pad), 0)
    j_iota = lax.broadcasted_iota(I32, (n_slc_pad, n_slc_pad), 1)
    lane = lax.broadcasted_iota(I32, (1, LANE), 1)
    idx_out = jnp.zeros((8, LANE), I32)
    for g in range(NSA_KV_HEADS):
        q8 = _heads_to_rows(qrow, g)
        q8b = q8.astype(BF16)
        n_ids = lax.broadcasted_iota(I32, (8, n_chunk), 1)
        l_cmp = _dot_nt(q8b, cmp_ref[g].astype(BF16)) * ATTN_SCALE
        p_cmp = _softmax_rows(l_cmp, n_ids * CMP_STRIDE + (CMP_LEN - 1) <= past)
        o_cmp = _dot(p_cmp.astype(BF16), cmp_ref[NSA_KV_HEADS + g].astype(BF16))
        imp = jnp.sum(jnp.where(rows8 < NSA_GROUP, p_cmp, 0.0), axis=0, keepdims=True)
        imp_slc = _dot(jnp.broadcast_to(imp, (8, n_chunk)), _slc_matrix(n_chunk, n_chunk - 1, n_slc_pad), HI)
        eligible = (ids <= cur) & (ids < n_slc)
        forced = (ids == 0) | (ids == cur) | (ids == cur - 1)
        score = jnp.where(eligible, jnp.where(forced, jnp.inf, imp_slc[0:1, :]), -jnp.inf)
        s_col = jnp.broadcast_to(score, (LANE, n_slc_pad)).T[:, 0:1]
        beats = (s_col > score) | ((s_col == score) & (m_iota < j_iota))
        rank = jnp.sum(beats.astype(F32), axis=0, keepdims=True)
        sel = jnp.where(eligible & (rank < SLC_TOPK), 1.0, 0.0)
        sel_col = jnp.broadcast_to(sel, (LANE, n_slc_pad)).T[:, 0:1]
        before = jnp.sum(jnp.where(m_iota < j_iota, sel_col, 0.0), axis=0, keepdims=True)
        idx_row = jnp.zeros((1, LANE), I32)
        for i in range(SLC_TOPK):
            hit = (sel > 0.5) & (before == i)
            idx_i = jnp.sum(jnp.where(hit, ids, 0), axis=-1, keepdims=True)
            idx_row = jnp.where(lane == i, idx_i, idx_row)
        idx_out = jnp.where(lax.broadcasted_iota(I32, (8, LANE), 0) == g, idx_row, idx_out)
        kw = win_ref[0, g].astype(BF16)
        vw = win_ref[1, g].astype(BF16)
        n_buf = win_ref.shape[-1]
        w_ids = lax.broadcasted_iota(I32, (8, n_buf), 1)
        l_win = _dot(q8b, kw) * ATTN_SCALE
        w_mask = w_ids >= n_buf - (WINDOW - 1)
        kw_new = kvn[:, 4 * NSA_KV_WIDTH + g * hd:4 * NSA_KV_WIDTH + (g + 1) * hd]
        vw_new = kvn[:, 5 * NSA_KV_WIDTH + g * hd:5 * NSA_KV_WIDTH + (g + 1) * hd]
        s_self = jnp.sum(q8 * kw_new, axis=-1, keepdims=True) * ATTN_SCALE
        m = jnp.maximum(jnp.max(jnp.where(w_mask, l_win, NEG_INF), axis=-1, keepdims=True), s_self)
        e = jnp.where(w_mask, jnp.exp(l_win - m), 0.0)
        e_self = jnp.exp(s_self - m)
        o_win = (_dot_nt(e.astype(BF16), vw) + e_self * vw_new) / (jnp.sum(e, axis=-1, keepdims=True) + e_self)
        for j in range(NSA_GROUP):
            h = g * NSA_GROUP + j
            part_ref[:, h * hd:(h + 1) * hd] = (gates[:, h:h + 1] * o_cmp[j:j + 1]
                                                + gates[:, 2 * N_HEADS + h:2 * N_HEADS + h + 1] * o_win[j:j + 1])
    idx_ref[...] = idx_out


def _nsa_sample_sel(q, cmp_tok, win_state, kv_new, gates, layer, past):
    bs = q.shape[0]
    n_buf = win_state.shape[-1]
    n_chunk = cmp_tok.shape[2]
    return pl.pallas_call(
        functools.partial(_nsa_sample_sel_body, past=past),
        grid=(bs,),
        in_specs=[pl.BlockSpec((None, 1, GROUP_WIDTH), lambda i: (i, 0, 0)),
                  pl.BlockSpec((None, 2 * NSA_KV_HEADS, n_chunk, HEAD_DIM), lambda i: (i, 0, 0, 0)),
                  pl.BlockSpec((None, None, 2, NSA_KV_HEADS, HEAD_DIM, n_buf), lambda i: (layer, i, 0, 0, 0, 0)),
                  pl.BlockSpec((None, 1, 6 * NSA_KV_WIDTH), lambda i: (i, 0, 0)),
                  pl.BlockSpec((None, 1, LANE), lambda i: (i, 0, 0))],
        out_specs=[pl.BlockSpec((None, 8, LANE), lambda i: (i, 0, 0)),
                   pl.BlockSpec((None, 1, GROUP_WIDTH), lambda i: (i, 0, 0))],
        out_shape=[jax.ShapeDtypeStruct((bs, 8, LANE), I32), jax.ShapeDtypeStruct((bs, 1, GROUP_WIDTH), F32)],
        compiler_params=_cparams(("parallel",)),
        name="nsa_sample_sel",
    )(q, cmp_tok, win_state, kv_new, gates)


NSA_BLOCKS_PER_STEP = 8


def _nsa_sample_attn_body(pt_ref, idx_ref, *refs, n_past_blk):
    bps = NSA_BLOCKS_PER_STEP
    n_pg = NSA_KV_HEADS * bps
    ks_refs, vs_refs = refs[:n_pg], refs[n_pg:2 * n_pg]
    q_ref, kvn_ref, gate_ref, part_ref, o_ref, m_s, l_s, acc_s = refs[2 * n_pg:]
    b, i = pl.program_id(0), pl.program_id(1)
    hd = HEAD_DIM
    qrow = q_ref[...]
    kvn = kvn_ref[...]
    half = lax.broadcasted_iota(I32, (8, PAGE_SIZE), 1) // SLC_BLOCK
    for g in range(NSA_KV_HEADS):
        q8 = _heads_to_rows(qrow, g)
        ks_new = kvn[:, 2 * NSA_KV_WIDTH + g * hd:2 * NSA_KV_WIDTH + (g + 1) * hd]
        vs_new = kvn[:, 3 * NSA_KV_WIDTH + g * hd:3 * NSA_KV_WIDTH + (g + 1) * hd]

        @pl.when(i == 0)
        def _():
            m_s[g] = jnp.broadcast_to(jnp.sum(q8 * ks_new, axis=-1, keepdims=True) * ATTN_SCALE, (8, hd))
            l_s[g] = jnp.ones((8, hd), F32)
            acc_s[g] = jnp.broadcast_to(vs_new, (8, hd))

        q8b = (q8 * ATTN_SCALE).astype(BF16)
        m_i = m_s[g][:, 0:1]
        l_i = l_s[g][:, 0:1]
        acc = acc_s[g]
        for u in range(bps):
            blk = idx_ref[b, g, i * bps + u]
            mask = half == jnp.where(blk < n_past_blk, blk % (PAGE_SIZE // SLC_BLOCK), -1)
            s = jnp.where(mask, _dot(q8b, ks_refs[g * bps + u][...].astype(BF16)), NEG_INF)
            m_new = jnp.maximum(m_i, jnp.max(s, axis=-1, keepdims=True))
            pr = jnp.where(mask, jnp.exp(s - m_new), 0.0)
            alpha = jnp.exp(m_i - m_new)
            l_i = alpha * l_i + jnp.sum(pr, axis=-1, keepdims=True)
            acc = alpha * acc + _dot_nt(pr.astype(BF16), vs_refs[g * bps + u][...].astype(BF16))
            m_i = m_new
        l_s[g] = jnp.broadcast_to(l_i, (8, hd))
        acc_s[g] = acc
        m_s[g] = jnp.broadcast_to(m_i, (8, hd))

    @pl.when(i == pl.num_programs(1) - 1)
    def _():
        gates = _sigmoid(gate_ref[...])
        for g in range(NSA_KV_HEADS):
            o_sel = acc_s[g] / l_s[g]
            for j in range(NSA_GROUP):
                h = g * NSA_GROUP + j
                sl = slice(h * hd, (h + 1) * hd)
                o_ref[:, sl] = (part_ref[:, sl] + gates[:, N_HEADS + h:N_HEADS + h + 1] * o_sel[j:j + 1]
                                ).astype(o_ref.dtype)


def _nsa_sample_attn(page_table, sel_idx, nsa_cache, q, kv_new, gates, part, layer, past):
    bs = page_table.shape[0]
    n_past_blk = past // SLC_BLOCK
    per_page = PAGE_SIZE // SLC_BLOCK

    bps = NSA_BLOCKS_PER_STEP

    def blk_spec(g, u, t):
        def imap(b, i, pt, ix):
            blk = jnp.minimum(ix[b, g, i * bps + u], n_past_blk - 1)
            return (layer, pt[b, blk // per_page], t, g, 0, 0)
        return pl.BlockSpec((None, None, None, None, HEAD_DIM, PAGE_SIZE), imap)

    row = lambda w: pl.BlockSpec((None, 1, w), lambda b, i, pt, ix: (b, 0, 0))
    gu = [(g, u) for g in range(NSA_KV_HEADS) for u in range(bps)]
    return pl.pallas_call(
        functools.partial(_nsa_sample_attn_body, n_past_blk=n_past_blk),
        grid_spec=pltpu.PrefetchScalarGridSpec(
            num_scalar_prefetch=2,
            grid=(bs, SLC_TOPK // bps),
            in_specs=[blk_spec(g, u, 2) for g, u in gu] + [blk_spec(g, u, 3) for g, u in gu]
                     + [row(GROUP_WIDTH), row(6 * NSA_KV_WIDTH), row(LANE), row(GROUP_WIDTH)],
            out_specs=row(GROUP_WIDTH),
            scratch_shapes=[pltpu.VMEM((NSA_KV_HEADS, 8, HEAD_DIM), F32)] * 3),
        out_shape=jax.ShapeDtypeStruct((bs, 1, GROUP_WIDTH), F32),
        compiler_params=_cparams(("parallel", "arbitrary")),
        name="nsa_sample_attn",
    )(page_table, sel_idx, *([nsa_cache] * (2 * len(gu))), q, kv_new, gates, part)


def _recur_sample_body(c_ref, shift_ref, mu_ref, w0_ref, w2_ref, a0_ref, a2_ref, g2_ref, kk_ref, ka_ref, rk_ref,
                       lng_ref, lnb_ref, s_rw_ref, ret_ref, cos_ref, sin_ref, gam_ref, gn_g_ref, gn_b_ref, s_rt_ref,
                       o_rw_ref, s_rw_out, o_rt_ref, s_rt_out, o_scr):
    gw = GROUP_WIDTH
    hd = HEAD_DIM
    p = (mu_ref[...], w0_ref[...], w2_ref[...], a0_ref[...], a2_ref[...], g2_ref[...], kk_ref[...], ka_ref[...])
    r, w, k, v, kk, kka, gate = _rwkv_prep(c_ref[...], shift_ref[...], p)

    def put_state(h, j, s):
        s_rw_out[j, h] = s

    ys = _rwkv_steps((r, w, k, v, kk, kka), lambda h, j: s_rw_ref[j, h], put_state, 1)
    o_rw_ref[...] = _rwkv_post(ys, r, k, v, gate, rk_ref[...], lng_ref[...], lnb_ref[...]).astype(o_rw_ref.dtype)

    cos = cos_ref[...]
    sin = sin_ref[...]
    q = _rope(ret_ref[:, 0:gw], cos, sin)
    kr = _rope(ret_ref[:, gw:2 * gw], cos, sin) * ATTN_SCALE
    vr = ret_ref[:, 2 * gw:3 * gw]
    gam = gam_ref[...]
    qk = _dot_split(q * kr, _head_blockdiag(gw))
    pad = jnp.zeros((LANE - 8, LANE), F32)
    for hp in range(N_HEADS // 2):
        ps = slice(hp * LANE, (hp + 1) * LANE)
        qt = jnp.concatenate([q[:, ps], pad], axis=0).T
        kt = jnp.concatenate([kr[:, ps], pad], axis=0).T
        for h2 in range(2):
            h = 2 * hp + h2
            sl = slice(h * hd, (h + 1) * hd)
            for j in range(8):
                s = s_rt_ref[j, h]
                qcol = qt[h2 * hd:(h2 + 1) * hd, j:j + 1]
                kcol = kt[h2 * hd:(h2 + 1) * hd, j:j + 1]
                g_h = gam[:, sl]
                o_scr[j:j + 1, sl] = (qk[j:j + 1, sl] * vr[j:j + 1, sl]
                                      + g_h * jnp.sum(qcol * s, axis=0, keepdims=True))
                s_rt_out[j, h] = s * g_h + kcol * vr[j:j + 1, sl]
    gt = ret_ref[:, 3 * gw:4 * gw]
    y = _head_norm(o_scr[...], gn_g_ref[...], gn_b_ref[...], GN_EPS)
    o_rt_ref[...] = (gt * _sigmoid(gt) * y).astype(o_rt_ref.dtype)


def _recur_sample(c_rwkv, shift, rwkv_params, s_rwkv, c_ret, rope, gamma, gn_g, gn_b, s_ret):
    bs = c_rwkv.shape[0]
    cos, sin = rope
    st = jax.ShapeDtypeStruct((bs, N_HEADS, HEAD_DIM, HEAD_DIM), F32)
    ob = jax.ShapeDtypeStruct((bs, GROUP_WIDTH), BF16)
    return pl.pallas_call(
        _recur_sample_body,
        out_shape=[ob, st, ob, st],
        scratch_shapes=[pltpu.VMEM((bs, GROUP_WIDTH), F32)],
        compiler_params=pltpu.CompilerParams(vmem_limit_bytes=VMEM_LIMIT),
        name="recur_sample",
    )(c_rwkv, shift, *rwkv_params, s_rwkv, c_ret, cos, sin, gamma, gn_g, gn_b, s_ret)


def _prep_rwkv_params(mu, w0, w2, a0, a2, g2, k_k, k_a, r_k, ln_g, ln_b):
    row = lambda z: z.reshape(1, -1)
    return (row(mu), row(w0), w2, row(a0), a2, g2, row(k_k), row(k_a), row(r_k), row(ln_g), row(ln_b))


def _prep_cmp_weights(w1, b1, w2, pe):
    span = CMP_LEN // CMP_STRIDE
    w1r = w1.reshape(2, span, CMP_STRIDE, HEAD_DIM, CMP_HIDDEN)
    w1r = jnp.transpose(w1r, (0, 2, 3, 1, 4)).reshape(2, CMP_STRIDE, HEAD_DIM, span * CMP_HIDDEN).astype(BF16)
    return (w1r, w1, b1.reshape(2, 1, CMP_HIDDEN), w2, pe.reshape(2, 1, CMP_LEN * HEAD_DIM))


def _pad_w_in(w):
    o_nsa = MOBA_COLS
    o_gate = o_nsa + GROUP_WIDTH + 6 * NSA_KV_WIDTH
    o_rwkv = o_nsa + NSA_COLS
    o_ret = o_rwkv + RWKV_COLS
    zeros = jnp.zeros((w.shape[0], N_PAD - OFF_GATE - 3 * N_HEADS), w.dtype)
    return jnp.concatenate([w[:, o_rwkv:o_ret], w[:, :o_gate], w[:, o_ret:], w[:, o_gate:o_rwkv], zeros], axis=1)


def kernel(x_prompt, x_sample, cache_moba_kv, cache_nsa_kv, state_nsa_win, state_rwkv, state_rwkv_shift, state_ret,
           page_table, norm_g, w_in, w_out, w_up, w_down, nsa_cmp_pe, nsa_cmp_w1, nsa_cmp_b1, nsa_cmp_w2, rwkv_mu,
           rwkv_w0, rwkv_w2, rwkv_a0, rwkv_a2, rwkv_g2, rwkv_k_k, rwkv_k_a, rwkv_r_k, rwkv_ln_g, rwkv_ln_b, ret_gn_g,
           ret_gn_b):
    bp, t, d = x_prompt.shape
    bs = x_sample.shape[0]
    assert x_sample.shape[1] == 1 and d == D_MODEL
    depth = w_in.shape[0]
    past = page_table.shape[1] * PAGE_SIZE
    gw = GROUP_WIDTH
    hd = HEAD_DIM

    log_gamma = jnp.log(1.0 - jnp.exp2(-5.0 - jnp.arange(N_HEADS, dtype=F32)))
    ret_tables = _ret_tables(log_gamma, RET_CHUNK)
    rope_p = _rope_tables(jnp.arange(t, dtype=I32))
    rope_s = _rope_tables(jnp.full((1,), past, I32))
    gamma_row = jnp.repeat(jnp.exp(log_gamma), hd)[None, :]
    moba_t = jnp.transpose(cache_moba_kv, (0, 1, 3, 4, 5, 2))
    nsa_t = jnp.transpose(cache_nsa_kv, (0, 1, 3, 4, 5, 2))
    win_t = jnp.transpose(state_nsa_win, (0, 1, 3, 4, 5, 2))

    xp = x_prompt.reshape(bp * t, d)
    xs = x_sample.reshape(bs, d)
    zero_state = jnp.zeros((bp, N_HEADS, hd, hd), F32)
    zero_shift = jnp.zeros((bp, RWKV_COLS), F32)
    st_p, st_s = [], []
    for l in range(depth):
        g = norm_g[l].reshape(4, 1, d)
        w_in_l = _pad_w_in(w_in[l]).astype(BF16)
        w_out_l = w_out[l].astype(BF16)
        w_up_l = w_up[l].astype(BF16)
        w_down_l = w_down[l].astype(BF16)
        cmp_w = _prep_cmp_weights(nsa_cmp_w1[l], nsa_cmp_b1[l], nsa_cmp_w2[l], nsa_cmp_pe[l])
        rwkv_p = _prep_rwkv_params(rwkv_mu[l], rwkv_w0[l], rwkv_w2[l], rwkv_a0[l], rwkv_a2[l], rwkv_g2[l],
                                   rwkv_k_k[l], rwkv_k_a[l], rwkv_r_k[l], rwkv_ln_g[l], rwkv_ln_b[l])
        gn_g = ret_gn_g[l].reshape(1, gw)
        gn_b = ret_gn_b[l].reshape(1, gw)

        cols = _inproj(xp, g[0], w_in_l, min(1024, bp * t), 1024)
        o_moba = _moba_prompt(cols, bp, t)
        o_nsa = _nsa_prompt(cols, cmp_w, bp, t)
        o_rwkv, rwkv_s = _rwkv_prompt(cols, zero_shift, zero_state, rwkv_p, bp, t)
        o_ret, ret_s = _ret_prompt(cols, zero_state, ret_tables, rope_p, gn_g, gn_b, bp, t)
        xp = _outproj((o_moba, o_nsa, o_rwkv, o_ret), xp, g[1], w_out_l, 256)
        xp = _ffn(xp, g[2], g[3], w_up_l, w_down_l, min(512, bp * t), 1024)
        c3 = cols.reshape(bp, t, N_PAD)
        win_keep = min(WINDOW, t)
        st_p.append((c3[:, :, OFF_MOBA + gw:OFF_MOBA + 3 * gw].reshape(bp, t, 2, N_HEADS, hd),
                     c3[:, :, OFF_NSAKV:OFF_NSAKV + 4 * NSA_KV_WIDTH].reshape(bp, t, 4, NSA_KV_HEADS, hd),
                     c3[:, t - win_keep:, OFF_NSAKV + 4 * NSA_KV_WIDTH:OFF_NSAKV + 6 * NSA_KV_WIDTH
                        ].reshape(bp, win_keep, 2, NSA_KV_HEADS, hd),
                     rwkv_s, c3[:, t - 1, OFF_RWKV:OFF_RWKV + RWKV_COLS], ret_s))

        cs = _inproj(xs, g[0], w_in_l, bs, 512)
        heads = lambda z: z.reshape(bs, N_HEADS, 1, hd)
        q_m = cs[:, OFF_MOBA:OFF_MOBA + gw]
        k_m = cs[:, OFF_MOBA + gw:OFF_MOBA + 2 * gw]
        v_m = cs[:, OFF_MOBA + 2 * gw:OFF_MOBA + 3 * gw]
        top, cmp_tok = _past_stats(page_table, moba_t, nsa_t, q_m.reshape(bs, N_HEADS, hd, 1), cmp_w, l)
        o_moba_s = _moba_sample(page_table, top[:, :MOBA_TOPK, :], moba_t, heads(q_m), heads(k_m), heads(v_m), l)
        q_n = cs[:, OFF_NSAQ:OFF_NSAQ + gw].reshape(bs, 1, gw)
        kv_new = cs[:, OFF_NSAKV:OFF_NSAKV + 6 * NSA_KV_WIDTH].reshape(bs, 1, 6 * NSA_KV_WIDTH)
        gates = cs[:, OFF_GATE:OFF_GATE + LANE].reshape(bs, 1, LANE)
        sel_idx, part = _nsa_sample_sel(q_n, cmp_tok, win_t, kv_new, gates, l, past)
        o_nsa_s = _nsa_sample_attn(page_table, sel_idx[:, :NSA_KV_HEADS, :SLC_TOPK], nsa_t, q_n, kv_new, gates, part,
                                   l, past)
        o_rwkv_s, rwkv_s_s, o_ret_s, ret_s_s = _recur_sample(
            cs[:, OFF_RWKV:OFF_RWKV + RWKV_COLS], state_rwkv_shift[l], rwkv_p, state_rwkv[l],
            cs[:, OFF_RET:OFF_RET + RET_COLS], rope_s, gamma_row, gn_g, gn_b, state_ret[l])
        parts_s = (o_moba_s.reshape(bs, gw).astype(BF16), o_nsa_s.reshape(bs, gw).astype(BF16), o_rwkv_s, o_ret_s)
        xs = _outproj(parts_s, xs, g[1], w_out_l, bs)
        xs = _ffn(xs, g[2], g[3], w_up_l, w_down_l, bs, 512)
        win_new = cs[:, OFF_NSAKV + 4 * NSA_KV_WIDTH:OFF_NSAKV + 6 * NSA_KV_WIDTH].reshape(bs, 1, 2, NSA_KV_HEADS, hd)
        win_all = jnp.concatenate([state_nsa_win[l], win_new], axis=1)
        keep_s = min(WINDOW, win_all.shape[1])
        st_s.append((cs[:, OFF_MOBA + gw:OFF_MOBA + 3 * gw].reshape(bs, 1, 2, N_HEADS, hd),
                     cs[:, OFF_NSAKV:OFF_NSAKV + 4 * NSA_KV_WIDTH].reshape(bs, 1, 4, NSA_KV_HEADS, hd),
                     win_all[:, win_all.shape[1] - keep_s:],
                     rwkv_s_s, cs[:, OFF_RWKV:OFF_RWKV + RWKV_COLS], ret_s_s))

    stk = lambda sts, i: jnp.stack([s[i] for s in sts], axis=0)
    outs = [xp.reshape(bp, t, d), xs.reshape(bs, 1, d)]
    for i in range(6):
        outs += [stk(st_p, i), stk(st_s, i)]
    return tuple(outs)
```
